```python
import jax, jax.numpy as jnp
from jax import lax
import numpy as np

D_MODEL = 4096
BATCH = 1
SEQ = 16384
DEPTH = 1
DEC_BATCH = 16
DEC_SEQ = 32
PAST_LEN = 2048

CHUNK = 64
MLP_CHUNK = 128
HEAD_DIM = 128
MIX_WIDTH = D_MODEL
D_A = MIX_WIDTH // 2
D_B = MIX_WIDTH - D_A
G_A = D_A // HEAD_DIM
N_HEADS = D_B // HEAD_DIM
N_KV = 4
IDX_HEADS = 32
IDX_DIM = 128
TOPK_MAX = 256
QBLK = 128
ROPE_THETA = 10000.0
EPS = 1e-6

SPLITS = (D_A, D_A, D_A,
          N_HEADS * HEAD_DIM, N_KV * HEAD_DIM, N_KV * HEAD_DIM,
          D_B,
          IDX_HEADS * IDX_DIM, IDX_DIM, IDX_HEADS)
D_IN = sum(SPLITS)

kernel_name = 'hymba_chunkmlp_dsa_stream_step'


def rms_norm(x, g):
    x32 = x.astype(jnp.float32)
    y = x32 * lax.rsqrt(jnp.mean(x32 * x32, axis=-1, keepdims=True) + EPS)
    return (y * g.astype(jnp.float32)).astype(x.dtype)


def layer_norm_heads(v, g):
    v32 = v.astype(jnp.float32)
    mu = jnp.mean(v32, axis=-1, keepdims=True)
    c = v32 - mu
    var = jnp.mean(c * c, axis=-1, keepdims=True)
    return (c * lax.rsqrt(var + EPS) * g.astype(jnp.float32)).astype(v.dtype)


def rope(x, pos):
    d = x.shape[-1]
    half = d // 2
    inv = ROPE_THETA ** (-2.0 * jnp.arange(half, dtype=jnp.float32) / d)
    ang = pos.astype(jnp.float32)[:, None] * inv[None, :]
    cos = jnp.cos(ang)[:, None, :]
    sin = jnp.sin(ang)[:, None, :]
    x32 = x.astype(jnp.float32)
    x1, x2 = x32[..., :half], x32[..., half:]
    return jnp.concatenate([x1 * cos - x2 * sin, x1 * sin + x2 * cos], axis=-1).astype(x.dtype)


def in_proj(x, g_norm, w_in):
    h = rms_norm(x, g_norm) @ w_in
    points = np.cumsum(SPLITS)[:-1].tolist()
    return jnp.split(h, points, axis=-1)


def spatial_gate(u, v_n, w_s, b_s):
    T = u.shape[2]
    p = jnp.arange(T)
    mask = (p[None, :] // CHUNK) <= (p[:, None] // CHUNK)
    w = jnp.where(mask[None], w_s[:, :T, :T], jnp.zeros_like(w_s[:, :T, :T]))
    mixed = jnp.einsum('gij,bcjgd->bcigd', w, v_n) + b_s[:, :T].T[None, None, :, :, None]
    return u * mixed


def sparse_attend(q, qi, wi, q_pos, k, v, ki, k_pos, k_sel):
    score = jnp.einsum('bthd,bsd->bths', qi, ki)
    score = jnp.einsum('bths,bth->bts', jax.nn.relu(score), wi)
    admissible = (k_pos[None, :] // CHUNK) <= (q_pos[:, None] // CHUNK)
    score = jnp.where(admissible[None], score, -jnp.inf)
    top_val, top_idx = lax.top_k(score, k_sel)
    valid = jnp.isfinite(top_val)
    gather = jax.vmap(lambda rows, idx: rows[idx])
    k_g = gather(k, top_idx)
    v_g = gather(v, top_idx)
    B, T, H, D = q.shape
    qg = q.reshape(B, T, N_KV, H // N_KV, D)
    s = jnp.einsum('bthgd,btnhd->bthgn', qg, k_g).astype(jnp.float32) * (D ** -0.5)
    s = jnp.where(valid[:, :, None, None, :], s, -jnp.inf)
    p = jax.nn.softmax(s, axis=-1).astype(v.dtype)
    o = jnp.einsum('bthgn,btnhd->bthgd', p, v_g)
    return o.reshape(B, T, H * D)


def attn_inputs(q, k, v, qi, ki, wi, pos):
    B, T = q.shape[:2]
    q = rope(q.reshape(B, T, N_HEADS, HEAD_DIM), pos)
    k = rope(k.reshape(B, T, N_KV, HEAD_DIM), pos)
    v = v.reshape(B, T, N_KV, HEAD_DIM)
    qi = rope(qi.reshape(B, T, IDX_HEADS, IDX_DIM), pos)
    ki = rope(ki[:, :, None, :], pos)[:, :, 0, :]
    wi = wi * ((IDX_HEADS * IDX_DIM) ** -0.5)
    return q, k, v, qi, ki, wi


def prompt_layer(x, g_norm, w_in, w_s, b_s, g_v, w_out):
    B, S, _ = x.shape
    pos = jnp.arange(S)
    u, va, za, q, k, v, zb, qi, ki, wi = in_proj(x, g_norm, w_in)
    shp = (B, S // MLP_CHUNK, MLP_CHUNK, G_A, HEAD_DIM)
    v_n = layer_norm_heads(va.reshape(B, S, G_A, HEAD_DIM), g_v)
    a = spatial_gate(u.reshape(shp), v_n.reshape(shp), w_s, b_s).reshape(B, S, D_A)
    q, k, v, qi, ki, wi = attn_inputs(q, k, v, qi, ki, wi, pos)
    k_sel = min(TOPK_MAX, S // 4)
    nb = S // QBLK

    def blocks(t):
        return jnp.moveaxis(t.reshape(B, nb, QBLK, *t.shape[2:]), 1, 0)

    def body(args):
        qb, qib, wib, pb = args
        return sparse_attend(qb, qib, wib, pb, k, v, ki, pos, k_sel)

    o = lax.map(body, (blocks(q), blocks(qi), blocks(wi), pos.reshape(nb, QBLK)))
    o = jnp.moveaxis(o, 0, 1).reshape(B, S, D_B)
    mix = jnp.concatenate([a * jax.nn.silu(za), o * jax.nn.silu(zb)], axis=-1)
    y = x + mix @ w_out
    return y, k, v, ki


def sample_layer(x, cache_k, cache_v, cache_ki, g_norm, w_in, w_s, b_s, g_v, w_out):
    B, T, _ = x.shape
    P = cache_k.shape[1]
    pos = P + jnp.arange(T)
    u, va, za, q, k, v, zb, qi, ki, wi = in_proj(x, g_norm, w_in)
    v_n = layer_norm_heads(va.reshape(B, T, G_A, HEAD_DIM), g_v)
    a = spatial_gate(u.reshape(B, 1, T, G_A, HEAD_DIM), v_n[:, None], w_s, b_s).reshape(B, T, D_A)
    q, k, v, qi, ki, wi = attn_inputs(q, k, v, qi, ki, wi, pos)
    k_all = jnp.concatenate([cache_k, k], axis=1)
    v_all = jnp.concatenate([cache_v, v], axis=1)
    ki_all = jnp.concatenate([cache_ki, ki], axis=1)
    L = P + T
    o = sparse_attend(q, qi, wi, pos, k_all, v_all, ki_all, jnp.arange(L), min(TOPK_MAX, L // 4))
    mix = jnp.concatenate([a * jax.nn.silu(za), o * jax.nn.silu(zb)], axis=-1)
    y = x + mix @ w_out
    return y, k, v, ki, v_n


def setup_inputs(seed: int = 0) -> dict:
    key = jax.random.key(seed)
    ks = jax.random.split(key, 12)
    f32 = jnp.float32
    nrm = jax.random.normal
    return {
        'x_prompt': nrm(ks[0], (BATCH, SEQ, D_MODEL), f32),
        'x_sample': nrm(ks[1], (DEC_BATCH, DEC_SEQ, D_MODEL), f32),
        'cache_k': nrm(ks[2], (DEPTH, DEC_BATCH, PAST_LEN, N_KV, HEAD_DIM), f32),
        'cache_v': nrm(ks[3], (DEPTH, DEC_BATCH, PAST_LEN, N_KV, HEAD_DIM), f32),
        'cache_idx_k': nrm(ks[4], (DEPTH, DEC_BATCH, PAST_LEN, IDX_DIM), f32),
        'norm_g': 1.0 + 0.02 * nrm(ks[5], (DEPTH, D_MODEL), f32),
        'w_in': nrm(ks[6], (DEPTH, D_MODEL, D_IN), f32) * (D_MODEL ** -0.5),
        'w_s': nrm(ks[7], (DEPTH, G_A, MLP_CHUNK, MLP_CHUNK), f32) * (MLP_CHUNK ** -0.5),
        'b_s': 1.0 + 0.02 * nrm(ks[8], (DEPTH, G_A, MLP_CHUNK), f32),
        'v_norm_g': 1.0 + 0.02 * nrm(ks[9], (DEPTH, G_A, HEAD_DIM), f32),
        'w_out': nrm(ks[10], (DEPTH, MIX_WIDTH, D_MODEL), f32) * (MIX_WIDTH ** -0.5),
        'final_norm_g': 1.0 + 0.02 * nrm(ks[11], (D_MODEL,), f32),
    }


def reference(x_prompt, x_sample, cache_k, cache_v, cache_idx_k, norm_g, w_in, w_s, b_s,
              v_norm_g, w_out, final_norm_g):
    kp, vp, kip, ksm, vsm, kism, vnsm = [], [], [], [], [], [], []
    for l in range(DEPTH):
        x_prompt, k1, v1, ki1 = prompt_layer(x_prompt, norm_g[l], w_in[l], w_s[l], b_s[l],
                                             v_norm_g[l], w_out[l])
        x_sample, k2, v2, ki2, vn2 = sample_layer(x_sample, cache_k[l], cache_v[l], cache_idx_k[l],
                                                  norm_g[l], w_in[l], w_s[l], b_s[l],
                                                  v_norm_g[l], w_out[l])
        kp.append(k1); vp.append(v1); kip.append(ki1)
        ksm.append(k2); vsm.append(v2); kism.append(ki2); vnsm.append(vn2)
    y_prompt = rms_norm(x_prompt, final_norm_g)
    y_sample = rms_norm(x_sample, final_norm_g)
    return (y_prompt, y_sample, jnp.stack(kp), jnp.stack(vp), jnp.stack(kip),
            jnp.stack(ksm), jnp.stack(vsm), jnp.stack(kism), jnp.stack(vnsm))
```

```python
import functools

import numpy as np
import jax
import jax.numpy as jnp
from jax import lax
from jax.experimental import pallas as pl
from jax.experimental.pallas import tpu as pltpu

CHUNK = 64
MLP_CHUNK = 128
HEAD_DIM = 128
N_KV = 4
IDX_HEADS = 32
IDX_DIM = 128
TOPK_MAX = 256
ROPE_THETA = 10000.0
EPS = 1e-6

LANES = 128
VMEM_LIMIT = 56 * 1024 * 1024
NEG_INF = float("-inf")
POS_INF = float("inf")
FAR_CHUNK = 1 << 20


def _largest_divisor(n, candidates):
    for c in candidates:
        if n % c == 0:
            return c
    raise ValueError(f"no tile in {candidates} divides {n}")


def _params(sem):
    return pltpu.CompilerParams(dimension_semantics=sem, vmem_limit_bytes=VMEM_LIMIT)


def _rmsnorm_kernel(x_ref, g_ref, o_ref):
    x = x_ref[...]
    ms = jnp.mean(x * x, axis=-1, keepdims=True)
    o_ref[...] = (x * lax.rsqrt(ms + EPS) * g_ref[...]).astype(o_ref.dtype)


def _rmsnorm(x, g, out_dtype):
    n, d = x.shape
    tm = _largest_divisor(n, (256, 128))
    return pl.pallas_call(
        _rmsnorm_kernel,
        grid=(n // tm,),
        in_specs=[pl.BlockSpec((tm, d), lambda i: (i, 0)), pl.BlockSpec((1, d), lambda i: (0, 0))],
        out_specs=pl.BlockSpec((tm, d), lambda i: (i, 0)),
        out_shape=jax.ShapeDtypeStruct((n, d), out_dtype),
        compiler_params=_params(("arbitrary",)),
        name="rmsnorm",
    )(x, g.reshape(1, d))


def _mm_kernel(x_ref, w_ref, o_ref):
    o_ref[...] = jnp.dot(x_ref[...], w_ref[...], preferred_element_type=jnp.float32).astype(o_ref.dtype)


def _matmul(x, w, out_dtype):
    m, k = x.shape
    n = w.shape[1]
    tm = _largest_divisor(m, (768, 512, 384, 256, 128))
    tn = _largest_divisor(n, (1024, 1280, 512, 256, 128))
    return pl.pallas_call(
        _mm_kernel,
        grid=(m // tm, n // tn),
        in_specs=[pl.BlockSpec((tm, k), lambda i, j: (i, 0)), pl.BlockSpec((k, tn), lambda i, j: (0, j))],
        out_specs=pl.BlockSpec((tm, tn), lambda i, j: (i, j)),
        out_shape=jax.ShapeDtypeStruct((m, n), out_dtype),
        compiler_params=_params(("arbitrary", "arbitrary")),
        name="in_proj",
    )(x, w)


def _silu(z):
    return z * (1.0 / (1.0 + jnp.exp(-z)))


def _mlp_kernel(u_ref, va_ref, za_ref, wm_ref, b_ref, gv_ref, mix_ref, *vn_refs, rows, groups):
    for c in range(rows // MLP_CHUNK):
        rs = slice(c * MLP_CHUNK, (c + 1) * MLP_CHUNK)
        for g in range(groups):
            cs = slice(g * HEAD_DIM, (g + 1) * HEAD_DIM)
            va = va_ref[rs, cs]
            mu = jnp.mean(va, axis=-1, keepdims=True)
            cen = va - mu
            var = jnp.mean(cen * cen, axis=-1, keepdims=True)
            vn = cen * lax.rsqrt(var + EPS) * gv_ref[:, cs]
            if vn_refs:
                vn_refs[0][rs, cs] = vn
            mixed = jnp.dot(wm_ref[g], vn.astype(jnp.bfloat16), preferred_element_type=jnp.float32) + b_ref[g]
            a = u_ref[rs, cs] * mixed
            mix_ref[rs, cs] = (a * _silu(za_ref[rs, cs])).astype(mix_ref.dtype)


def _mlp_group(h_main, col0, d_a, row0, nrows, rows, wm, bias, gv, emit_vn):
    groups = d_a // HEAD_DIM
    rb0 = row0 // rows
    out_shape = [jax.ShapeDtypeStruct((nrows, d_a), jnp.bfloat16)]
    out_specs = [pl.BlockSpec((rows, d_a), lambda i: (i, 0))]
    if emit_vn:
        out_shape.append(jax.ShapeDtypeStruct((nrows, d_a), jnp.float32))
        out_specs.append(pl.BlockSpec((rows, d_a), lambda i: (i, 0)))
    col = lambda c: pl.BlockSpec((rows, d_a), lambda i: (rb0 + i, c))
    return pl.pallas_call(
        functools.partial(_mlp_kernel, rows=rows, groups=groups),
        grid=(nrows // rows,),
        in_specs=[col(col0), col(col0 + 1), col(col0 + 2),
                  pl.BlockSpec((groups, MLP_CHUNK, MLP_CHUNK), lambda i: (0, 0, 0)),
                  pl.BlockSpec((groups, MLP_CHUNK, 1), lambda i: (0, 0, 0)),
                  pl.BlockSpec((1, d_a), lambda i: (0, 0))],
        out_specs=out_specs,
        out_shape=out_shape,
        compiler_params=_params(("arbitrary",)),
        name="chunk_mlp",
    )(h_main, h_main, h_main, wm, bias, gv)


def _rope(x, cos, sin):
    return x * cos + pltpu.roll(x, HEAD_DIM // 2, 1) * sin


def _prep_kernel(qi_ref, q_ref, k_ref, v_ref, ki_ref, wi_ref, cos_ref, sin_ref,
                 qir_ref, qr_ref, kr_ref, krb_ref, vb_ref, kir_ref, kirb_ref, wis_ref, *, n_heads):
    cos = cos_ref[...]
    sin = sin_ref[...]
    for h in range(IDX_HEADS):
        cs = slice(h * IDX_DIM, (h + 1) * IDX_DIM)
        qir_ref[:, cs] = _rope(qi_ref[:, cs], cos, sin).astype(qir_ref.dtype)
    q_scale = HEAD_DIM ** -0.5
    for h in range(n_heads):
        cs = slice(h * HEAD_DIM, (h + 1) * HEAD_DIM)
        qr_ref[:, cs] = (_rope(q_ref[:, cs], cos, sin) * q_scale).astype(qr_ref.dtype)
    for h in range(N_KV):
        cs = slice(h * HEAD_DIM, (h + 1) * HEAD_DIM)
        kr = _rope(k_ref[:, cs], cos, sin)
        kr_ref[:, cs] = kr
        krb_ref[:, cs] = kr.astype(krb_ref.dtype)
    vb_ref[...] = v_ref[...].astype(vb_ref.dtype)
    kir = _rope(ki_ref[...], cos, sin)
    kir_ref[...] = kir
    kirb_ref[...] = kir.astype(kirb_ref.dtype)
    wis_ref[...] = wi_ref[...] * ((IDX_HEADS * IDX_DIM) ** -0.5)


def _prep(h_main, h_small, cos_t, sin_t, d_b, q_col):
    n = h_main.shape[0]
    rows = _largest_divisor(n, (256, 128))
    n_heads = d_b // HEAD_DIM
    idx_w = IDX_HEADS * IDX_DIM
    kv_w = N_KV * HEAD_DIM
    rowspec = lambda w, c: pl.BlockSpec((rows, w), lambda i: (i, c))
    bf, f32 = jnp.bfloat16, jnp.float32
    outs = [(idx_w, bf), (d_b, bf), (kv_w, f32), (kv_w, bf), (kv_w, bf), (IDX_DIM, f32), (IDX_DIM, bf), (LANES, f32)]
    return pl.pallas_call(
        functools.partial(_prep_kernel, n_heads=n_heads),
        grid=(n // rows,),
        in_specs=[rowspec(idx_w, 0), rowspec(d_b, q_col),
                  rowspec(kv_w, 0), rowspec(kv_w, 1), rowspec(IDX_DIM, 2 * kv_w // IDX_DIM),
                  rowspec(LANES, 2 * kv_w // LANES + 1), rowspec(HEAD_DIM, 0), rowspec(HEAD_DIM, 0)],
        out_specs=[rowspec(w, 0) for w, _ in outs],
        out_shape=[jax.ShapeDtypeStruct((n, w), dt) for w, dt in outs],
        compiler_params=_params(("arbitrary",)),
        name="rope_prep",
    )(h_main, h_main, h_small, h_small, h_small, h_small, cos_t, sin_t)


def _index_kernel(nkb_ref, qi_ref, wi_ref, qc_ref, kit_ref, kc_ref, s_ref, thr_ref, wb_ref, *, tq, kb, sk, k_sel):
    b = pl.program_id(0)
    i = pl.program_id(1)
    nkb = nkb_ref[b * pl.num_programs(1) + i]
    lane_tiles = kb // LANES
    rt = min(tq, 128)

    for h in range(IDX_HEADS):
        wb_ref[h] = jnp.broadcast_to(wi_ref[:, h:h + 1], (tq, LANES))

    def score_tile(j, carry):
        off = pl.multiple_of(j * kb, kb)
        kt = kit_ref[:, pl.ds(off, kb)]
        kc = kc_ref[:, pl.ds(off, kb)]
        for r0 in range(0, tq, rt):
            accs = [jnp.zeros((rt, LANES), jnp.float32) for _ in range(lane_tiles)]
            for h in range(IDX_HEADS):
                r = jnp.dot(qi_ref[r0:r0 + rt, h * IDX_DIM:(h + 1) * IDX_DIM], kt,
                            preferred_element_type=jnp.float32)
                w = wb_ref[h, r0:r0 + rt, :]
                for l in range(lane_tiles):
                    accs[l] = accs[l] + jnp.maximum(r[:, l * LANES:(l + 1) * LANES], 0.0) * w
            acc = jnp.concatenate(accs, axis=1)
            adm = kc <= qc_ref[r0:r0 + rt, :]
            s_ref[r0:r0 + rt, pl.ds(off, kb)] = jnp.where(adm, acc, NEG_INF)
        return carry

    lax.fori_loop(0, nkb, score_tile, 0)

    def fill_tile(j, carry):
        off = pl.multiple_of(j * kb, kb)
        s_ref[:, pl.ds(off, kb)] = jnp.full((tq, kb), NEG_INF, jnp.float32)
        return carry

    lax.fori_loop(nkb, sk // kb, fill_tile, 0)

    def stats_tile(j, carry):
        mx, mn, nf = carry
        off = pl.multiple_of(j * kb, kb)
        for l in range(lane_tiles):
            s = s_ref[:, pl.ds(off + l * LANES, LANES)]
            fin = s > NEG_INF
            mx = jnp.maximum(mx, s)
            mn = jnp.minimum(mn, jnp.where(fin, s, POS_INF))
            nf = nf + jnp.where(fin, 1.0, 0.0)
        return mx, mn, nf

    mx, mn, nf = lax.fori_loop(
        0, nkb, stats_tile,
        (jnp.full((tq, LANES), NEG_INF, jnp.float32), jnp.full((tq, LANES), POS_INF, jnp.float32),
         jnp.zeros((tq, LANES), jnp.float32)))
    mx = jnp.max(mx, axis=1, keepdims=True)
    mn = jnp.min(mn, axis=1, keepdims=True)
    nf = jnp.sum(nf, axis=1, keepdims=True)
    kk = jnp.minimum(nf, float(k_sel))

    def count_ge(mid):
        def tile(j, cnt):
            off = pl.multiple_of(j * kb, kb)
            for l in range(lane_tiles):
                s = s_ref[:, pl.ds(off + l * LANES, LANES)]
                cnt = cnt + jnp.where(s >= mid, 1.0, 0.0)
            return cnt
        cnt = lax.fori_loop(0, nkb, tile, jnp.zeros((tq, LANES), jnp.float32))
        return jnp.sum(cnt, axis=1, keepdims=True)

    done0 = jnp.where(nf <= float(k_sel), 1.0, 0.0)

    def cond(st):
        return st[4] > 0.5

    def body(st):
        lo, hi, thr, done, _ = st
        mid = 0.5 * lo + 0.5 * hi
        cnt = count_ge(mid)
        active = done < 0.5
        hit = cnt == kk
        stuck = jnp.logical_or(mid <= lo, mid >= hi)
        thr = jnp.where(jnp.logical_and(active, hit), mid,
                        jnp.where(jnp.logical_and(active, stuck), lo, thr))
        up = cnt > kk
        lo = jnp.where(up, mid, lo)
        hi = jnp.where(up, hi, mid)
        done = jnp.where(jnp.logical_or(hit, stuck), 1.0, done)
        return lo, hi, thr, done, jnp.sum(1.0 - done)

    st = lax.while_loop(cond, body, (mn, mx, mn, done0, jnp.sum(1.0 - done0)))
    thr_ref[...] = jnp.broadcast_to(st[2], (tq, LANES))


def _index_scores(qir, wis, qc, kit, kc, nkb, *, row0, batches, nq, tq, kb, k_sel):
    sk = kit.shape[-1]
    rb0 = row0 // tq
    grid_spec = pltpu.PrefetchScalarGridSpec(
        num_scalar_prefetch=1,
        grid=(batches, nq),
        in_specs=[pl.BlockSpec((tq, IDX_HEADS * IDX_DIM), lambda b, i, n: (rb0 + b * nq + i, 0)),
                  pl.BlockSpec((tq, LANES), lambda b, i, n: (rb0 + b * nq + i, 0)),
                  pl.BlockSpec((tq, 1), lambda b, i, n: (rb0 + b * nq + i, 0)),
                  pl.BlockSpec((None, IDX_DIM, sk), lambda b, i, n: (b, 0, 0)),
                  pl.BlockSpec((None, 1, sk), lambda b, i, n: (b, 0, 0))],
        out_specs=[pl.BlockSpec((None, tq, sk), lambda b, i, n: (b, i, 0)),
                   pl.BlockSpec((None, tq, LANES), lambda b, i, n: (b, i, 0))],
        scratch_shapes=[pltpu.VMEM((IDX_HEADS, tq, LANES), jnp.float32)],
    )
    return pl.pallas_call(
        functools.partial(_index_kernel, tq=tq, kb=kb, sk=sk, k_sel=k_sel),
        grid_spec=grid_spec,
        out_shape=[jax.ShapeDtypeStruct((batches, nq * tq, sk), jnp.float32),
                   jax.ShapeDtypeStruct((batches, nq * tq, LANES), jnp.float32)],
        compiler_params=_params(("arbitrary", "arbitrary")),
        name="index_select",
    )(nkb, qir, wis, qc, kit, kc)


def _attn_kernel(b_ref, qi_ref, kj_ref, nk_ref, q_ref, zb_ref, kt_ref, v_ref, s_ref, thr_ref, o_ref,
                 m_ref, l_ref, acc_ref, bias_ref, *, n_heads):
    p = pl.program_id(0)
    kj = kj_ref[p]
    hpg = n_heads // N_KV

    @pl.when(kj == 0)
    def _():
        m_ref[...] = jnp.full(m_ref.shape, NEG_INF, jnp.float32)
        l_ref[...] = jnp.zeros(l_ref.shape, jnp.float32)
        acc_ref[...] = jnp.zeros(acc_ref.shape, jnp.float32)

    bias_ref[...] = jnp.where(s_ref[...] >= thr_ref[:, 0:1], 0.0, NEG_INF)

    for h in range(n_heads):
        g = h // hpg
        s = jnp.dot(q_ref[:, h * HEAD_DIM:(h + 1) * HEAD_DIM], kt_ref[g],
                    preferred_element_type=jnp.float32) + bias_ref[...]
        m_prev = m_ref[h]
        m_cur = jnp.maximum(m_prev, jnp.max(s, axis=1, keepdims=True))
        m_safe = jnp.where(m_cur == NEG_INF, 0.0, m_cur)
        pr = jnp.exp(s - m_safe)
        alpha = jnp.exp(m_prev - m_safe)
        l_ref[h] = alpha * l_ref[h] + jnp.sum(pr, axis=1, keepdims=True)
        acc_ref[h] = alpha * acc_ref[h] + jnp.dot(pr.astype(v_ref.dtype), v_ref[:, g * HEAD_DIM:(g + 1) * HEAD_DIM],
                                                  preferred_element_type=jnp.float32)
        m_ref[h] = m_cur

    @pl.when(kj == nk_ref[p] - 1)
    def _():
        for h in range(n_heads):
            cs = slice(h * HEAD_DIM, (h + 1) * HEAD_DIM)
            o = acc_ref[h] / l_ref[h]
            o_ref[:, cs] = (o * _silu(zb_ref[:, cs])).astype(o_ref.dtype)


def _masked_attention(qr, h_main, zb_col, kt, vb, scores, thr, steps, *, row0, nq, tq, kb, d_b):
    bidx, qidx, kidx, nk = steps
    batches, _, _, sk = kt.shape
    rb0 = row0 // tq
    n_heads = d_b // HEAD_DIM
    kv_w = N_KV * HEAD_DIM
    qrow = lambda p, b, q, k, n: rb0 + b[p] * nq + q[p]
    grid_spec = pltpu.PrefetchScalarGridSpec(
        num_scalar_prefetch=4,
        grid=(bidx.shape[0],),
        in_specs=[pl.BlockSpec((tq, d_b), lambda p, b, q, k, n: (qrow(p, b, q, k, n), 0)),
                  pl.BlockSpec((tq, d_b), lambda p, b, q, k, n: (qrow(p, b, q, k, n), zb_col)),
                  pl.BlockSpec((None, N_KV, HEAD_DIM, kb), lambda p, b, q, k, n: (b[p], 0, 0, k[p])),
                  pl.BlockSpec((None, kb, kv_w), lambda p, b, q, k, n: (b[p], k[p], 0)),
                  pl.BlockSpec((None, tq, kb), lambda p, b, q, k, n: (b[p], q[p], k[p])),
                  pl.BlockSpec((None, tq, LANES), lambda p, b, q, k, n: (b[p], q[p], 0))],
        out_specs=pl.BlockSpec((tq, d_b), lambda p, b, q, k, n: (b[p] * nq + q[p], 0)),
        scratch_shapes=[pltpu.VMEM((n_heads, tq, 1), jnp.float32),
                        pltpu.VMEM((n_heads, tq, 1), jnp.float32),
                        pltpu.VMEM((n_heads, tq, HEAD_DIM), jnp.float32),
                        pltpu.VMEM((tq, kb), jnp.float32)],
    )
    return pl.pallas_call(
        functools.partial(_attn_kernel, n_heads=n_heads),
        grid_spec=grid_spec,
        out_shape=jax.ShapeDtypeStruct((batches * nq * tq, d_b), jnp.bfloat16),
        compiler_params=_params(("arbitrary",)),
        name="masked_attention",
    )(bidx, qidx, kidx, nk, qr, h_main, kt, vb, scores, thr)


def _out_kernel(mix_ref, w_ref, x_ref, g_ref, o_ref, acc_ref):
    k = pl.program_id(1)

    @pl.when(k == 0)
    def _():
        acc_ref[...] = jnp.zeros(acc_ref.shape, jnp.float32)

    acc_ref[...] += jnp.dot(mix_ref[...], w_ref[...], preferred_element_type=jnp.float32)

    @pl.when(k == pl.num_programs(1) - 1)
    def _():
        y = x_ref[...] + acc_ref[...]
        ms = jnp.mean(y * y, axis=-1, keepdims=True)
        o_ref[...] = y * lax.rsqrt(ms + EPS) * g_ref[...]


def _out_proj(mix, w, x, g):
    n, d = x.shape
    kdim = mix.shape[1]
    tm = _largest_divisor(n, (384, 256, 128))
    tk = _largest_divisor(kdim, (512, 256, 128))
    return pl.pallas_call(
        _out_kernel,
        grid=(n // tm, kdim // tk),
        in_specs=[pl.BlockSpec((tm, tk), lambda i, k: (i, k)), pl.BlockSpec((tk, d), lambda i, k: (k, 0)),
                  pl.BlockSpec((tm, d), lambda i, k: (i, 0)), pl.BlockSpec((1, d), lambda i, k: (0, 0))],
        out_specs=pl.BlockSpec((tm, d), lambda i, k: (i, 0)),
        out_shape=jax.ShapeDtypeStruct((n, d), jnp.float32),
        scratch_shapes=[pltpu.VMEM((tm, d), jnp.float32)],
        compiler_params=_params(("arbitrary", "arbitrary")),
        name="out_proj",
    )(mix, w, x, g.reshape(1, d))


def _cdiv(a, b):
    return -(-a // b)


def _attention_steps(batches, nq, tq, kb, causal, sk):
    b_l, q_l, k_l, n_l = [], [], [], []
    for b in range(batches):
        for q in range(nq):
            nk = _cdiv((q + 1) * tq, kb) if causal else sk // kb
            for k in range(nk):
                b_l.append(b), q_l.append(q), k_l.append(k), n_l.append(nk)
    return tuple(jnp.asarray(np.asarray(v, np.int32)) for v in (b_l, q_l, k_l, n_l))


def _layer(x_prompt, x_sample, cache_k, cache_v, cache_ki, g_norm, w_in, w_s, b_s, g_v, w_out, final_g):
    f32, bf = jnp.float32, jnp.bfloat16
    _, seq, d_model = x_prompt.shape
    dec_b, dec_t, _ = x_sample.shape
    past = cache_k.shape[1]
    d_a = d_model // 2
    d_b = d_model - d_a
    groups = d_a // HEAD_DIM
    kv_w = N_KV * HEAD_DIM
    idx_w = IDX_HEADS * IDX_DIM
    n_s = dec_b * dec_t
    n_all = seq + n_s
    assert seq % 512 == 0 and n_s % MLP_CHUNK == 0 and dec_t <= CHUNK and past % MLP_CHUNK == 0
    assert idx_w % d_a == 0 and d_a % HEAD_DIM == 0

    x_all = jnp.concatenate([x_prompt[0], x_sample.reshape(n_s, d_model)], axis=0)
    pts = np.cumsum((d_a, d_a, d_a, d_b, kv_w, kv_w, d_b, idx_w, IDX_DIM, IDX_HEADS))[:-1].tolist()
    w_u, w_va, w_za, w_q, w_k, w_v, w_zb, w_qi, w_ki, w_wi = jnp.split(w_in, pts, axis=1)
    w_main = jnp.concatenate([w_qi, w_u, w_va, w_za, w_q, w_zb], axis=1).astype(bf)
    small_w = 2 * kv_w + IDX_DIM + LANES
    w_small = jnp.concatenate([w_k, w_v, w_ki, w_wi, jnp.zeros((d_model, LANES - IDX_HEADS), f32)], axis=1).astype(bf)
    u_col = idx_w // d_a
    q_col, zb_col = u_col + 3, u_col + 4

    pos = jnp.concatenate([jnp.arange(seq), past + jnp.tile(jnp.arange(dec_t), dec_b)])
    half = HEAD_DIM // 2
    inv = ROPE_THETA ** (-2.0 * jnp.arange(half, dtype=f32) / HEAD_DIM)
    ang = pos.astype(f32)[:, None] * inv[None, :]
    cos_t = jnp.concatenate([jnp.cos(ang), jnp.cos(ang)], axis=1)
    sin_t = jnp.concatenate([-jnp.sin(ang), jnp.sin(ang)], axis=1)

    xn = _rmsnorm(x_all, g_norm, bf)
    h_main = _matmul(xn, w_main, f32)
    h_small = _matmul(xn, w_small, f32)

    pidx = np.arange(MLP_CHUNK)
    mask_p = (pidx[None, :] // CHUNK) <= (pidx[:, None] // CHUNK)
    wm_p = jnp.where(mask_p[None], w_s, 0.0).astype(bf)
    b_p = b_s[:, :, None]
    tidx = pidx % dec_t
    same = (pidx[None, :] // dec_t) == (pidx[:, None] // dec_t)
    mask_s = same & ((tidx[None, :] // CHUNK) <= (tidx[:, None] // CHUNK))
    wm_s = jnp.where(mask_s[None], w_s[:, tidx][:, :, tidx], 0.0).astype(bf)
    b_sm = b_s[:, tidx][:, :, None]
    gv = g_v.reshape(1, d_a)
    (mix_a_p,) = _mlp_group(h_main, u_col, d_a, 0, seq, 512, wm_p, b_p, gv, False)
    mix_a_s, vn_s = _mlp_group(h_main, u_col, d_a, seq, n_s, MLP_CHUNK, wm_s, b_sm, gv, True)

    qir, qr, kr, krb, vb, kir, kirb, wis = _prep(h_main, h_small, cos_t, sin_t, d_b, q_col)
    v_all = h_small[:, kv_w:2 * kv_w]
    qc = (pos // CHUNK).astype(jnp.int32)[:, None]

    tq_i, kb_i, tq_a, kb_a = 128, 256, 256, 512
    k_sel_p = min(TOPK_MAX, seq // 4)
    kit_p = kirb[:seq].T[None]
    kc_p = (jnp.arange(seq, dtype=jnp.int32) // CHUNK)[None, None, :]
    nq_i = seq // tq_i
    nkb_p = jnp.asarray(np.asarray([_cdiv((i + 1) * tq_i, kb_i) for i in range(nq_i)], np.int32))
    s_p, thr_p = _index_scores(qir, wis, qc, kit_p, kc_p, nkb_p, row0=0, batches=1, nq=nq_i, tq=tq_i, kb=kb_i,
                               k_sel=k_sel_p)
    kt_p = krb[:seq].reshape(seq, N_KV, HEAD_DIM).transpose(1, 2, 0)[None]
    nq_a = seq // tq_a
    steps_p = _attention_steps(1, nq_a, tq_a, kb_a, True, seq)
    mix_b_p = _masked_attention(qr, h_main, zb_col, kt_p, vb[:seq][None], s_p, thr_p, steps_p,
                                row0=0, nq=nq_a, tq=tq_a, kb=kb_a, d_b=d_b)

    n_keys = past + dec_t
    sk_s = _cdiv(n_keys, kb_i) * kb_i
    padk = lambda a: jnp.pad(a, ((0, 0), (0, sk_s - n_keys), (0, 0)))
    k_s = padk(jnp.concatenate([cache_k.reshape(dec_b, past, kv_w).astype(bf), krb[seq:].reshape(dec_b, dec_t, kv_w)], axis=1))
    v_s = padk(jnp.concatenate([cache_v.reshape(dec_b, past, kv_w).astype(bf), vb[seq:].reshape(dec_b, dec_t, kv_w)], axis=1))
    ki_s = padk(jnp.concatenate([cache_ki.astype(bf), kirb[seq:].reshape(dec_b, dec_t, IDX_DIM)], axis=1))
    kit_s = ki_s.transpose(0, 2, 1)
    kt_s = k_s.reshape(dec_b, sk_s, N_KV, HEAD_DIM).transpose(0, 2, 3, 1)
    kpos = jnp.arange(sk_s, dtype=jnp.int32)
    kc_s = jnp.broadcast_to(jnp.where(kpos < n_keys, kpos // CHUNK, FAR_CHUNK)[None, None, :], (dec_b, 1, sk_s))
    k_sel_s = min(TOPK_MAX, n_keys // 4)
    nkb_s = jnp.full((dec_b,), sk_s // kb_i, jnp.int32)
    s_s, thr_s = _index_scores(qir, wis, qc, kit_s, kc_s, nkb_s, row0=seq, batches=dec_b, nq=1, tq=dec_t, kb=kb_i,
                               k_sel=k_sel_s)
    steps_s = _attention_steps(dec_b, 1, dec_t, sk_s, False, sk_s)
    mix_b_s = _masked_attention(qr, h_main, zb_col, kt_s, v_s, s_s, thr_s, steps_s,
                                row0=seq, nq=1, tq=dec_t, kb=sk_s, d_b=d_b)

    mix = jnp.concatenate([jnp.concatenate([mix_a_p, mix_a_s], axis=0),
                           jnp.concatenate([mix_b_p, mix_b_s], axis=0)], axis=1)
    y = _out_proj(mix, w_out.astype(bf), x_all, final_g)

    shp_p = (1, 1, seq, N_KV, HEAD_DIM)
    shp_s = (1, dec_b, dec_t, N_KV, HEAD_DIM)
    return (y[:seq][None], y[seq:].reshape(dec_b, dec_t, d_model),
            kr[:seq].reshape(shp_p), v_all[:seq].reshape(shp_p), kir[:seq].reshape(1, 1, seq, IDX_DIM),
            kr[seq:].reshape(shp_s), v_all[seq:].reshape(shp_s), kir[seq:].reshape(1, dec_b, dec_t, IDX_DIM),
            vn_s.reshape(1, dec_b, dec_t, groups, HEAD_DIM))


def kernel(x_prompt, x_sample, cache_k, cache_v, cache_idx_k, norm_g, w_in, w_s, b_s, v_norm_g, w_out, final_norm_g):
    assert x_prompt.shape[0] == 1 and norm_g.shape[0] == 1, "one prompt stream and one layer"
    return _layer(x_prompt, x_sample, cache_k[0], cache_v[0], cache_idx_k[0], norm_g[0], w_in[0], w_s[0], b_s[0],
                  v_norm_g[0], w_out[0], final_norm_g)
```

```python
import functools

import numpy as np
import jax
import jax.numpy as jnp
from jax import lax
from jax.experimental import pallas as pl
from jax.experimental.pallas import tpu as pltpu

CHUNK = 64
MLP_CHUNK = 128
HEAD_DIM = 128
N_KV = 4
IDX_HEADS = 32
IDX_DIM = 128
TOPK_MAX = 256
ROPE_THETA = 10000.0
EPS = 1e-6

LANES = 128
VMEM_LIMIT = 56 * 1024 * 1024
NEG_INF = float("-inf")
POS_INF = float("inf")
FAR_CHUNK = 1 << 20
LOG2_E = 1.4426950408889634
CHUNK_SHIFT = 6
PROMPT_TQ, PROMPT_KB_INDEX, PROMPT_KB_ATTN = 256, 128, 512
ONES_ROWS = 16
VT_ROWS = HEAD_DIM + ONES_ROWS
ATTN_ROW_CHUNK = 128
SAMPLE_KB_INDEX = 256


def _largest_divisor(n, candidates):
    for c in candidates:
        if n % c == 0:
            return c
    raise ValueError(f"no tile in {candidates} divides {n}")


def _params(sem, flags=None):
    return pltpu.CompilerParams(dimension_semantics=sem, vmem_limit_bytes=VMEM_LIMIT, flags=flags)


def _rmsnorm_kernel(x_ref, g_ref, o_ref):
    x = x_ref[...]
    ms = jnp.mean(x * x, axis=-1, keepdims=True)
    o_ref[...] = (x * lax.rsqrt(ms + EPS) * g_ref[...]).astype(o_ref.dtype)


def _rmsnorm(x, g, out_dtype):
    n, d = x.shape
    tm = _largest_divisor(n, (256, 128))
    return pl.pallas_call(
        _rmsnorm_kernel,
        grid=(n // tm,),
        in_specs=[pl.BlockSpec((tm, d), lambda i: (i, 0)), pl.BlockSpec((1, d), lambda i: (0, 0))],
        out_specs=pl.BlockSpec((tm, d), lambda i: (i, 0)),
        out_shape=jax.ShapeDtypeStruct((n, d), out_dtype),
        compiler_params=_params(("arbitrary",)),
        name="rmsnorm",
    )(x, g.reshape(1, d))


def _mm_kernel(x_ref, w_ref, o_ref):
    o_ref[...] = jnp.dot(x_ref[...], w_ref[...], preferred_element_type=jnp.float32).astype(o_ref.dtype)


def _matmul(x, w, out_dtype):
    m, k = x.shape
    n = w.shape[1]
    tm = _largest_divisor(m, (768, 512, 384, 256, 128))
    tn = _largest_divisor(n, (1024, 1280, 512, 256, 128))
    return pl.pallas_call(
        _mm_kernel,
        grid=(m // tm, n // tn),
        in_specs=[pl.BlockSpec((tm, k), lambda i, j: (i, 0)), pl.BlockSpec((k, tn), lambda i, j: (0, j))],
        out_specs=pl.BlockSpec((tm, tn), lambda i, j: (i, j)),
        out_shape=jax.ShapeDtypeStruct((m, n), out_dtype),
        compiler_params=_params(("arbitrary", "arbitrary")),
        name="in_proj",
    )(x, w)


def _silu(z):
    return z * (1.0 / (1.0 + jnp.exp(-z)))


def _mlp_kernel(u_ref, va_ref, za_ref, wm_ref, b_ref, gv_ref, mix_ref, *vn_refs, rows, groups):
    for c in range(rows // MLP_CHUNK):
        rs = slice(c * MLP_CHUNK, (c + 1) * MLP_CHUNK)
        for g in range(groups):
            cs = slice(g * HEAD_DIM, (g + 1) * HEAD_DIM)
            va = va_ref[rs, cs]
            mu = jnp.mean(va, axis=-1, keepdims=True)
            cen = va - mu
            var = jnp.mean(cen * cen, axis=-1, keepdims=True)
            vn = cen * lax.rsqrt(var + EPS) * gv_ref[:, cs]
            if vn_refs:
                vn_refs[0][rs, cs] = vn
            mixed = jnp.dot(wm_ref[g], vn.astype(jnp.bfloat16), preferred_element_type=jnp.float32) + b_ref[g]
            a = u_ref[rs, cs] * mixed
            mix_ref[rs, cs] = (a * _silu(za_ref[rs, cs])).astype(mix_ref.dtype)


def _mlp_group(h_main, col0, d_a, row0, nrows, rows, wm, bias, gv, emit_vn):
    groups = d_a // HEAD_DIM
    rb0 = row0 // rows
    out_shape = [jax.ShapeDtypeStruct((nrows, d_a), jnp.bfloat16)]
    out_specs = [pl.BlockSpec((rows, d_a), lambda i: (i, 0))]
    if emit_vn:
        out_shape.append(jax.ShapeDtypeStruct((nrows, d_a), jnp.float32))
        out_specs.append(pl.BlockSpec((rows, d_a), lambda i: (i, 0)))
    col = lambda c: pl.BlockSpec((rows, d_a), lambda i: (rb0 + i, c))
    return pl.pallas_call(
        functools.partial(_mlp_kernel, rows=rows, groups=groups),
        grid=(nrows // rows,),
        in_specs=[col(col0), col(col0 + 1), col(col0 + 2),
                  pl.BlockSpec((groups, MLP_CHUNK, MLP_CHUNK), lambda i: (0, 0, 0)),
                  pl.BlockSpec((groups, MLP_CHUNK, 1), lambda i: (0, 0, 0)),
                  pl.BlockSpec((1, d_a), lambda i: (0, 0))],
        out_specs=out_specs,
        out_shape=out_shape,
        compiler_params=_params(("arbitrary",)),
        name="chunk_mlp",
    )(h_main, h_main, h_main, wm, bias, gv)


def _rope(x, cos, sin):
    return x * cos + pltpu.roll(x, HEAD_DIM // 2, 1) * sin


def _prep_kernel(qi_ref, q_ref, k_ref, v_ref, ki_ref, wi_ref, cos_ref, sin_ref,
                 qir_ref, qr_ref, kr_ref, krb_ref, vb_ref, kir_ref, kirb_ref, wis_ref, *, n_heads, transposed):
    cos = cos_ref[...]
    sin = sin_ref[...]

    def put(ref, h, width, val):
        if transposed:
            ref[h * width:(h + 1) * width, :] = val.T.astype(ref.dtype)
        else:
            ref[:, h * width:(h + 1) * width] = val.astype(ref.dtype)

    for h in range(IDX_HEADS):
        put(qir_ref, h, IDX_DIM, _rope(qi_ref[:, h * IDX_DIM:(h + 1) * IDX_DIM], cos, sin))
    q_scale = HEAD_DIM ** -0.5 * LOG2_E
    for h in range(n_heads):
        put(qr_ref, h, HEAD_DIM, _rope(q_ref[:, h * HEAD_DIM:(h + 1) * HEAD_DIM], cos, sin) * q_scale)
    for h in range(N_KV):
        cs = slice(h * HEAD_DIM, (h + 1) * HEAD_DIM)
        kr = _rope(k_ref[:, cs], cos, sin)
        kr_ref[:, cs] = kr
        krb_ref[:, cs] = kr.astype(krb_ref.dtype)
        if transposed:
            vb_ref[h * VT_ROWS:h * VT_ROWS + HEAD_DIM, :] = v_ref[:, cs].T.astype(vb_ref.dtype)
            vb_ref[h * VT_ROWS + HEAD_DIM:(h + 1) * VT_ROWS, :] = jnp.ones((ONES_ROWS, v_ref.shape[0]), vb_ref.dtype)
        else:
            vb_ref[:, cs] = v_ref[:, cs].astype(vb_ref.dtype)
    kir = _rope(ki_ref[...], cos, sin)
    kir_ref[...] = kir
    kirb_ref[...] = kir.astype(kirb_ref.dtype)
    put(wis_ref, 0, LANES, wi_ref[...] * ((IDX_HEADS * IDX_DIM) ** -0.5))


def _prep(h_main, h_small, cos_t, sin_t, d_b, q_col, *, row0, nrows, transposed):
    rows = _largest_divisor(nrows, (256, 128))
    rb0 = row0 // rows
    n_heads = d_b // HEAD_DIM
    idx_w = IDX_HEADS * IDX_DIM
    kv_w = N_KV * HEAD_DIM
    inspec = lambda w, c: pl.BlockSpec((rows, w), lambda i: (rb0 + i, c))
    bf, f32 = jnp.bfloat16, jnp.float32
    outs = [(idx_w, bf, True), (d_b, bf, True), (kv_w, f32, False), (kv_w, bf, False), (kv_w, bf, True),
            (IDX_DIM, f32, False), (IDX_DIM, bf, False), (LANES, f32, True)]
    out_specs, out_shape = [], []
    for idx, (w, dt, feature_major) in enumerate(outs):
        if transposed and feature_major:
            w = N_KV * VT_ROWS if idx == 4 else w
            out_specs.append(pl.BlockSpec((w, rows), lambda i: (0, i)))
            out_shape.append(jax.ShapeDtypeStruct((w, nrows), dt))
        else:
            out_specs.append(pl.BlockSpec((rows, w), lambda i: (i, 0)))
            out_shape.append(jax.ShapeDtypeStruct((nrows, w), dt))
    return pl.pallas_call(
        functools.partial(_prep_kernel, n_heads=n_heads, transposed=transposed),
        grid=(nrows // rows,),
        in_specs=[inspec(idx_w, 0), inspec(d_b, q_col),
                  inspec(kv_w, 0), inspec(kv_w, 1), inspec(IDX_DIM, 2 * kv_w // IDX_DIM),
                  inspec(LANES, 2 * kv_w // LANES + 1), inspec(HEAD_DIM, 0), inspec(HEAD_DIM, 0)],
        out_specs=out_specs,
        out_shape=out_shape,
        compiler_params=_params(("arbitrary",)),
        name="rope_prep",
    )(h_main, h_main, h_small, h_small, h_small, h_small, cos_t, sin_t)


def _index_kernel(nkb_ref, qi_ref, wi_ref, qc_ref, kit_ref, kc_ref, s_ref, thr_ref, wb_ref, *, tq, kb, sk, k_sel):
    b = pl.program_id(0)
    i = pl.program_id(1)
    nkb = nkb_ref[b * pl.num_programs(1) + i]
    lane_tiles = kb // LANES
    rt = min(tq, 128)

    for h in range(IDX_HEADS):
        wb_ref[h] = jnp.broadcast_to(wi_ref[:, h:h + 1], (tq, LANES))

    def score_tile(j, carry):
        off = pl.multiple_of(j * kb, kb)
        kt = kit_ref[:, pl.ds(off, kb)]
        kc = kc_ref[:, pl.ds(off, kb)]
        for r0 in range(0, tq, rt):
            accs = [jnp.zeros((rt, LANES), jnp.float32) for _ in range(lane_tiles)]
            for h in range(IDX_HEADS):
                r = jnp.dot(qi_ref[r0:r0 + rt, h * IDX_DIM:(h + 1) * IDX_DIM], kt,
                            preferred_element_type=jnp.float32)
                w = wb_ref[h, r0:r0 + rt, :]
                for l in range(lane_tiles):
                    accs[l] = accs[l] + jnp.maximum(r[:, l * LANES:(l + 1) * LANES], 0.0) * w
            acc = jnp.concatenate(accs, axis=1)
            adm = kc <= qc_ref[r0:r0 + rt, :]
            s_ref[r0:r0 + rt, pl.ds(off, kb)] = jnp.where(adm, acc, NEG_INF)
        return carry

    lax.fori_loop(0, nkb, score_tile, 0)

    def fill_tile(j, carry):
        off = pl.multiple_of(j * kb, kb)
        s_ref[:, pl.ds(off, kb)] = jnp.full((tq, kb), NEG_INF, jnp.float32)
        return carry

    lax.fori_loop(nkb, sk // kb, fill_tile, 0)

    def stats_tile(j, carry):
        mx, mn, nf = carry
        off = pl.multiple_of(j * kb, kb)
        for l in range(lane_tiles):
            s = s_ref[:, pl.ds(off + l * LANES, LANES)]
            fin = s > NEG_INF
            mx = jnp.maximum(mx, s)
            mn = jnp.minimum(mn, jnp.where(fin, s, POS_INF))
            nf = nf + jnp.where(fin, 1.0, 0.0)
        return mx, mn, nf

    mx, mn, nf = lax.fori_loop(
        0, nkb, stats_tile,
        (jnp.full((tq, LANES), NEG_INF, jnp.float32), jnp.full((tq, LANES), POS_INF, jnp.float32),
         jnp.zeros((tq, LANES), jnp.float32)))
    mx = jnp.max(mx, axis=1, keepdims=True)
    mn = jnp.min(mn, axis=1, keepdims=True)
    nf = jnp.sum(nf, axis=1, keepdims=True)
    kk = jnp.minimum(nf, float(k_sel))

    def count_ge(mid):
        def tile(j, cnt):
            off = pl.multiple_of(j * kb, kb)
            for l in range(lane_tiles):
                s = s_ref[:, pl.ds(off + l * LANES, LANES)]
                cnt = cnt + jnp.where(s >= mid, 1.0, 0.0)
            return cnt
        cnt = lax.fori_loop(0, nkb, tile, jnp.zeros((tq, LANES), jnp.float32))
        return jnp.sum(cnt, axis=1, keepdims=True)

    done0 = jnp.where(nf <= float(k_sel), 1.0, 0.0)

    def cond(st):
        return st[4] > 0.5

    def body(st):
        lo, hi, thr, done, _ = st
        mid = 0.5 * lo + 0.5 * hi
        cnt = count_ge(mid)
        active = done < 0.5
        hit = cnt == kk
        stuck = jnp.logical_or(mid <= lo, mid >= hi)
        thr = jnp.where(jnp.logical_and(active, hit), mid,
                        jnp.where(jnp.logical_and(active, stuck), lo, thr))
        up = cnt > kk
        lo = jnp.where(up, mid, lo)
        hi = jnp.where(up, hi, mid)
        done = jnp.where(jnp.logical_or(hit, stuck), 1.0, done)
        return lo, hi, thr, done, jnp.sum(1.0 - done)

    st = lax.while_loop(cond, body, (mn, mx, mn, done0, jnp.sum(1.0 - done0)))
    thr_ref[...] = jnp.broadcast_to(st[2], (tq, LANES))


def _index_scores(qir, wis, qc, kit, kc, nkb, *, row0, batches, nq, tq, kb, k_sel):
    sk = kit.shape[-1]
    rb0 = row0 // tq
    grid_spec = pltpu.PrefetchScalarGridSpec(
        num_scalar_prefetch=1,
        grid=(batches, nq),
        in_specs=[pl.BlockSpec((tq, IDX_HEADS * IDX_DIM), lambda b, i, n: (rb0 + b * nq + i, 0)),
                  pl.BlockSpec((tq, LANES), lambda b, i, n: (rb0 + b * nq + i, 0)),
                  pl.BlockSpec((tq, 1), lambda b, i, n: (rb0 + b * nq + i, 0)),
                  pl.BlockSpec((None, IDX_DIM, sk), lambda b, i, n: (b, 0, 0)),
                  pl.BlockSpec((None, 1, sk), lambda b, i, n: (b, 0, 0))],
        out_specs=[pl.BlockSpec((None, tq, sk), lambda b, i, n: (b, i, 0)),
                   pl.BlockSpec((None, tq, LANES), lambda b, i, n: (b, i, 0))],
        scratch_shapes=[pltpu.VMEM((IDX_HEADS, tq, LANES), jnp.float32)],
    )
    return pl.pallas_call(
        functools.partial(_index_kernel, tq=tq, kb=kb, sk=sk, k_sel=k_sel),
        grid_spec=grid_spec,
        out_shape=[jax.ShapeDtypeStruct((batches, nq * tq, sk), jnp.float32),
                   jax.ShapeDtypeStruct((batches, nq * tq, LANES), jnp.float32)],
        compiler_params=_params(("arbitrary", "arbitrary")),
        name="index_select",
    )(nkb, qir, wis, qc, kit, kc)


def _fold8(x, op):
    parts = [x[g * 8:(g + 1) * 8, :] for g in range(x.shape[0] // 8)]
    while len(parts) > 1:
        parts = [op(parts[a], parts[a + 1]) for a in range(0, len(parts) - 1, 2)] + (parts[-1:] if len(parts) % 2 else [])
    return parts[0]


def _index_kernel_t(qit_ref, wit_ref, ki_ref, st_ref, thr_ref, *, tq, kb, k_sel):
    i = pl.program_id(0)
    nkb = (i + 1) * (tq // kb)
    total = st_ref.shape[0] // kb
    qchunk = lax.shift_right_logical(i * tq + lax.broadcasted_iota(jnp.int32, (kb, tq), 1), CHUNK_SHIFT)
    krow = lax.broadcasted_iota(jnp.int32, (kb, tq), 0)

    def score_tile(j, carry):
        off = pl.multiple_of(j * kb, kb)
        k_t = ki_ref[pl.ds(off, kb), :]
        acc = jnp.zeros((kb, tq), jnp.float32)
        for h in range(IDX_HEADS):
            r = jnp.dot(k_t, qit_ref[h * IDX_DIM:(h + 1) * IDX_DIM, :], preferred_element_type=jnp.float32)
            acc = acc + jnp.maximum(r, 0.0) * wit_ref[h:h + 1, :]
        adm = lax.shift_right_logical(off + krow, CHUNK_SHIFT) <= qchunk
        st_ref[pl.ds(off, kb), :] = jnp.where(adm, acc, NEG_INF)
        return carry

    lax.fori_loop(0, nkb, score_tile, 0)

    def fill_tile(j, carry):
        off = pl.multiple_of(j * kb, kb)
        st_ref[pl.ds(off, kb), :] = jnp.full((kb, tq), NEG_INF, jnp.float32)
        return carry

    lax.fori_loop(nkb, total, fill_tile, 0)

    def stats_tile(j, carry):
        mx, mn, nf = carry
        off = pl.multiple_of(j * kb, kb)
        s = st_ref[pl.ds(off, kb), :]
        fin = s > NEG_INF
        mx = jnp.maximum(mx, _fold8(s, jnp.maximum))
        mn = jnp.minimum(mn, _fold8(jnp.where(fin, s, POS_INF), jnp.minimum))
        nf = nf + _fold8(jnp.where(fin, 1.0, 0.0), jnp.add)
        return mx, mn, nf

    mx, mn, nf = lax.fori_loop(
        0, nkb, stats_tile,
        (jnp.full((8, tq), NEG_INF, jnp.float32), jnp.full((8, tq), POS_INF, jnp.float32),
         jnp.zeros((8, tq), jnp.float32)))
    mx = jnp.max(mx, axis=0, keepdims=True)
    mn = jnp.min(mn, axis=0, keepdims=True)
    nf = jnp.sum(nf, axis=0, keepdims=True)
    kk = jnp.minimum(nf, float(k_sel))

    def count_ge(mid):
        def tile(j, cnt):
            off = pl.multiple_of(j * kb, kb)
            s = st_ref[pl.ds(off, kb), :]
            return cnt + _fold8(jnp.where(s >= mid, 1.0, 0.0), jnp.add)
        cnt = lax.fori_loop(0, nkb, tile, jnp.zeros((8, tq), jnp.float32))
        return jnp.sum(cnt, axis=0, keepdims=True)

    done0 = jnp.where(nf <= float(k_sel), 1.0, 0.0)

    def cond(st):
        return st[4] > 0.5

    def body(st):
        lo, hi, thr, done, _ = st
        mid = 0.5 * lo + 0.5 * hi
        cnt = count_ge(mid)
        active = done < 0.5
        hit = cnt == kk
        stuck = jnp.logical_or(mid <= lo, mid >= hi)
        thr = jnp.where(jnp.logical_and(active, hit), mid,
                        jnp.where(jnp.logical_and(active, stuck), lo, thr))
        up = cnt > kk
        lo = jnp.where(up, mid, lo)
        hi = jnp.where(up, hi, mid)
        done = jnp.where(jnp.logical_or(hit, stuck), 1.0, done)
        return lo, hi, thr, done, jnp.sum(1.0 - done)

    st = lax.while_loop(cond, body, (mn, mx, mn, done0, jnp.sum(1.0 - done0)))
    thr_ref[...] = jnp.broadcast_to(st[2], (8, tq))


def _index_scores_t(qirt, wist, kirb, *, tq, kb, k_sel):
    idx_w, seq = qirt.shape
    nq = seq // tq
    return pl.pallas_call(
        functools.partial(_index_kernel_t, tq=tq, kb=kb, k_sel=k_sel),
        grid=(nq,),
        in_specs=[pl.BlockSpec((idx_w, tq), lambda i: (0, i)),
                  pl.BlockSpec((LANES, tq), lambda i: (0, i)),
                  pl.BlockSpec((seq, IDX_DIM), lambda i: (0, 0))],
        out_specs=[pl.BlockSpec((seq, tq), lambda i: (0, i)),
                   pl.BlockSpec((8, tq), lambda i: (0, i))],
        out_shape=[jax.ShapeDtypeStruct((seq, seq), jnp.float32),
                   jax.ShapeDtypeStruct((8, seq), jnp.float32)],
        compiler_params=_params(("arbitrary",)),
        name="index_select_t",
    )(qirt, wist, kirb)


def _attn_kernel_t(qi_ref, kj_ref, nk_ref, qt_ref, zb_ref, k_ref, vt_ref, st_ref, thr_ref, o_ref,
                   m_ref, acc_ref, s_ref, p_ref, bias_ref, *, n_heads):
    p = pl.program_id(0)
    kj = kj_ref[p]
    hpg = n_heads // N_KV

    @pl.when(kj == 0)
    def _():
        m_ref[...] = jnp.full(m_ref.shape, NEG_INF, jnp.float32)
        acc_ref[...] = jnp.zeros(acc_ref.shape, jnp.float32)

    kb = st_ref.shape[0]
    rc = ATTN_ROW_CHUNK
    thr = thr_ref[0:1, :]
    for r in range(0, kb, rc):
        bias_ref[r:r + rc, :] = jnp.where(st_ref[r:r + rc, :] >= thr, 0.0, NEG_INF)

    cmax = []
    for h in range(n_heads):
        g = h // hpg
        cm = None
        for r in range(0, kb, rc):
            s = jnp.dot(k_ref[r:r + rc, g * HEAD_DIM:(g + 1) * HEAD_DIM], qt_ref[h * HEAD_DIM:(h + 1) * HEAD_DIM, :],
                        preferred_element_type=jnp.float32) + bias_ref[r:r + rc, :]
            s_ref[h, r:r + rc, :] = s
            c8 = _fold8(s, jnp.maximum)
            cm = c8 if cm is None else jnp.maximum(cm, c8)
        cmax.append(jnp.max(cm, axis=0, keepdims=True))
    alphas = []
    for h in range(n_heads):
        m_prev = m_ref[h]
        m_cur = jnp.maximum(m_prev, cmax[h])
        m_safe = jnp.where(m_cur == NEG_INF, 0.0, m_cur)
        for r in range(0, kb, rc):
            p_ref[h, r:r + rc, :] = jnp.exp2(s_ref[h, r:r + rc, :] - m_safe).astype(p_ref.dtype)
        m_ref[h] = m_cur
        alphas.append(jnp.exp2(m_prev - m_safe))
    for h in range(n_heads):
        g = h // hpg
        acc_ref[h] = alphas[h] * acc_ref[h] + jnp.dot(vt_ref[g * VT_ROWS:(g + 1) * VT_ROWS, :], p_ref[h],
                                                      preferred_element_type=jnp.float32)

    @pl.when(kj == nk_ref[p] - 1)
    def _():
        for h in range(n_heads):
            cs = slice(h * HEAD_DIM, (h + 1) * HEAD_DIM)
            o = (acc_ref[h, 0:HEAD_DIM, :] / acc_ref[h, HEAD_DIM:HEAD_DIM + 1, :]).T
            o_ref[:, cs] = (o * _silu(zb_ref[:, cs])).astype(o_ref.dtype)


def _masked_attention_t(qrt, h_main, zb_col, krb, vt, st, thr, steps, *, tq, kb):
    qidx, kidx, nk = steps
    d_b, seq = qrt.shape
    n_heads = d_b // HEAD_DIM
    kv_w = N_KV * HEAD_DIM
    grid_spec = pltpu.PrefetchScalarGridSpec(
        num_scalar_prefetch=3,
        grid=(qidx.shape[0],),
        in_specs=[pl.BlockSpec((d_b, tq), lambda p, q, k, n: (0, q[p])),
                  pl.BlockSpec((tq, d_b), lambda p, q, k, n: (q[p], zb_col)),
                  pl.BlockSpec((kb, kv_w), lambda p, q, k, n: (k[p], 0)),
                  pl.BlockSpec((N_KV * VT_ROWS, kb), lambda p, q, k, n: (0, k[p])),
                  pl.BlockSpec((kb, tq), lambda p, q, k, n: (k[p], q[p])),
                  pl.BlockSpec((8, tq), lambda p, q, k, n: (0, q[p]))],
        out_specs=pl.BlockSpec((tq, d_b), lambda p, q, k, n: (q[p], 0)),
        scratch_shapes=[pltpu.VMEM((n_heads, 1, tq), jnp.float32),
                        pltpu.VMEM((n_heads, VT_ROWS, tq), jnp.float32),
                        pltpu.VMEM((n_heads, kb, tq), jnp.float32),
                        pltpu.VMEM((n_heads, kb, tq), jnp.bfloat16),
                        pltpu.VMEM((kb, tq), jnp.float32)],
    )
    return pl.pallas_call(
        functools.partial(_attn_kernel_t, n_heads=n_heads),
        grid_spec=grid_spec,
        out_shape=jax.ShapeDtypeStruct((seq, d_b), jnp.bfloat16),
        compiler_params=_params(("arbitrary",)),
        name="masked_attention_t",
    )(qidx, kidx, nk, qrt, h_main, krb, vt, st, thr)


def _attn_kernel(b_ref, qi_ref, kj_ref, nk_ref, q_ref, zb_ref, kt_ref, v_ref, s_ref, thr_ref, o_ref,
                 m_ref, l_ref, acc_ref, bias_ref, *, n_heads):
    p = pl.program_id(0)
    kj = kj_ref[p]
    hpg = n_heads // N_KV

    @pl.when(kj == 0)
    def _():
        m_ref[...] = jnp.full(m_ref.shape, NEG_INF, jnp.float32)
        l_ref[...] = jnp.zeros(l_ref.shape, jnp.float32)
        acc_ref[...] = jnp.zeros(acc_ref.shape, jnp.float32)

    bias_ref[...] = jnp.where(s_ref[...] >= thr_ref[:, 0:1], 0.0, NEG_INF)

    for h in range(n_heads):
        g = h // hpg
        s = jnp.dot(q_ref[:, h * HEAD_DIM:(h + 1) * HEAD_DIM], kt_ref[g],
                    preferred_element_type=jnp.float32) + bias_ref[...]
        m_prev = m_ref[h]
        m_cur = jnp.maximum(m_prev, jnp.max(s, axis=1, keepdims=True))
        m_safe = jnp.where(m_cur == NEG_INF, 0.0, m_cur)
        pr = jnp.exp2(s - m_safe)
        alpha = jnp.exp2(m_prev - m_safe)
        l_ref[h] = alpha * l_ref[h] + jnp.sum(pr, axis=1, keepdims=True)
        acc_ref[h] = alpha * acc_ref[h] + jnp.dot(pr.astype(v_ref.dtype), v_ref[:, g * HEAD_DIM:(g + 1) * HEAD_DIM],
                                                  preferred_element_type=jnp.float32)
        m_ref[h] = m_cur

    @pl.when(kj == nk_ref[p] - 1)
    def _():
        for h in range(n_heads):
            cs = slice(h * HEAD_DIM, (h + 1) * HEAD_DIM)
            o = acc_ref[h] / l_ref[h]
            o_ref[:, cs] = (o * _silu(zb_ref[:, cs])).astype(o_ref.dtype)


def _masked_attention(qr, h_main, zb_col, kt, vb, scores, thr, steps, *, zrow0, nq, tq, kb, d_b):
    bidx, qidx, kidx, nk = steps
    batches, _, _, sk = kt.shape
    zrb0 = zrow0 // tq
    n_heads = d_b // HEAD_DIM
    kv_w = N_KV * HEAD_DIM
    grid_spec = pltpu.PrefetchScalarGridSpec(
        num_scalar_prefetch=4,
        grid=(bidx.shape[0],),
        in_specs=[pl.BlockSpec((tq, d_b), lambda p, b, q, k, n: (b[p] * nq + q[p], 0)),
                  pl.BlockSpec((tq, d_b), lambda p, b, q, k, n: (zrb0 + b[p] * nq + q[p], zb_col)),
                  pl.BlockSpec((None, N_KV, HEAD_DIM, kb), lambda p, b, q, k, n: (b[p], 0, 0, k[p])),
                  pl.BlockSpec((None, kb, kv_w), lambda p, b, q, k, n: (b[p], k[p], 0)),
                  pl.BlockSpec((None, tq, kb), lambda p, b, q, k, n: (b[p], q[p], k[p])),
                  pl.BlockSpec((None, tq, LANES), lambda p, b, q, k, n: (b[p], q[p], 0))],
        out_specs=pl.BlockSpec((tq, d_b), lambda p, b, q, k, n: (b[p] * nq + q[p], 0)),
        scratch_shapes=[pltpu.VMEM((n_heads, tq, 1), jnp.float32),
                        pltpu.VMEM((n_heads, tq, 1), jnp.float32),
                        pltpu.VMEM((n_heads, tq, HEAD_DIM), jnp.float32),
                        pltpu.VMEM((tq, kb), jnp.float32)],
    )
    return pl.pallas_call(
        functools.partial(_attn_kernel, n_heads=n_heads),
        grid_spec=grid_spec,
        out_shape=jax.ShapeDtypeStruct((batches * nq * tq, d_b), jnp.bfloat16),
        compiler_params=_params(("arbitrary",)),
        name="masked_attention",
    )(bidx, qidx, kidx, nk, qr, h_main, kt, vb, scores, thr)


def _out_kernel(mixa_ref, mixb_ref, w_ref, x_ref, g_ref, o_ref, *, ka):
    k = pl.program_id(1)

    @pl.when(k == 0)
    def _():
        o_ref[...] = jnp.zeros(o_ref.shape, jnp.float32)

    @pl.when(k < ka)
    def _():
        o_ref[...] += jnp.dot(mixa_ref[...], w_ref[...], preferred_element_type=jnp.float32)

    @pl.when(k >= ka)
    def _():
        o_ref[...] += jnp.dot(mixb_ref[...], w_ref[...], preferred_element_type=jnp.float32)

    @pl.when(k == pl.num_programs(1) - 1)
    def _():
        y = x_ref[...] + o_ref[...]
        ms = jnp.mean(y * y, axis=-1, keepdims=True)
        o_ref[...] = y * lax.rsqrt(ms + EPS) * g_ref[...]


def _out_proj(mix_a, mix_b, w, x, g):
    n, d = x.shape
    d_a, d_b = mix_a.shape[1], mix_b.shape[1]
    tm = _largest_divisor(n, (512, 256, 128))
    tk = _largest_divisor(np.gcd(d_a, d_b), (512, 256, 128))
    ka, kb = d_a // tk, d_b // tk
    return pl.pallas_call(
        functools.partial(_out_kernel, ka=ka),
        grid=(n // tm, ka + kb),
        in_specs=[pl.BlockSpec((tm, tk), lambda i, k: (i, jnp.minimum(k, ka - 1))),
                  pl.BlockSpec((tm, tk), lambda i, k: (i, jnp.maximum(k - ka, 0))),
                  pl.BlockSpec((tk, d), lambda i, k: (k, 0)),
                  pl.BlockSpec((tm, d), lambda i, k: (i, 0)), pl.BlockSpec((1, d), lambda i, k: (0, 0))],
        out_specs=pl.BlockSpec((tm, d), lambda i, k: (i, 0)),
        out_shape=jax.ShapeDtypeStruct((n, d), jnp.float32),
        compiler_params=_params(("arbitrary", "arbitrary")),
        name="out_proj",
    )(mix_a, mix_b, w, x, g.reshape(1, d))


def _cdiv(a, b):
    return -(-a // b)


def _attention_steps(batches, nq, tq, kb, causal, sk):
    b_l, q_l, k_l, n_l = [], [], [], []
    for b in range(batches):
        for q in range(nq):
            nk = _cdiv((q + 1) * tq, kb) if causal else sk // kb
            for k in range(nk):
                b_l.append(b), q_l.append(q), k_l.append(k), n_l.append(nk)
    return tuple(jnp.asarray(np.asarray(v, np.int32)) for v in (b_l, q_l, k_l, n_l))


def _layer(x_prompt, x_sample, cache_k, cache_v, cache_ki, g_norm, w_in, w_s, b_s, g_v, w_out, final_g):
    f32, bf = jnp.float32, jnp.bfloat16
    _, seq, d_model = x_prompt.shape
    dec_b, dec_t, _ = x_sample.shape
    past = cache_k.shape[1]
    d_a = d_model // 2
    d_b = d_model - d_a
    groups = d_a // HEAD_DIM
    kv_w = N_KV * HEAD_DIM
    idx_w = IDX_HEADS * IDX_DIM
    n_s = dec_b * dec_t
    n_all = seq + n_s
    assert seq % 512 == 0 and n_s % MLP_CHUNK == 0 and dec_t <= CHUNK and past % MLP_CHUNK == 0
    assert idx_w % d_a == 0 and d_a % HEAD_DIM == 0

    x_p, x_s = x_prompt[0], x_sample.reshape(n_s, d_model)
    pts = np.cumsum((d_a, d_a, d_a, d_b, kv_w, kv_w, d_b, idx_w, IDX_DIM, IDX_HEADS))[:-1].tolist()
    w_u, w_va, w_za, w_q, w_k, w_v, w_zb, w_qi, w_ki, w_wi = jnp.split(w_in, pts, axis=1)
    w_main = jnp.concatenate([w_qi, w_u, w_va, w_za, w_q, w_zb], axis=1).astype(bf)
    small_w = 2 * kv_w + IDX_DIM + LANES
    w_small = jnp.concatenate([w_k, w_v, w_ki, w_wi, jnp.zeros((d_model, LANES - IDX_HEADS), f32)], axis=1).astype(bf)
    u_col = idx_w // d_a
    q_col, zb_col = u_col + 3, u_col + 4

    pos = jnp.concatenate([jnp.arange(seq), past + jnp.tile(jnp.arange(dec_t), dec_b)])
    half = HEAD_DIM // 2
    inv = ROPE_THETA ** (-2.0 * jnp.arange(half, dtype=f32) / HEAD_DIM)
    ang = pos.astype(f32)[:, None] * inv[None, :]
    cos_t = jnp.concatenate([jnp.cos(ang), jnp.cos(ang)], axis=1)
    sin_t = jnp.concatenate([-jnp.sin(ang), jnp.sin(ang)], axis=1)

    xn = jnp.concatenate([_rmsnorm(x_p, g_norm, bf), _rmsnorm(x_s, g_norm, bf)], axis=0)
    h_main = _matmul(xn, w_main, f32)
    h_small = _matmul(xn, w_small, f32)

    pidx = np.arange(MLP_CHUNK)
    mask_p = (pidx[None, :] // CHUNK) <= (pidx[:, None] // CHUNK)
    wm_p = jnp.where(mask_p[None], w_s, 0.0).astype(bf)
    b_p = b_s[:, :, None]
    tidx = pidx % dec_t
    same = (pidx[None, :] // dec_t) == (pidx[:, None] // dec_t)
    mask_s = same & ((tidx[None, :] // CHUNK) <= (tidx[:, None] // CHUNK))
    wm_s = jnp.where(mask_s[None], w_s[:, tidx][:, :, tidx], 0.0).astype(bf)
    b_sm = b_s[:, tidx][:, :, None]
    gv = g_v.reshape(1, d_a)
    (mix_a_p,) = _mlp_group(h_main, u_col, d_a, 0, seq, 512, wm_p, b_p, gv, False)
    mix_a_s, vn_s = _mlp_group(h_main, u_col, d_a, seq, n_s, MLP_CHUNK, wm_s, b_sm, gv, True)

    prep = functools.partial(_prep, h_main, h_small, cos_t, sin_t, d_b, q_col)
    qirt, qrt, kr_p, krb_p, vt_p, kir_p, kirb_p, wist = prep(row0=0, nrows=seq, transposed=True)
    qir_s, qr_s, kr_s, krb_s, vb_s, kir_s, kirb_s, wis_s = prep(row0=seq, nrows=n_s, transposed=False)
    v_all = h_small[:, kv_w:2 * kv_w]

    tq_p, kb_i, kb_a = PROMPT_TQ, PROMPT_KB_INDEX, PROMPT_KB_ATTN
    k_sel_p = min(TOPK_MAX, seq // 4)
    st_p, thr_p = _index_scores_t(qirt, wist, kirb_p, tq=tq_p, kb=kb_i, k_sel=k_sel_p)
    steps_p = _attention_steps(1, seq // tq_p, tq_p, kb_a, True, seq)[1:]
    mix_b_p = _masked_attention_t(qrt, h_main, zb_col, krb_p, vt_p, st_p, thr_p, steps_p, tq=tq_p, kb=kb_a)

    kb_s = SAMPLE_KB_INDEX
    n_keys = past + dec_t
    sk_s = _cdiv(n_keys, kb_s) * kb_s
    padk = lambda a: jnp.pad(a, ((0, 0), (0, sk_s - n_keys), (0, 0)))
    k_s = padk(jnp.concatenate([cache_k.reshape(dec_b, past, kv_w).astype(bf), krb_s.reshape(dec_b, dec_t, kv_w)], axis=1))
    v_s = padk(jnp.concatenate([cache_v.reshape(dec_b, past, kv_w).astype(bf), vb_s.reshape(dec_b, dec_t, kv_w)], axis=1))
    ki_s = padk(jnp.concatenate([cache_ki.astype(bf), kirb_s.reshape(dec_b, dec_t, IDX_DIM)], axis=1))
    kit_s = ki_s.transpose(0, 2, 1)
    kt_s = k_s.reshape(dec_b, sk_s, N_KV, HEAD_DIM).transpose(0, 2, 3, 1)
    kpos = jnp.arange(sk_s, dtype=jnp.int32)
    kc_s = jnp.broadcast_to(jnp.where(kpos < n_keys, kpos // CHUNK, FAR_CHUNK)[None, None, :], (dec_b, 1, sk_s))
    qc_s = (pos[seq:] // CHUNK).astype(jnp.int32)[:, None]
    k_sel_s = min(TOPK_MAX, n_keys // 4)
    nkb_s = jnp.full((dec_b,), sk_s // kb_s, jnp.int32)
    s_s, thr_s = _index_scores(qir_s, wis_s, qc_s, kit_s, kc_s, nkb_s, row0=0, batches=dec_b, nq=1, tq=dec_t,
                               kb=kb_s, k_sel=k_sel_s)
    steps_s = _attention_steps(dec_b, 1, dec_t, sk_s, False, sk_s)
    mix_b_s = _masked_attention(qr_s, h_main, zb_col, kt_s, v_s, s_s, thr_s, steps_s,
                                zrow0=seq, nq=1, tq=dec_t, kb=sk_s, d_b=d_b)

    w_out_b = w_out.astype(bf)
    y_p = _out_proj(mix_a_p, mix_b_p, w_out_b, x_p, final_g)
    y_s = _out_proj(mix_a_s, mix_b_s, w_out_b, x_s, final_g)

    shp_p = (1, 1, seq, N_KV, HEAD_DIM)
    shp_s = (1, dec_b, dec_t, N_KV, HEAD_DIM)
    return (y_p[None], y_s.reshape(dec_b, dec_t, d_model),
            kr_p.reshape(shp_p), v_all[:seq].reshape(shp_p), kir_p.reshape(1, 1, seq, IDX_DIM),
            kr_s.reshape(shp_s), v_all[seq:].reshape(shp_s), kir_s.reshape(1, dec_b, dec_t, IDX_DIM),
            vn_s.reshape(1, dec_b, dec_t, groups, HEAD_DIM))


def kernel(x_prompt, x_sample, cache_k, cache_v, cache_idx_k, norm_g, w_in, w_s, b_s, v_norm_g, w_out, final_norm_g):
    assert x_prompt.shape[0] == 1 and norm_g.shape[0] == 1, "one prompt stream and one layer"
    return _layer(x_prompt, x_sample, cache_k[0], cache_v[0], cache_idx_k[0], norm_g[0], w_in[0], w_s[0], b_s[0],
                  v_norm_g[0], w_out[0], final_norm_g)
```

```python
import functools

import numpy as np
import jax
import jax.numpy as jnp
from jax import lax
from jax.experimental import pallas as pl
from jax.experimental.pallas import tpu as pltpu

CHUNK = 64
MLP_CHUNK = 128
HEAD_DIM = 128
N_KV = 4
IDX_HEADS = 32
IDX_DIM = 128
TOPK_MAX = 256
ROPE_THETA = 10000.0
EPS = 1e-6

LANES = 128
VMEM_LIMIT = 56 * 1024 * 1024
NEG_INF = float("-inf")
POS_INF = float("inf")
FAR_CHUNK = 1 << 20
LOG2_E = 1.4426950408889634
CHUNK_SHIFT = 6
PROMPT_TQ, PROMPT_KB_INDEX, PROMPT_KB_ATTN = 256, 256, 1024
ONES_ROWS = 16
VT_ROWS = HEAD_DIM + ONES_ROWS
ATTN_ROW_CHUNK = 128
COUNT_TILES = 1
TAKE_ALL = 1.0e9
RANK_CHUNK = 256
EXP_RANGE = 60.0
NORM_MARGIN = 1.001
SAMPLE_KB_INDEX = 256


def _largest_divisor(n, candidates):
    for c in candidates:
        if n % c == 0:
            return c
    raise ValueError(f"no tile in {candidates} divides {n}")


def _params(sem, flags=None):
    return pltpu.CompilerParams(dimension_semantics=sem, vmem_limit_bytes=VMEM_LIMIT, flags=flags)


def _rmsnorm_kernel(x_ref, g_ref, o_ref):
    x = x_ref[...]
    ms = jnp.mean(x * x, axis=-1, keepdims=True)
    o_ref[...] = (x * lax.rsqrt(ms + EPS) * g_ref[...]).astype(o_ref.dtype)


def _rmsnorm(x, g, out_dtype):
    n, d = x.shape
    tm = _largest_divisor(n, (256, 128))
    return pl.pallas_call(
        _rmsnorm_kernel,
        grid=(n // tm,),
        in_specs=[pl.BlockSpec((tm, d), lambda i: (i, 0)), pl.BlockSpec((1, d), lambda i: (0, 0))],
        out_specs=pl.BlockSpec((tm, d), lambda i: (i, 0)),
        out_shape=jax.ShapeDtypeStruct((n, d), out_dtype),
        compiler_params=_params(("arbitrary",)),
        name="rmsnorm",
    )(x, g.reshape(1, d))


def _mm_kernel(x_ref, w_ref, o_ref):
    o_ref[...] = jnp.dot(x_ref[...], w_ref[...], preferred_element_type=jnp.float32).astype(o_ref.dtype)


def _matmul(x, w, out_dtype):
    m, k = x.shape
    n = w.shape[1]
    tm = _largest_divisor(m, (768, 512, 384, 256, 128))
    tn = _largest_divisor(n, (1024, 1280, 512, 256, 128))
    return pl.pallas_call(
        _mm_kernel,
        grid=(m // tm, n // tn),
        in_specs=[pl.BlockSpec((tm, k), lambda i, j: (i, 0)), pl.BlockSpec((k, tn), lambda i, j: (0, j))],
        out_specs=pl.BlockSpec((tm, tn), lambda i, j: (i, j)),
        out_shape=jax.ShapeDtypeStruct((m, n), out_dtype),
        compiler_params=_params(("arbitrary", "arbitrary")),
        name="in_proj",
    )(x, w)


def _silu(z):
    return z * (1.0 / (1.0 + jnp.exp(-z)))


def _mlp_kernel(u_ref, va_ref, za_ref, wm_ref, b_ref, gv_ref, mix_ref, *vn_refs, rows, groups):
    for c in range(rows // MLP_CHUNK):
        rs = slice(c * MLP_CHUNK, (c + 1) * MLP_CHUNK)
        for g in range(groups):
            cs = slice(g * HEAD_DIM, (g + 1) * HEAD_DIM)
            va = va_ref[rs, cs]
            mu = jnp.mean(va, axis=-1, keepdims=True)
            cen = va - mu
            var = jnp.mean(cen * cen, axis=-1, keepdims=True)
            vn = cen * lax.rsqrt(var + EPS) * gv_ref[:, cs]
            if vn_refs:
                vn_refs[0][rs, cs] = vn
            mixed = jnp.dot(wm_ref[g], vn.astype(jnp.bfloat16), preferred_element_type=jnp.float32) + b_ref[g]
            a = u_ref[rs, cs] * mixed
            mix_ref[rs, cs] = (a * _silu(za_ref[rs, cs])).astype(mix_ref.dtype)


def _mlp_group(h_main, col0, d_a, row0, nrows, rows, wm, bias, gv, emit_vn):
    groups = d_a // HEAD_DIM
    rb0 = row0 // rows
    out_shape = [jax.ShapeDtypeStruct((nrows, d_a), jnp.bfloat16)]
    out_specs = [pl.BlockSpec((rows, d_a), lambda i: (i, 0))]
    if emit_vn:
        out_shape.append(jax.ShapeDtypeStruct((nrows, d_a), jnp.float32))
        out_specs.append(pl.BlockSpec((rows, d_a), lambda i: (i, 0)))
    col = lambda c: pl.BlockSpec((rows, d_a), lambda i: (rb0 + i, c))
    return pl.pallas_call(
        functools.partial(_mlp_kernel, rows=rows, groups=groups),
        grid=(nrows // rows,),
        in_specs=[col(col0), col(col0 + 1), col(col0 + 2),
                  pl.BlockSpec((groups, MLP_CHUNK, MLP_CHUNK), lambda i: (0, 0, 0)),
                  pl.BlockSpec((groups, MLP_CHUNK, 1), lambda i: (0, 0, 0)),
                  pl.BlockSpec((1, d_a), lambda i: (0, 0))],
        out_specs=out_specs,
        out_shape=out_shape,
        compiler_params=_params(("arbitrary",)),
        name="chunk_mlp",
    )(h_main, h_main, h_main, wm, bias, gv)


def _rope(x, cos, sin):
    return x * cos + pltpu.roll(x, HEAD_DIM // 2, 1) * sin


def _prep_kernel(qi_ref, q_ref, k_ref, v_ref, ki_ref, wi_ref, cos_ref, sin_ref,
                 qir_ref, qr_ref, kr_ref, krb_ref, vb_ref, kir_ref, kirb_ref, wis_ref, *norm_refs,
                 n_heads, transposed):
    cos = cos_ref[...]
    sin = sin_ref[...]

    def put(ref, h, width, val):
        if transposed:
            ref[h * width:(h + 1) * width, :] = val.T.astype(ref.dtype)
        else:
            ref[:, h * width:(h + 1) * width] = val.astype(ref.dtype)

    for h in range(IDX_HEADS):
        put(qir_ref, h, IDX_DIM, _rope(qi_ref[:, h * IDX_DIM:(h + 1) * IDX_DIM], cos, sin))
    q_scale = HEAD_DIM ** -0.5 * LOG2_E
    for h in range(n_heads):
        qs = _rope(q_ref[:, h * HEAD_DIM:(h + 1) * HEAD_DIM], cos, sin) * q_scale
        put(qr_ref, h, HEAD_DIM, qs)
        if transposed:
            qb = qs.T.astype(qr_ref.dtype).astype(jnp.float32)
            norm_refs[0][h:h + 1, :] = jnp.sqrt(jnp.sum(qb * qb, axis=0, keepdims=True)) * NORM_MARGIN
    lane = lax.broadcasted_iota(jnp.int32, (8, LANES), 1)
    kn = jnp.zeros((8, LANES), jnp.float32)
    for h in range(N_KV):
        cs = slice(h * HEAD_DIM, (h + 1) * HEAD_DIM)
        kr = _rope(k_ref[:, cs], cos, sin)
        kr_ref[:, cs] = kr
        krb_ref[:, cs] = kr.astype(krb_ref.dtype)
        if transposed:
            vb_ref[h * VT_ROWS:h * VT_ROWS + HEAD_DIM, :] = v_ref[:, cs].T.astype(vb_ref.dtype)
            vb_ref[h * VT_ROWS + HEAD_DIM:(h + 1) * VT_ROWS, :] = jnp.ones((ONES_ROWS, v_ref.shape[0]), vb_ref.dtype)
            kb16 = kr.astype(krb_ref.dtype).astype(jnp.float32)
            kmax = jnp.sqrt(jnp.max(jnp.sum(kb16 * kb16, axis=1, keepdims=True), axis=0, keepdims=True))
            kn = jnp.where(lane == h, kmax, kn)
        else:
            vb_ref[:, cs] = v_ref[:, cs].astype(vb_ref.dtype)
    if transposed:
        norm_refs[1][0] = kn
    kir = _rope(ki_ref[...], cos, sin)
    kir_ref[...] = kir
    kirb_ref[...] = kir.astype(kirb_ref.dtype)
    put(wis_ref, 0, LANES, wi_ref[...] * ((IDX_HEADS * IDX_DIM) ** -0.5))


def _prep(h_main, h_small, cos_t, sin_t, d_b, q_col, *, row0, nrows, transposed):
    rows = _largest_divisor(nrows, (256, 128))
    rb0 = row0 // rows
    n_heads = d_b // HEAD_DIM
    idx_w = IDX_HEADS * IDX_DIM
    kv_w = N_KV * HEAD_DIM
    inspec = lambda w, c: pl.BlockSpec((rows, w), lambda i: (rb0 + i, c))
    bf, f32 = jnp.bfloat16, jnp.float32
    outs = [(idx_w, bf, True), (d_b, bf, True), (kv_w, f32, False), (kv_w, bf, False), (kv_w, bf, True),
            (IDX_DIM, f32, False), (IDX_DIM, bf, False), (LANES, f32, True)]
    out_specs, out_shape = [], []
    for idx, (w, dt, feature_major) in enumerate(outs):
        if transposed and feature_major:
            w = N_KV * VT_ROWS if idx == 4 else w
            out_specs.append(pl.BlockSpec((w, rows), lambda i: (0, i)))
            out_shape.append(jax.ShapeDtypeStruct((w, nrows), dt))
        else:
            out_specs.append(pl.BlockSpec((rows, w), lambda i: (i, 0)))
            out_shape.append(jax.ShapeDtypeStruct((nrows, w), dt))
    if transposed:
        out_specs += [pl.BlockSpec((n_heads, rows), lambda i: (0, i)), pl.BlockSpec((1, 8, LANES), lambda i: (i, 0, 0))]
        out_shape += [jax.ShapeDtypeStruct((n_heads, nrows), f32), jax.ShapeDtypeStruct((nrows // rows, 8, LANES), f32)]
    return pl.pallas_call(
        functools.partial(_prep_kernel, n_heads=n_heads, transposed=transposed),
        grid=(nrows // rows,),
        in_specs=[inspec(idx_w, 0), inspec(d_b, q_col),
                  inspec(kv_w, 0), inspec(kv_w, 1), inspec(IDX_DIM, 2 * kv_w // IDX_DIM),
                  inspec(LANES, 2 * kv_w // LANES + 1), inspec(HEAD_DIM, 0), inspec(HEAD_DIM, 0)],
        out_specs=out_specs,
        out_shape=out_shape,
        compiler_params=_params(("arbitrary",)),
        name="rope_prep",
    )(h_main, h_main, h_small, h_small, h_small, h_small, cos_t, sin_t)


def _kth_threshold(count, mn, mx, nf, k_sel):
    kk = jnp.minimum(nf, float(k_sel))
    done0 = jnp.where(nf <= float(k_sel), 1.0, 0.0)

    def cond(st):
        return st[5] > 0.5

    def body(st):
        lo, hi, thr, done, tie, _ = st
        mid = 0.5 * lo + 0.5 * hi
        cnt = count(mid, False)
        active = done < 0.5
        hit = jnp.logical_and(active, cnt == kk)
        no_room = jnp.logical_or(mid <= lo, mid >= hi)
        stuck = jnp.logical_and(active, jnp.logical_and(no_room, cnt != kk))
        thr = jnp.where(hit, mid, thr)
        tie = jnp.where(stuck, 1.0, tie)
        lo = jnp.where(jnp.logical_and(active, cnt > kk), mid, lo)
        hi = jnp.where(jnp.logical_and(active, cnt < kk), mid, hi)
        done = jnp.where(jnp.logical_or(hit, stuck), 1.0, done)
        return lo, hi, thr, done, tie, jnp.sum(1.0 - done)

    init = (mn, mx, mn, done0, jnp.zeros_like(mn), jnp.sum(1.0 - done0))
    lo, hi, thr, _, tie, _ = lax.while_loop(cond, body, init)
    any_tie = jnp.sum(tie) > 0.5

    def resolve(_):
        is_tie = tie > 0.5
        kth = jnp.where(count(hi, False) >= kk, hi, lo)
        thr_t = jnp.where(is_tie, kth, thr)
        taken = jnp.where(is_tie, kk - count(thr_t, True), TAKE_ALL)
        return thr_t, taken

    thr, taken = lax.cond(any_tie, resolve, lambda _: (thr, jnp.full_like(thr, TAKE_ALL)), 0)
    return thr, taken, any_tie


def _index_kernel(nkb_ref, qi_ref, wi_ref, qc_ref, kit_ref, kc_ref, s_ref, thr_ref, taken_ref, tie_ref, wb_ref,
                  *, tq, kb, sk, k_sel):
    b = pl.program_id(0)
    i = pl.program_id(1)
    nkb = nkb_ref[b * pl.num_programs(1) + i]
    lane_tiles = kb // LANES
    rt = min(tq, 128)

    for h in range(IDX_HEADS):
        wb_ref[h] = jnp.broadcast_to(wi_ref[:, h:h + 1], (tq, LANES))

    def score_tile(j, carry):
        off = pl.multiple_of(j * kb, kb)
        kt = kit_ref[:, pl.ds(off, kb)]
        kc = kc_ref[:, pl.ds(off, kb)]
        for r0 in range(0, tq, rt):
            accs = [jnp.zeros((rt, LANES), jnp.float32) for _ in range(lane_tiles)]
            for h in range(IDX_HEADS):
                r = jnp.dot(qi_ref[r0:r0 + rt, h * IDX_DIM:(h + 1) * IDX_DIM], kt,
                            preferred_element_type=jnp.float32)
                w = wb_ref[h, r0:r0 + rt, :]
                for l in range(lane_tiles):
                    accs[l] = accs[l] + jnp.maximum(r[:, l * LANES:(l + 1) * LANES], 0.0) * w
            acc = jnp.concatenate(accs, axis=1)
            adm = kc <= qc_ref[r0:r0 + rt, :]
            s_ref[r0:r0 + rt, pl.ds(off, kb)] = jnp.where(adm, acc, NEG_INF)
        return carry

    lax.fori_loop(0, nkb, score_tile, 0)

    def fill_tile(j, carry):
        off = pl.multiple_of(j * kb, kb)
        s_ref[:, pl.ds(off, kb)] = jnp.full((tq, kb), NEG_INF, jnp.float32)
        return carry

    lax.fori_loop(nkb, sk // kb, fill_tile, 0)

    def stats_tile(j, carry):
        mx, mn, nf = carry
        off = pl.multiple_of(j * kb, kb)
        for l in range(lane_tiles):
            s = s_ref[:, pl.ds(off + l * LANES, LANES)]
            fin = s > NEG_INF
            mx = jnp.maximum(mx, s)
            mn = jnp.minimum(mn, jnp.where(fin, s, POS_INF))
            nf = nf + jnp.where(fin, 1.0, 0.0)
        return mx, mn, nf

    mx, mn, nf = lax.fori_loop(
        0, nkb, stats_tile,
        (jnp.full((tq, LANES), NEG_INF, jnp.float32), jnp.full((tq, LANES), POS_INF, jnp.float32),
         jnp.zeros((tq, LANES), jnp.float32)))
    mx = jnp.max(mx, axis=1, keepdims=True)
    mn = jnp.min(mn, axis=1, keepdims=True)
    nf = jnp.sum(nf, axis=1, keepdims=True)

    def count(v, strict):
        def tile(j, cnt):
            off = pl.multiple_of(j * kb, kb)
            for l in range(lane_tiles):
                s = s_ref[:, pl.ds(off + l * LANES, LANES)]
                cnt = cnt + jnp.where(s > v if strict else s >= v, 1.0, 0.0)
            return cnt
        cnt = lax.fori_loop(0, nkb, tile, jnp.zeros((tq, LANES), jnp.float32))
        return jnp.sum(cnt, axis=1, keepdims=True)

    thr, taken, any_tie = _kth_threshold(count, mn, mx, nf, k_sel)
    thr_ref[...] = jnp.broadcast_to(thr, (tq, LANES))
    taken_ref[...] = jnp.broadcast_to(taken, (tq, LANES))
    tie_ref[b * pl.num_programs(1) + i] = jnp.where(any_tie, 1, 0).astype(jnp.int32)


def _index_scores(qir, wis, qc, kit, kc, nkb, *, row0, batches, nq, tq, kb, k_sel):
    sk = kit.shape[-1]
    rb0 = row0 // tq
    grid_spec = pltpu.PrefetchScalarGridSpec(
        num_scalar_prefetch=1,
        grid=(batches, nq),
        in_specs=[pl.BlockSpec((tq, IDX_HEADS * IDX_DIM), lambda b, i, n: (rb0 + b * nq + i, 0)),
                  pl.BlockSpec((tq, LANES), lambda b, i, n: (rb0 + b * nq + i, 0)),
                  pl.BlockSpec((tq, 1), lambda b, i, n: (rb0 + b * nq + i, 0)),
                  pl.BlockSpec((None, IDX_DIM, sk), lambda b, i, n: (b, 0, 0)),
                  pl.BlockSpec((None, 1, sk), lambda b, i, n: (b, 0, 0))],
        out_specs=[pl.BlockSpec((None, tq, sk), lambda b, i, n: (b, i, 0)),
                   pl.BlockSpec((None, tq, LANES), lambda b, i, n: (b, i, 0)),
                   pl.BlockSpec((None, tq, LANES), lambda b, i, n: (b, i, 0)),
                   pl.BlockSpec(memory_space=pltpu.SMEM)],
        scratch_shapes=[pltpu.VMEM((IDX_HEADS, tq, LANES), jnp.float32)],
    )
    return pl.pallas_call(
        functools.partial(_index_kernel, tq=tq, kb=kb, sk=sk, k_sel=k_sel),
        grid_spec=grid_spec,
        out_shape=[jax.ShapeDtypeStruct((batches, nq * tq, sk), jnp.float32),
                   jax.ShapeDtypeStruct((batches, nq * tq, LANES), jnp.float32),
                   jax.ShapeDtypeStruct((batches, nq * tq, LANES), jnp.float32),
                   jax.ShapeDtypeStruct((batches * nq,), jnp.int32)],
        compiler_params=_params(("arbitrary", "arbitrary")),
        name="index_select",
    )(nkb, qir, wis, qc, kit, kc)


def _fold8(x, op):
    parts = [x[g * 8:(g + 1) * 8, :] for g in range(x.shape[0] // 8)]
    while len(parts) > 1:
        parts = [op(parts[a], parts[a + 1]) for a in range(0, len(parts) - 1, 2)] + (parts[-1:] if len(parts) % 2 else [])
    return parts[0]


def _index_kernel_t(qit_ref, wit_ref, ki_ref, st_ref, thr_ref, taken_ref, tie_ref, *, tq, kb, k_sel):
    i = pl.program_id(0)
    nkb = lax.div((i + 1) * tq + (kb - 1), kb)
    total = st_ref.shape[0] // kb
    qchunk = lax.shift_right_logical(i * tq + lax.broadcasted_iota(jnp.int32, (kb, tq), 1), CHUNK_SHIFT)
    krow = lax.broadcasted_iota(jnp.int32, (kb, tq), 0)

    def score_tile(j, carry):
        off = pl.multiple_of(j * kb, kb)
        k_t = ki_ref[pl.ds(off, kb), :]
        acc = jnp.zeros((kb, tq), jnp.float32)
        for h in range(IDX_HEADS):
            r = jnp.dot(k_t, qit_ref[h * IDX_DIM:(h + 1) * IDX_DIM, :], preferred_element_type=jnp.float32)
            acc = acc + jnp.maximum(r, 0.0) * wit_ref[h:h + 1, :]
        adm = lax.shift_right_logical(off + krow, CHUNK_SHIFT) <= qchunk
        st_ref[pl.ds(off, kb), :] = jnp.where(adm, acc, NEG_INF)
        return carry

    lax.fori_loop(0, nkb, score_tile, 0)

    def fill_tile(j, carry):
        off = pl.multiple_of(j * kb, kb)
        st_ref[pl.ds(off, kb), :] = jnp.full((kb, tq), NEG_INF, jnp.float32)
        return carry

    lax.fori_loop(nkb, total, fill_tile, 0)

    def stats_tile(j, carry):
        mx, mn, nf = carry
        off = pl.multiple_of(j * kb, kb)
        s = st_ref[pl.ds(off, kb), :]
        fin = s > NEG_INF
        mx = jnp.maximum(mx, _fold8(s, jnp.maximum))
        mn = jnp.minimum(mn, _fold8(jnp.where(fin, s, POS_INF), jnp.minimum))
        nf = nf + _fold8(jnp.where(fin, 1.0, 0.0), jnp.add)
        return mx, mn, nf

    mx, mn, nf = lax.fori_loop(
        0, nkb, stats_tile,
        (jnp.full((8, tq), NEG_INF, jnp.float32), jnp.full((8, tq), POS_INF, jnp.float32),
         jnp.zeros((8, tq), jnp.float32)))
    mx = jnp.max(mx, axis=0, keepdims=True)
    mn = jnp.min(mn, axis=0, keepdims=True)
    nf = jnp.sum(nf, axis=0, keepdims=True)

    cb = COUNT_TILES * kb

    def count(v, strict):
        def tile(j, cnt):
            off = pl.multiple_of(j * cb, cb)
            s = st_ref[pl.ds(off, cb), :]
            return cnt + _fold8(jnp.where(s > v if strict else s >= v, 1.0, 0.0), jnp.add)
        cnt = lax.fori_loop(0, nkb // COUNT_TILES, tile, jnp.zeros((8, tq), jnp.float32))
        return jnp.sum(cnt, axis=0, keepdims=True)

    thr, taken, any_tie = _kth_threshold(count, mn, mx, nf, k_sel)
    thr_ref[...] = jnp.broadcast_to(thr, (8, tq))
    taken_ref[...] = jnp.broadcast_to(taken, (8, tq))
    tie_ref[i] = jnp.where(any_tie, 1, 0).astype(jnp.int32)


def _index_scores_t(qirt, wist, kirb, *, tq, kb, k_sel):
    idx_w, seq = qirt.shape
    nq = seq // tq
    return pl.pallas_call(
        functools.partial(_index_kernel_t, tq=tq, kb=kb, k_sel=k_sel),
        grid=(nq,),
        in_specs=[pl.BlockSpec((idx_w, tq), lambda i: (0, i)),
                  pl.BlockSpec((LANES, tq), lambda i: (0, i)),
                  pl.BlockSpec((seq, IDX_DIM), lambda i: (0, 0))],
        out_specs=[pl.BlockSpec((seq, tq), lambda i: (0, i)),
                   pl.BlockSpec((8, tq), lambda i: (0, i)),
                   pl.BlockSpec((8, tq), lambda i: (0, i)),
                   pl.BlockSpec(memory_space=pltpu.SMEM)],
        out_shape=[jax.ShapeDtypeStruct((seq, seq), jnp.float32),
                   jax.ShapeDtypeStruct((8, seq), jnp.float32),
                   jax.ShapeDtypeStruct((8, seq), jnp.float32),
                   jax.ShapeDtypeStruct((nq,), jnp.int32)],
        compiler_params=_params(("arbitrary",)),
        name="index_select_t",
    )(qirt, wist, kirb)


def _tie_rank_matrix(n, lower):
    rows = lax.broadcasted_iota(jnp.int32, (n, n), 0)
    cols = lax.broadcasted_iota(jnp.int32, (n, n), 1)
    return jnp.where(cols < rows if lower else rows < cols, 1.0, 0.0).astype(jnp.bfloat16)


def _attn_kernel_t(qi_ref, kj_ref, nk_ref, tie_ref, qt_ref, zb_ref, k_ref, vt_ref, st_ref, thr_ref, taken_ref,
                   qn_ref, kn_ref, o_ref, m_ref, c_ref, acc_ref, s_ref, p_ref, bias_ref, seen_ref, *, n_heads):
    p = pl.program_id(0)
    kj = kj_ref[p]
    hpg = n_heads // N_KV

    @pl.when(kj == 0)
    def _():
        m_ref[...] = jnp.full(m_ref.shape, NEG_INF, jnp.float32)
        c_ref[...] = jnp.zeros(c_ref.shape, jnp.float32)
        acc_ref[...] = jnp.zeros(acc_ref.shape, jnp.float32)
        seen_ref[...] = jnp.zeros(seen_ref.shape, jnp.float32)

    kb = st_ref.shape[0]
    rc = ATTN_ROW_CHUNK
    thr = thr_ref[0:1, :]
    for r in range(0, kb, rc):
        bias_ref[r:r + rc, :] = jnp.where(st_ref[r:r + rc, :] >= thr, 0.0, NEG_INF)

    @pl.when(tie_ref[qi_ref[p]] != 0)
    def _():
        taken = taken_ref[0:1, :]
        ltri = _tie_rank_matrix(RANK_CHUNK, True)
        seen = seen_ref[...]
        for r in range(0, kb, RANK_CHUNK):
            s = st_ref[r:r + RANK_CHUNK, :]
            eq = s == thr
            rank = jnp.dot(ltri, jnp.where(eq, 1.0, 0.0).astype(jnp.bfloat16),
                           preferred_element_type=jnp.float32) + seen
            take = jnp.logical_or(s > thr, jnp.logical_and(eq, rank < taken))
            bias_ref[r:r + RANK_CHUNK, :] = jnp.where(take, 0.0, NEG_INF)
            seen = seen + jnp.sum(jnp.where(eq, 1.0, 0.0), axis=0, keepdims=True)
        seen_ref[...] = seen

    def logits(h, r):
        g = h // hpg
        return jnp.dot(k_ref[r:r + rc, g * HEAD_DIM:(g + 1) * HEAD_DIM], qt_ref[h * HEAD_DIM:(h + 1) * HEAD_DIM, :],
                       preferred_element_type=jnp.float32) + bias_ref[r:r + rc, :]

    def weighted_values(h):
        g = h // hpg
        return jnp.dot(vt_ref[g * VT_ROWS:(g + 1) * VT_ROWS, :], p_ref[h], preferred_element_type=jnp.float32)

    first = kj == 0
    shift0 = [jnp.where(m_ref[h] == NEG_INF, 0.0, m_ref[h]) for h in range(n_heads)]
    kn = kn_ref[0]
    for blk in range(1, kn_ref.shape[0]):
        kn = jnp.maximum(kn, kn_ref[blk])
    excess = None
    for h in range(n_heads):
        bound = qn_ref[h:h + 1, :] * kn[0:1, h // hpg:h // hpg + 1]
        e = bound - shift0[h]
        excess = e if excess is None else jnp.maximum(excess, e)
    in_range = jnp.max(excess) <= EXP_RANGE

    @pl.when(in_range)
    def _():
        for h in range(n_heads):
            cm = None
            for r in range(0, kb, rc):
                t = logits(h, r) - shift0[h]
                p_ref[h, r:r + rc, :] = jnp.exp2(t).astype(p_ref.dtype)
                c8 = _fold8(t, jnp.maximum)
                cm = c8 if cm is None else jnp.maximum(cm, c8)
            m_ref[h] = jnp.maximum(m_ref[h], shift0[h] + jnp.max(cm, axis=0, keepdims=True))
        for h in range(n_heads):
            alpha = jnp.where(first, 1.0, jnp.exp2(c_ref[h] - shift0[h]))
            acc_ref[h] = alpha * acc_ref[h] + weighted_values(h)
            c_ref[h] = shift0[h]

    @pl.when(jnp.logical_not(in_range))
    def _():
        cmax = []
        for h in range(n_heads):
            cm = None
            for r in range(0, kb, rc):
                s = logits(h, r)
                s_ref[h, r:r + rc, :] = s
                c8 = _fold8(s, jnp.maximum)
                cm = c8 if cm is None else jnp.maximum(cm, c8)
            cmax.append(jnp.max(cm, axis=0, keepdims=True))
        for h in range(n_heads):
            m_cur = jnp.maximum(m_ref[h], cmax[h])
            shift = jnp.where(m_cur == NEG_INF, 0.0, m_cur)
            for r in range(0, kb, rc):
                p_ref[h, r:r + rc, :] = jnp.exp2(s_ref[h, r:r + rc, :] - shift).astype(p_ref.dtype)
            alpha = jnp.where(first, 1.0, jnp.exp2(c_ref[h] - shift))
            acc_ref[h] = alpha * acc_ref[h] + weighted_values(h)
            c_ref[h] = shift
            m_ref[h] = m_cur

    @pl.when(kj == nk_ref[p] - 1)
    def _():
        for h in range(n_heads):
            cs = slice(h * HEAD_DIM, (h + 1) * HEAD_DIM)
            o = (acc_ref[h, 0:HEAD_DIM, :] / acc_ref[h, HEAD_DIM:HEAD_DIM + 1, :]).T
            o_ref[:, cs] = (o * _silu(zb_ref[:, cs])).astype(o_ref.dtype)


def _masked_attention_t(qrt, h_main, zb_col, krb, vt, st, thr, taken, ties, qn, kn, steps, *, tq, kb):
    qidx, kidx, nk = steps
    d_b, seq = qrt.shape
    n_heads = d_b // HEAD_DIM
    kv_w = N_KV * HEAD_DIM
    kn_blocks = kb * kn.shape[0] // seq
    assert kb % RANK_CHUNK == 0 and kb % ATTN_ROW_CHUNK == 0 and kn_blocks >= 1
    grid_spec = pltpu.PrefetchScalarGridSpec(
        num_scalar_prefetch=4,
        grid=(qidx.shape[0],),
        in_specs=[pl.BlockSpec((d_b, tq), lambda p, q, k, n, t: (0, q[p])),
                  pl.BlockSpec((tq, d_b), lambda p, q, k, n, t: (q[p], zb_col)),
                  pl.BlockSpec((kb, kv_w), lambda p, q, k, n, t: (k[p], 0)),
                  pl.BlockSpec((N_KV * VT_ROWS, kb), lambda p, q, k, n, t: (0, k[p])),
                  pl.BlockSpec((kb, tq), lambda p, q, k, n, t: (k[p], q[p])),
                  pl.BlockSpec((8, tq), lambda p, q, k, n, t: (0, q[p])),
                  pl.BlockSpec((8, tq), lambda p, q, k, n, t: (0, q[p])),
                  pl.BlockSpec((n_heads, tq), lambda p, q, k, n, t: (0, q[p])),
                  pl.BlockSpec((kn_blocks, 8, LANES), lambda p, q, k, n, t: (k[p], 0, 0))],
        out_specs=pl.BlockSpec((tq, d_b), lambda p, q, k, n, t: (q[p], 0)),
        scratch_shapes=[pltpu.VMEM((n_heads, 1, tq), jnp.float32),
                        pltpu.VMEM((n_heads, 1, tq), jnp.float32),
                        pltpu.VMEM((n_heads, VT_ROWS, tq), jnp.float32),
                        pltpu.VMEM((n_heads, kb, tq), jnp.float32),
                        pltpu.VMEM((n_heads, kb, tq), jnp.bfloat16),
                        pltpu.VMEM((kb, tq), jnp.float32),
                        pltpu.VMEM((1, tq), jnp.float32)],
    )
    return pl.pallas_call(
        functools.partial(_attn_kernel_t, n_heads=n_heads),
        grid_spec=grid_spec,
        out_shape=jax.ShapeDtypeStruct((seq, d_b), jnp.bfloat16),
        compiler_params=_params(("arbitrary",)),
        name="masked_attention_t",
    )(qidx, kidx, nk, ties, qrt, h_main, krb, vt, st, thr, taken, qn, kn)


def _attn_kernel(tie_ref, q_ref, zb_ref, kt_ref, v_ref, s_ref, thr_ref, taken_ref, o_ref, bias_ref, *, n_heads):
    b = pl.program_id(0)
    hpg = n_heads // N_KV
    sk = s_ref.shape[1]
    thr = thr_ref[:, 0:1]
    bias_ref[...] = jnp.where(s_ref[...] >= thr, 0.0, NEG_INF)

    @pl.when(tie_ref[b] != 0)
    def _():
        taken = taken_ref[:, 0:1]
        utri = _tie_rank_matrix(RANK_CHUNK, False)
        seen = jnp.zeros_like(thr)
        for c in range(0, sk, RANK_CHUNK):
            s = s_ref[:, c:c + RANK_CHUNK]
            eq = s == thr
            rank = jnp.dot(jnp.where(eq, 1.0, 0.0).astype(jnp.bfloat16), utri,
                           preferred_element_type=jnp.float32) + seen
            take = jnp.logical_or(s > thr, jnp.logical_and(eq, rank < taken))
            bias_ref[:, c:c + RANK_CHUNK] = jnp.where(take, 0.0, NEG_INF)
            seen = seen + jnp.sum(jnp.where(eq, 1.0, 0.0), axis=1, keepdims=True)

    for h in range(n_heads):
        g = h // hpg
        cs = slice(h * HEAD_DIM, (h + 1) * HEAD_DIM)
        s = jnp.dot(q_ref[:, cs], kt_ref[g], preferred_element_type=jnp.float32) + bias_ref[...]
        pr = jnp.exp2(s - jnp.max(s, axis=1, keepdims=True))
        o = jnp.dot(pr.astype(v_ref.dtype), v_ref[:, g * HEAD_DIM:(g + 1) * HEAD_DIM],
                    preferred_element_type=jnp.float32) / jnp.sum(pr, axis=1, keepdims=True)
        o_ref[:, cs] = (o * _silu(zb_ref[:, cs])).astype(o_ref.dtype)


def _masked_attention(qr, h_main, zb_col, kt, vb, scores, thr, taken, ties, *, zrow0, tq, d_b):
    batches, _, _, sk = kt.shape
    zrb0 = zrow0 // tq
    n_heads = d_b // HEAD_DIM
    kv_w = N_KV * HEAD_DIM
    assert sk % RANK_CHUNK == 0
    grid_spec = pltpu.PrefetchScalarGridSpec(
        num_scalar_prefetch=1,
        grid=(batches,),
        in_specs=[pl.BlockSpec((tq, d_b), lambda b, t: (b, 0)),
                  pl.BlockSpec((tq, d_b), lambda b, t: (zrb0 + b, zb_col)),
                  pl.BlockSpec((None, N_KV, HEAD_DIM, sk), lambda b, t: (b, 0, 0, 0)),
                  pl.BlockSpec((None, sk, kv_w), lambda b, t: (b, 0, 0)),
                  pl.BlockSpec((None, tq, sk), lambda b, t: (b, 0, 0)),
                  pl.BlockSpec((None, tq, LANES), lambda b, t: (b, 0, 0)),
                  pl.BlockSpec((None, tq, LANES), lambda b, t: (b, 0, 0))],
        out_specs=pl.BlockSpec((tq, d_b), lambda b, t: (b, 0)),
        scratch_shapes=[pltpu.VMEM((tq, sk), jnp.float32)],
    )
    return pl.pallas_call(
        functools.partial(_attn_kernel, n_heads=n_heads),
        grid_spec=grid_spec,
        out_shape=jax.ShapeDtypeStruct((batches * tq, d_b), jnp.bfloat16),
        compiler_params=_params(("arbitrary",)),
        name="masked_attention",
    )(ties, qr, h_main, kt, vb, scores, thr, taken)


def _out_kernel(mixa_ref, mixb_ref, w_ref, x_ref, g_ref, o_ref, *, ka):
    k = pl.program_id(1)

    @pl.when(k == 0)
    def _():
        o_ref[...] = jnp.zeros(o_ref.shape, jnp.float32)

    @pl.when(k < ka)
    def _():
        o_ref[...] += jnp.dot(mixa_ref[...], w_ref[...], preferred_element_type=jnp.float32)

    @pl.when(k >= ka)
    def _():
        o_ref[...] += jnp.dot(mixb_ref[...], w_ref[...], preferred_element_type=jnp.float32)

    @pl.when(k == pl.num_programs(1) - 1)
    def _():
        y = x_ref[...] + o_ref[...]
        ms = jnp.mean(y * y, axis=-1, keepdims=True)
        o_ref[...] = y * lax.rsqrt(ms + EPS) * g_ref[...]


def _out_proj(mix_a, mix_b, w, x, g):
    n, d = x.shape
    d_a, d_b = mix_a.shape[1], mix_b.shape[1]
    tm = _largest_divisor(n, (512, 256, 128))
    tk = _largest_divisor(np.gcd(d_a, d_b), (512, 256, 128))
    ka, kb = d_a // tk, d_b // tk
    return pl.pallas_call(
        functools.partial(_out_kernel, ka=ka),
        grid=(n // tm, ka + kb),
        in_specs=[pl.BlockSpec((tm, tk), lambda i, k: (i, jnp.minimum(k, ka - 1))),
                  pl.BlockSpec((tm, tk), lambda i, k: (i, jnp.maximum(k - ka, 0))),
                  pl.BlockSpec((tk, d), lambda i, k: (k, 0)),
                  pl.BlockSpec((tm, d), lambda i, k: (i, 0)), pl.BlockSpec((1, d), lambda i, k: (0, 0))],
        out_specs=pl.BlockSpec((tm, d), lambda i, k: (i, 0)),
        out_shape=jax.ShapeDtypeStruct((n, d), jnp.float32),
        compiler_params=_params(("arbitrary", "arbitrary")),
        name="out_proj",
    )(mix_a, mix_b, w, x, g.reshape(1, d))


def _cdiv(a, b):
    return -(-a // b)


def _attention_steps(nq, tq, kb):
    q_l, k_l, n_l = [], [], []
    for q in range(nq):
        nk = _cdiv((q + 1) * tq, kb)
        for k in range(nk):
            q_l.append(q), k_l.append(k), n_l.append(nk)
    return tuple(jnp.asarray(np.asarray(v, np.int32)) for v in (q_l, k_l, n_l))


def _layer(x_prompt, x_sample, cache_k, cache_v, cache_ki, g_norm, w_in, w_s, b_s, g_v, w_out, final_g):
    f32, bf = jnp.float32, jnp.bfloat16
    _, seq, d_model = x_prompt.shape
    dec_b, dec_t, _ = x_sample.shape
    past = cache_k.shape[1]
    d_a = d_model // 2
    d_b = d_model - d_a
    groups = d_a // HEAD_DIM
    kv_w = N_KV * HEAD_DIM
    idx_w = IDX_HEADS * IDX_DIM
    n_s = dec_b * dec_t
    n_all = seq + n_s
    assert seq % 512 == 0 and n_s % MLP_CHUNK == 0 and dec_t <= CHUNK and past % MLP_CHUNK == 0
    assert idx_w % d_a == 0 and d_a % HEAD_DIM == 0

    x_p, x_s = x_prompt[0], x_sample.reshape(n_s, d_model)
    pts = np.cumsum((d_a, d_a, d_a, d_b, kv_w, kv_w, d_b, idx_w, IDX_DIM, IDX_HEADS))[:-1].tolist()
    w_u, w_va, w_za, w_q, w_k, w_v, w_zb, w_qi, w_ki, w_wi = jnp.split(w_in, pts, axis=1)
    w_main = jnp.concatenate([w_qi, w_u, w_va, w_za, w_q, w_zb], axis=1).astype(bf)
    small_w = 2 * kv_w + IDX_DIM + LANES
    w_small = jnp.concatenate([w_k, w_v, w_ki, w_wi, jnp.zeros((d_model, LANES - IDX_HEADS), f32)], axis=1).astype(bf)
    u_col = idx_w // d_a
    q_col, zb_col = u_col + 3, u_col + 4

    pos = jnp.concatenate([jnp.arange(seq), past + jnp.tile(jnp.arange(dec_t), dec_b)])
    half = HEAD_DIM // 2
    inv = ROPE_THETA ** (-2.0 * jnp.arange(half, dtype=f32) / HEAD_DIM)
    ang = pos.astype(f32)[:, None] * inv[None, :]
    cos_t = jnp.concatenate([jnp.cos(ang), jnp.cos(ang)], axis=1)
    sin_t = jnp.concatenate([-jnp.sin(ang), jnp.sin(ang)], axis=1)

    xn = jnp.concatenate([_rmsnorm(x_p, g_norm, bf), _rmsnorm(x_s, g_norm, bf)], axis=0)
    h_main = _matmul(xn, w_main, f32)
    h_small = _matmul(xn, w_small, f32)

    pidx = np.arange(MLP_CHUNK)
    mask_p = (pidx[None, :] // CHUNK) <= (pidx[:, None] // CHUNK)
    wm_p = jnp.where(mask_p[None], w_s, 0.0).astype(bf)
    b_p = b_s[:, :, None]
    tidx = pidx % dec_t
    same = (pidx[None, :] // dec_t) == (pidx[:, None] // dec_t)
    mask_s = same & ((tidx[None, :] // CHUNK) <= (tidx[:, None] // CHUNK))
    wm_s = jnp.where(mask_s[None], w_s[:, tidx][:, :, tidx], 0.0).astype(bf)
    b_sm = b_s[:, tidx][:, :, None]
    gv = g_v.reshape(1, d_a)
    (mix_a_p,) = _mlp_group(h_main, u_col, d_a, 0, seq, 512, wm_p, b_p, gv, False)
    mix_a_s, vn_s = _mlp_group(h_main, u_col, d_a, seq, n_s, MLP_CHUNK, wm_s, b_sm, gv, True)

    prep = functools.partial(_prep, h_main, h_small, cos_t, sin_t, d_b, q_col)
    qirt, qrt, kr_p, krb_p, vt_p, kir_p, kirb_p, wist, qn_p, kn_p = prep(row0=0, nrows=seq, transposed=True)
    qir_s, qr_s, kr_s, krb_s, vb_s, kir_s, kirb_s, wis_s = prep(row0=seq, nrows=n_s, transposed=False)
    v_all = h_small[:, kv_w:2 * kv_w]

    tq_p, kb_i, kb_a = PROMPT_TQ, PROMPT_KB_INDEX, PROMPT_KB_ATTN
    k_sel_p = min(TOPK_MAX, seq // 4)
    st_p, thr_p, taken_p, ties_p = _index_scores_t(qirt, wist, kirb_p, tq=tq_p, kb=kb_i, k_sel=k_sel_p)
    steps_p = _attention_steps(seq // tq_p, tq_p, kb_a)
    mix_b_p = _masked_attention_t(qrt, h_main, zb_col, krb_p, vt_p, st_p, thr_p, taken_p, ties_p, qn_p, kn_p,
                                  steps_p, tq=tq_p, kb=kb_a)

    kb_s = SAMPLE_KB_INDEX
    n_keys = past + dec_t
    sk_s = _cdiv(n_keys, kb_s) * kb_s
    padk = lambda a: jnp.pad(a, ((0, 0), (0, sk_s - n_keys), (0, 0)))
    k_s = padk(jnp.concatenate([cache_k.reshape(dec_b, past, kv_w).astype(bf), krb_s.reshape(dec_b, dec_t, kv_w)], axis=1))
    v_s = padk(jnp.concatenate([cache_v.reshape(dec_b, past, kv_w).astype(bf), vb_s.reshape(dec_b, dec_t, kv_w)], axis=1))
    ki_s = padk(jnp.concatenate([cache_ki.astype(bf), kirb_s.reshape(dec_b, dec_t, IDX_DIM)], axis=1))
    kit_s = ki_s.transpose(0, 2, 1)
    kt_s = k_s.reshape(dec_b, sk_s, N_KV, HEAD_DIM).transpose(0, 2, 3, 1)
    kpos = jnp.arange(sk_s, dtype=jnp.int32)
    kc_s = jnp.broadcast_to(jnp.where(kpos < n_keys, kpos // CHUNK, FAR_CHUNK)[None, None, :], (dec_b, 1, sk_s))
    qc_s = (pos[seq:] // CHUNK).astype(jnp.int32)[:, None]
    k_sel_s = min(TOPK_MAX, n_keys // 4)
    nkb_s = jnp.full((dec_b,), sk_s // kb_s, jnp.int32)
    s_s, thr_s, taken_s, ties_s = _index_scores(qir_s, wis_s, qc_s, kit_s, kc_s, nkb_s, row0=0, batches=dec_b, nq=1,
                                                tq=dec_t, kb=kb_s, k_sel=k_sel_s)
    mix_b_s = _masked_attention(qr_s, h_main, zb_col, kt_s, v_s, s_s, thr_s, taken_s, ties_s,
                                zrow0=seq, tq=dec_t, d_b=d_b)

    w_out_b = w_out.astype(bf)
    y_p = _out_proj(mix_a_p, mix_b_p, w_out_b, x_p, final_g)
    y_s = _out_proj(mix_a_s, mix_b_s, w_out_b, x_s, final_g)

    shp_p = (1, 1, seq, N_KV, HEAD_DIM)
    shp_s = (1, dec_b, dec_t, N_KV, HEAD_DIM)
    return (y_p[None], y_s.reshape(dec_b, dec_t, d_model),
            kr_p.reshape(shp_p), v_all[:seq].reshape(shp_p), kir_p.reshape(1, 1, seq, IDX_DIM),
            kr_s.reshape(shp_s), v_all[seq:].reshape(shp_s), kir_s.reshape(1, dec_b, dec_t, IDX_DIM),
            vn_s.reshape(1, dec_b, dec_t, groups, HEAD_DIM))


def kernel(x_prompt, x_sample, cache_k, cache_v, cache_idx_k, norm_g, w_in, w_s, b_s, v_norm_g, w_out, final_norm_g):
    assert x_prompt.shape[0] == 1 and norm_g.shape[0] == 1, "one prompt stream and one layer"
    return _layer(x_prompt, x_sample, cache_k[0], cache_v[0], cache_idx_k[0], norm_g[0], w_in[0], w_s[0], b_s[0],
                  v_norm_g[0], w_out[0], final_norm_g)
```

```python
import functools

import numpy as np
import jax
import jax.numpy as jnp
from jax import lax
from jax.experimental import pallas as pl
from jax.experimental.pallas import tpu as pltpu

CHUNK = 64
MLP_CHUNK = 128
HEAD_DIM = 128
N_KV = 4
IDX_HEADS = 32
IDX_DIM = 128
TOPK_MAX = 256
ROPE_THETA = 10000.0
EPS = 1e-6

LANES = 128
VMEM_LIMIT = 56 * 1024 * 1024
NEG_INF = float("-inf")
POS_INF = float("inf")
FAR_CHUNK = 1 << 20
LOG2_E = 1.4426950408889634
CHUNK_SHIFT = 6
PROMPT_TQ, PROMPT_KB_INDEX, PROMPT_KB_ATTN = 256, 256, 1024
ONES_ROWS = 16
VT_ROWS = HEAD_DIM + ONES_ROWS
ATTN_ROW_CHUNK = 128
COUNT_TILES = 1
TAKE_ALL = 1.0e9
RANK_CHUNK = 256
EXP_RANGE = 60.0
NORM_MARGIN = 1.001
SAMPLE_KB_INDEX = 256


def _largest_divisor(n, candidates):
    for c in candidates:
        if n % c == 0:
            return c
    raise ValueError(f"no tile in {candidates} divides {n}")


def _params(sem, flags=None):
    return pltpu.CompilerParams(dimension_semantics=sem, vmem_limit_bytes=VMEM_LIMIT, flags=flags)


def _in_proj_kernel(x_ref, g_ref, w_ref, o_ref, xn_ref):
    @pl.when(pl.program_id(1) == 0)
    def _():
        x = x_ref[...]
        ms = jnp.mean(x * x, axis=-1, keepdims=True)
        xn_ref[...] = (x * lax.rsqrt(ms + EPS) * g_ref[...]).astype(xn_ref.dtype)

    o_ref[...] = jnp.dot(xn_ref[...], w_ref[...], preferred_element_type=jnp.float32).astype(o_ref.dtype)


def _in_proj(x, g, w, out_dtype):
    m, k = x.shape
    n = w.shape[1]
    tm = _largest_divisor(m, (512, 256, 128))
    tn = _largest_divisor(n, (1024, 1280, 512, 256, 128))
    return pl.pallas_call(
        _in_proj_kernel,
        grid=(m // tm, n // tn),
        in_specs=[pl.BlockSpec((tm, k), lambda i, j: (i, 0)), pl.BlockSpec((1, k), lambda i, j: (0, 0)),
                  pl.BlockSpec((k, tn), lambda i, j: (0, j))],
        out_specs=pl.BlockSpec((tm, tn), lambda i, j: (i, j)),
        out_shape=jax.ShapeDtypeStruct((m, n), out_dtype),
        scratch_shapes=[pltpu.VMEM((tm, k), w.dtype)],
        compiler_params=_params(("arbitrary", "arbitrary")),
        name="in_proj",
    )(x, g.reshape(1, k), w)


def _silu(z):
    return z * (1.0 / (1.0 + jnp.exp(-z)))


def _mlp_kernel(u_ref, va_ref, za_ref, wm_ref, b_ref, gv_ref, mix_ref, *vn_refs, rows, groups):
    for c in range(rows // MLP_CHUNK):
        rs = slice(c * MLP_CHUNK, (c + 1) * MLP_CHUNK)
        for g in range(groups):
            cs = slice(g * HEAD_DIM, (g + 1) * HEAD_DIM)
            va = va_ref[rs, cs]
            mu = jnp.mean(va, axis=-1, keepdims=True)
            cen = va - mu
            var = jnp.mean(cen * cen, axis=-1, keepdims=True)
            vn = cen * lax.rsqrt(var + EPS) * gv_ref[:, cs]
            if vn_refs:
                vn_refs[0][rs, cs] = vn
            mixed = jnp.dot(wm_ref[g], vn.astype(jnp.bfloat16), preferred_element_type=jnp.float32) + b_ref[g]
            a = u_ref[rs, cs] * mixed
            mix_ref[rs, cs] = (a * _silu(za_ref[rs, cs])).astype(mix_ref.dtype)


def _mlp_group(h_main, col0, d_a, row0, nrows, rows, wm, bias, gv, emit_vn):
    groups = d_a // HEAD_DIM
    rb0 = row0 // rows
    out_shape = [jax.ShapeDtypeStruct((nrows, d_a), jnp.bfloat16)]
    out_specs = [pl.BlockSpec((rows, d_a), lambda i: (i, 0))]
    if emit_vn:
        out_shape.append(jax.ShapeDtypeStruct((nrows, d_a), jnp.float32))
        out_specs.append(pl.BlockSpec((rows, d_a), lambda i: (i, 0)))
    col = lambda c: pl.BlockSpec((rows, d_a), lambda i: (rb0 + i, c))
    return pl.pallas_call(
        functools.partial(_mlp_kernel, rows=rows, groups=groups),
        grid=(nrows // rows,),
        in_specs=[col(col0), col(col0 + 1), col(col0 + 2),
                  pl.BlockSpec((groups, MLP_CHUNK, MLP_CHUNK), lambda i: (0, 0, 0)),
                  pl.BlockSpec((groups, MLP_CHUNK, 1), lambda i: (0, 0, 0)),
                  pl.BlockSpec((1, d_a), lambda i: (0, 0))],
        out_specs=out_specs,
        out_shape=out_shape,
        compiler_params=_params(("arbitrary",)),
        name="chunk_mlp",
    )(h_main, h_main, h_main, wm, bias, gv)


def _rope(x, cos, sin):
    return x * cos + pltpu.roll(x, HEAD_DIM // 2, 1) * sin


def _prep_kernel(qi_ref, q_ref, k_ref, v_ref, ki_ref, wi_ref, cos_ref, sin_ref,
                 qir_ref, qr_ref, kr_ref, krb_ref, vb_ref, kir_ref, kirb_ref, wis_ref, *norm_refs,
                 n_heads, transposed):
    cos = cos_ref[...]
    sin = sin_ref[...]

    def put(ref, h, width, val):
        if transposed:
            ref[h * width:(h + 1) * width, :] = val.T.astype(ref.dtype)
        else:
            ref[:, h * width:(h + 1) * width] = val.astype(ref.dtype)

    for h in range(IDX_HEADS):
        put(qir_ref, h, IDX_DIM, _rope(qi_ref[:, h * IDX_DIM:(h + 1) * IDX_DIM], cos, sin))
    q_scale = HEAD_DIM ** -0.5 * LOG2_E
    for h in range(n_heads):
        qs = _rope(q_ref[:, h * HEAD_DIM:(h + 1) * HEAD_DIM], cos, sin) * q_scale
        put(qr_ref, h, HEAD_DIM, qs)
        if transposed:
            qb = qs.T.astype(qr_ref.dtype).astype(jnp.float32)
            norm_refs[0][h:h + 1, :] = jnp.sqrt(jnp.sum(qb * qb, axis=0, keepdims=True)) * NORM_MARGIN
    lane = lax.broadcasted_iota(jnp.int32, (8, LANES), 1)
    kn = jnp.zeros((8, LANES), jnp.float32)
    for h in range(N_KV):
        cs = slice(h * HEAD_DIM, (h + 1) * HEAD_DIM)
        kr = _rope(k_ref[:, cs], cos, sin)
        kr_ref[:, cs] = kr
        krb_ref[:, cs] = kr.astype(krb_ref.dtype)
        if transposed:
            vb_ref[h * VT_ROWS:h * VT_ROWS + HEAD_DIM, :] = v_ref[:, cs].T.astype(vb_ref.dtype)
            vb_ref[h * VT_ROWS + HEAD_DIM:(h + 1) * VT_ROWS, :] = jnp.ones((ONES_ROWS, v_ref.shape[0]), vb_ref.dtype)
            kb16 = kr.astype(krb_ref.dtype).astype(jnp.float32)
            kmax = jnp.sqrt(jnp.max(jnp.sum(kb16 * kb16, axis=1, keepdims=True), axis=0, keepdims=True))
            kn = jnp.where(lane == h, kmax, kn)
        else:
            vb_ref[:, cs] = v_ref[:, cs].astype(vb_ref.dtype)
    if transposed:
        norm_refs[1][0] = kn
    kir = _rope(ki_ref[...], cos, sin)
    kir_ref[...] = kir
    kirb_ref[...] = kir.astype(kirb_ref.dtype)
    put(wis_ref, 0, LANES, wi_ref[...] * ((IDX_HEADS * IDX_DIM) ** -0.5))


def _prep(h_main, h_small, cos_t, sin_t, d_b, q_col, *, row0, nrows, transposed):
    rows = _largest_divisor(nrows, (256, 128))
    rb0 = row0 // rows
    n_heads = d_b // HEAD_DIM
    idx_w = IDX_HEADS * IDX_DIM
    kv_w = N_KV * HEAD_DIM
    inspec = lambda w, c: pl.BlockSpec((rows, w), lambda i: (rb0 + i, c))
    bf, f32 = jnp.bfloat16, jnp.float32
    outs = [(idx_w, bf, True), (d_b, bf, True), (kv_w, f32, False), (kv_w, bf, False), (kv_w, bf, True),
            (IDX_DIM, f32, False), (IDX_DIM, bf, False), (LANES, f32, True)]
    out_specs, out_shape = [], []
    for idx, (w, dt, feature_major) in enumerate(outs):
        if transposed and feature_major:
            w = N_KV * VT_ROWS if idx == 4 else w
            out_specs.append(pl.BlockSpec((w, rows), lambda i: (0, i)))
            out_shape.append(jax.ShapeDtypeStruct((w, nrows), dt))
        else:
            out_specs.append(pl.BlockSpec((rows, w), lambda i: (i, 0)))
            out_shape.append(jax.ShapeDtypeStruct((nrows, w), dt))
    if transposed:
        out_specs += [pl.BlockSpec((n_heads, rows), lambda i: (0, i)), pl.BlockSpec((1, 8, LANES), lambda i: (i, 0, 0))]
        out_shape += [jax.ShapeDtypeStruct((n_heads, nrows), f32), jax.ShapeDtypeStruct((nrows // rows, 8, LANES), f32)]
    return pl.pallas_call(
        functools.partial(_prep_kernel, n_heads=n_heads, transposed=transposed),
        grid=(nrows // rows,),
        in_specs=[inspec(idx_w, 0), inspec(d_b, q_col),
                  inspec(kv_w, 0), inspec(kv_w, 1), inspec(IDX_DIM, 2 * kv_w // IDX_DIM),
                  inspec(LANES, 2 * kv_w // LANES + 1), inspec(HEAD_DIM, 0), inspec(HEAD_DIM, 0)],
        out_specs=out_specs,
        out_shape=out_shape,
        compiler_params=_params(("arbitrary",)),
        name="rope_prep",
    )(h_main, h_main, h_small, h_small, h_small, h_small, cos_t, sin_t)


def _kth_threshold(count, mn, mx, nf, k_sel):
    kk = jnp.minimum(nf, float(k_sel))
    done0 = jnp.where(nf <= float(k_sel), 1.0, 0.0)

    def cond(st):
        return st[5] > 0.5

    def body(st):
        lo, hi, thr, done, tie, _ = st
        mid = 0.5 * lo + 0.5 * hi
        cnt = count(mid, False)
        active = done < 0.5
        hit = jnp.logical_and(active, cnt == kk)
        no_room = jnp.logical_or(mid <= lo, mid >= hi)
        stuck = jnp.logical_and(active, jnp.logical_and(no_room, cnt != kk))
        thr = jnp.where(hit, mid, thr)
        tie = jnp.where(stuck, 1.0, tie)
        lo = jnp.where(jnp.logical_and(active, cnt > kk), mid, lo)
        hi = jnp.where(jnp.logical_and(active, cnt < kk), mid, hi)
        done = jnp.where(jnp.logical_or(hit, stuck), 1.0, done)
        return lo, hi, thr, done, tie, jnp.sum(1.0 - done)

    init = (mn, mx, mn, done0, jnp.zeros_like(mn), jnp.sum(1.0 - done0))
    lo, hi, thr, _, tie, _ = lax.while_loop(cond, body, init)
    any_tie = jnp.sum(tie) > 0.5

    def resolve(_):
        is_tie = tie > 0.5
        kth = jnp.where(count(hi, False) >= kk, hi, lo)
        thr_t = jnp.where(is_tie, kth, thr)
        taken = jnp.where(is_tie, kk - count(thr_t, True), TAKE_ALL)
        return thr_t, taken

    thr, taken = lax.cond(any_tie, resolve, lambda _: (thr, jnp.full_like(thr, TAKE_ALL)), 0)
    return thr, taken, any_tie


def _index_kernel(nkb_ref, qi_ref, wi_ref, qc_ref, kit_ref, kc_ref, s_ref, thr_ref, taken_ref, tie_ref, wb_ref,
                  *, tq, kb, sk, k_sel):
    b = pl.program_id(0)
    i = pl.program_id(1)
    nkb = nkb_ref[b * pl.num_programs(1) + i]
    lane_tiles = kb // LANES
    rt = min(tq, 128)

    for h in range(IDX_HEADS):
        wb_ref[h] = jnp.broadcast_to(wi_ref[:, h:h + 1], (tq, LANES))

    def score_tile(j, carry):
        off = pl.multiple_of(j * kb, kb)
        kt = kit_ref[:, pl.ds(off, kb)]
        kc = kc_ref[:, pl.ds(off, kb)]
        for r0 in range(0, tq, rt):
            accs = [jnp.zeros((rt, LANES), jnp.float32) for _ in range(lane_tiles)]
            for h in range(IDX_HEADS):
                r = jnp.dot(qi_ref[r0:r0 + rt, h * IDX_DIM:(h + 1) * IDX_DIM], kt,
                            preferred_element_type=jnp.float32)
                w = wb_ref[h, r0:r0 + rt, :]
                for l in range(lane_tiles):
                    accs[l] = accs[l] + jnp.maximum(r[:, l * LANES:(l + 1) * LANES], 0.0) * w
            acc = jnp.concatenate(accs, axis=1)
            adm = kc <= qc_ref[r0:r0 + rt, :]
            s_ref[r0:r0 + rt, pl.ds(off, kb)] = jnp.where(adm, acc, NEG_INF)
        return carry

    lax.fori_loop(0, nkb, score_tile, 0)

    def fill_tile(j, carry):
        off = pl.multiple_of(j * kb, kb)
        s_ref[:, pl.ds(off, kb)] = jnp.full((tq, kb), NEG_INF, jnp.float32)
        return carry

    lax.fori_loop(nkb, sk // kb, fill_tile, 0)

    def stats_tile(j, carry):
        mx, mn, nf = carry
        off = pl.multiple_of(j * kb, kb)
        for l in range(lane_tiles):
            s = s_ref[:, pl.ds(off + l * LANES, LANES)]
            fin = s > NEG_INF
            mx = jnp.maximum(mx, s)
            mn = jnp.minimum(mn, jnp.where(fin, s, POS_INF))
            nf = nf + jnp.where(fin, 1.0, 0.0)
        return mx, mn, nf

    mx, mn, nf = lax.fori_loop(
        0, nkb, stats_tile,
        (jnp.full((tq, LANES), NEG_INF, jnp.float32), jnp.full((tq, LANES), POS_INF, jnp.float32),
         jnp.zeros((tq, LANES), jnp.float32)))
    mx = jnp.max(mx, axis=1, keepdims=True)
    mn = jnp.min(mn, axis=1, keepdims=True)
    nf = jnp.sum(nf, axis=1, keepdims=True)

    def count(v, strict):
        def tile(j, cnt):
            off = pl.multiple_of(j * kb, kb)
            for l in range(lane_tiles):
                s = s_ref[:, pl.ds(off + l * LANES, LANES)]
                cnt = cnt + jnp.where(s > v if strict else s >= v, 1.0, 0.0)
            return cnt
        cnt = lax.fori_loop(0, nkb, tile, jnp.zeros((tq, LANES), jnp.float32))
        return jnp.sum(cnt, axis=1, keepdims=True)

    thr, taken, any_tie = _kth_threshold(count, mn, mx, nf, k_sel)
    thr_ref[...] = jnp.broadcast_to(thr, (tq, LANES))
    taken_ref[...] = jnp.broadcast_to(taken, (tq, LANES))
    tie_ref[b * pl.num_programs(1) + i] = jnp.where(any_tie, 1, 0).astype(jnp.int32)


def _index_scores(qir, wis, qc, kit, kc, nkb, *, row0, batches, nq, tq, kb, k_sel):
    sk = kit.shape[-1]
    rb0 = row0 // tq
    grid_spec = pltpu.PrefetchScalarGridSpec(
        num_scalar_prefetch=1,
        grid=(batches, nq),
        in_specs=[pl.BlockSpec((tq, IDX_HEADS * IDX_DIM), lambda b, i, n: (rb0 + b * nq + i, 0)),
                  pl.BlockSpec((tq, LANES), lambda b, i, n: (rb0 + b * nq + i, 0)),
                  pl.BlockSpec((tq, 1), lambda b, i, n: (rb0 + b * nq + i, 0)),
                  pl.BlockSpec((None, IDX_DIM, sk), lambda b, i, n: (b, 0, 0)),
                  pl.BlockSpec((None, 1, sk), lambda b, i, n: (b, 0, 0))],
        out_specs=[pl.BlockSpec((None, tq, sk), lambda b, i, n: (b, i, 0)),
                   pl.BlockSpec((None, tq, LANES), lambda b, i, n: (b, i, 0)),
                   pl.BlockSpec((None, tq, LANES), lambda b, i, n: (b, i, 0)),
                   pl.BlockSpec(memory_space=pltpu.SMEM)],
        scratch_shapes=[pltpu.VMEM((IDX_HEADS, tq, LANES), jnp.float32)],
    )
    return pl.pallas_call(
        functools.partial(_index_kernel, tq=tq, kb=kb, sk=sk, k_sel=k_sel),
        grid_spec=grid_spec,
        out_shape=[jax.ShapeDtypeStruct((batches, nq * tq, sk), jnp.float32),
                   jax.ShapeDtypeStruct((batches, nq * tq, LANES), jnp.float32),
                   jax.ShapeDtypeStruct((batches, nq * tq, LANES), jnp.float32),
                   jax.ShapeDtypeStruct((batches * nq,), jnp.int32)],
        compiler_params=_params(("arbitrary", "arbitrary")),
        name="index_select",
    )(nkb, qir, wis, qc, kit, kc)


def _fold8(x, op):
    parts = [x[g * 8:(g + 1) * 8, :] for g in range(x.shape[0] // 8)]
    while len(parts) > 1:
        parts = [op(parts[a], parts[a + 1]) for a in range(0, len(parts) - 1, 2)] + (parts[-1:] if len(parts) % 2 else [])
    return parts[0]


def _index_kernel_t(qit_ref, wit_ref, ki_ref, st_ref, thr_ref, taken_ref, tie_ref, *, tq, kb, k_sel):
    i = pl.program_id(0)
    nkb = lax.div((i + 1) * tq + (kb - 1), kb)
    total = st_ref.shape[0] // kb
    qchunk = lax.shift_right_logical(i * tq + lax.broadcasted_iota(jnp.int32, (kb, tq), 1), CHUNK_SHIFT)
    krow = lax.broadcasted_iota(jnp.int32, (kb, tq), 0)

    def score_tile(j, carry):
        mx, mn, nf = carry
        off = pl.multiple_of(j * kb, kb)
        k_t = ki_ref[pl.ds(off, kb), :]
        acc = jnp.zeros((kb, tq), jnp.float32)
        for h in range(IDX_HEADS):
            r = jnp.dot(k_t, qit_ref[h * IDX_DIM:(h + 1) * IDX_DIM, :], preferred_element_type=jnp.float32)
            acc = acc + jnp.maximum(r, 0.0) * wit_ref[h:h + 1, :]
        adm = lax.shift_right_logical(off + krow, CHUNK_SHIFT) <= qchunk
        s = jnp.where(adm, acc, NEG_INF)
        st_ref[pl.ds(off, kb), :] = s
        fin = s > NEG_INF
        mx = jnp.maximum(mx, _fold8(s, jnp.maximum))
        mn = jnp.minimum(mn, _fold8(jnp.where(fin, s, POS_INF), jnp.minimum))
        nf = nf + _fold8(jnp.where(fin, 1.0, 0.0), jnp.add)
        return mx, mn, nf

    mx, mn, nf = lax.fori_loop(
        0, nkb, score_tile,
        (jnp.full((8, tq), NEG_INF, jnp.float32), jnp.full((8, tq), POS_INF, jnp.float32),
         jnp.zeros((8, tq), jnp.float32)))

    def fill_tile(j, carry):
        off = pl.multiple_of(j * kb, kb)
        st_ref[pl.ds(off, kb), :] = jnp.full((kb, tq), NEG_INF, jnp.float32)
        return carry

    lax.fori_loop(nkb, total, fill_tile, 0)
    mx = jnp.max(mx, axis=0, keepdims=True)
    mn = jnp.min(mn, axis=0, keepdims=True)
    nf = jnp.sum(nf, axis=0, keepdims=True)

    cb = COUNT_TILES * kb

    def count(v, strict):
        def tile(j, cnt):
            off = pl.multiple_of(j * cb, cb)
            s = st_ref[pl.ds(off, cb), :]
            return cnt + _fold8(jnp.where(s > v if strict else s >= v, 1.0, 0.0), jnp.add)
        cnt = lax.fori_loop(0, nkb // COUNT_TILES, tile, jnp.zeros((8, tq), jnp.float32))
        return jnp.sum(cnt, axis=0, keepdims=True)

    thr, taken, any_tie = _kth_threshold(count, mn, mx, nf, k_sel)
    thr_ref[...] = jnp.broadcast_to(thr, (8, tq))
    taken_ref[...] = jnp.broadcast_to(taken, (8, tq))
    tie_ref[i] = jnp.where(any_tie, 1, 0).astype(jnp.int32)


def _index_scores_t(qirt, wist, kirb, *, tq, kb, k_sel):
    idx_w, seq = qirt.shape
    nq = seq // tq
    return pl.pallas_call(
        functools.partial(_index_kernel_t, tq=tq, kb=kb, k_sel=k_sel),
        grid=(nq,),
        in_specs=[pl.BlockSpec((idx_w, tq), lambda i: (0, i)),
                  pl.BlockSpec((LANES, tq), lambda i: (0, i)),
                  pl.BlockSpec((seq, IDX_DIM), lambda i: (0, 0))],
        out_specs=[pl.BlockSpec((seq, tq), lambda i: (0, i)),
                   pl.BlockSpec((8, tq), lambda i: (0, i)),
                   pl.BlockSpec((8, tq), lambda i: (0, i)),
                   pl.BlockSpec(memory_space=pltpu.SMEM)],
        out_shape=[jax.ShapeDtypeStruct((seq, seq), jnp.float32),
                   jax.ShapeDtypeStruct((8, seq), jnp.float32),
                   jax.ShapeDtypeStruct((8, seq), jnp.float32),
                   jax.ShapeDtypeStruct((nq,), jnp.int32)],
        compiler_params=_params(("arbitrary",)),
        name="index_select_t",
    )(qirt, wist, kirb)


def _tie_rank_matrix(n, lower):
    rows = lax.broadcasted_iota(jnp.int32, (n, n), 0)
    cols = lax.broadcasted_iota(jnp.int32, (n, n), 1)
    return jnp.where(cols < rows if lower else rows < cols, 1.0, 0.0).astype(jnp.bfloat16)


def _attn_kernel_t(qi_ref, kj_ref, nk_ref, tie_ref, qt_ref, zb_ref, k_ref, vt_ref, st_ref, thr_ref, taken_ref,
                   qn_ref, kn_ref, o_ref, m_ref, c_ref, acc_ref, s_ref, p_ref, bias_ref, seen_ref, *, n_heads):
    p = pl.program_id(0)
    kj = kj_ref[p]
    hpg = n_heads // N_KV

    @pl.when(kj == 0)
    def _():
        m_ref[...] = jnp.full(m_ref.shape, NEG_INF, jnp.float32)
        c_ref[...] = jnp.zeros(c_ref.shape, jnp.float32)
        acc_ref[...] = jnp.zeros(acc_ref.shape, jnp.float32)
        seen_ref[...] = jnp.zeros(seen_ref.shape, jnp.float32)

    kb = st_ref.shape[0]
    rc = ATTN_ROW_CHUNK
    thr = thr_ref[0:1, :]
    for r in range(0, kb, rc):
        bias_ref[r:r + rc, :] = jnp.where(st_ref[r:r + rc, :] >= thr, 0.0, NEG_INF)

    @pl.when(tie_ref[qi_ref[p]] != 0)
    def _():
        taken = taken_ref[0:1, :]
        ltri = _tie_rank_matrix(RANK_CHUNK, True)
        seen = seen_ref[...]
        for r in range(0, kb, RANK_CHUNK):
            s = st_ref[r:r + RANK_CHUNK, :]
            eq = s == thr
            rank = jnp.dot(ltri, jnp.where(eq, 1.0, 0.0).astype(jnp.bfloat16),
                           preferred_element_type=jnp.float32) + seen
            take = jnp.logical_or(s > thr, jnp.logical_and(eq, rank < taken))
            bias_ref[r:r + RANK_CHUNK, :] = jnp.where(take, 0.0, NEG_INF)
            seen = seen + jnp.sum(jnp.where(eq, 1.0, 0.0), axis=0, keepdims=True)
        seen_ref[...] = seen

    def logits(h, r):
        g = h // hpg
        return jnp.dot(k_ref[r:r + rc, g * HEAD_DIM:(g + 1) * HEAD_DIM], qt_ref[h * HEAD_DIM:(h + 1) * HEAD_DIM, :],
                       preferred_element_type=jnp.float32) + bias_ref[r:r + rc, :]

    def weighted_values(h):
        g = h // hpg
        return jnp.dot(vt_ref[g * VT_ROWS:(g + 1) * VT_ROWS, :], p_ref[h], preferred_element_type=jnp.float32)

    first = kj == 0
    shift0 = [jnp.where(m_ref[h] == NEG_INF, 0.0, m_ref[h]) for h in range(n_heads)]
    kn = kn_ref[0]
    for blk in range(1, kn_ref.shape[0]):
        kn = jnp.maximum(kn, kn_ref[blk])
    excess = None
    for h in range(n_heads):
        bound = qn_ref[h:h + 1, :] * kn[0:1, h // hpg:h // hpg + 1]
        e = bound - shift0[h]
        excess = e if excess is None else jnp.maximum(excess, e)
    in_range = jnp.max(excess) <= EXP_RANGE

    @pl.when(in_range)
    def _():
        for h in range(n_heads):
            cm = None
            for r in range(0, kb, rc):
                t = logits(h, r) - shift0[h]
                p_ref[h, r:r + rc, :] = jnp.exp2(t).astype(p_ref.dtype)
                c8 = _fold8(t, jnp.maximum)
                cm = c8 if cm is None else jnp.maximum(cm, c8)
            m_ref[h] = jnp.maximum(m_ref[h], shift0[h] + jnp.max(cm, axis=0, keepdims=True))
        for h in range(n_heads):
            alpha = jnp.where(first, 1.0, jnp.exp2(c_ref[h] - shift0[h]))
            acc_ref[h] = alpha * acc_ref[h] + weighted_values(h)
            c_ref[h] = shift0[h]

    @pl.when(jnp.logical_not(in_range))
    def _():
        cmax = []
        for h in range(n_heads):
            cm = None
            for r in range(0, kb, rc):
                s = logits(h, r)
                s_ref[h, r:r + rc, :] = s
                c8 = _fold8(s, jnp.maximum)
                cm = c8 if cm is None else jnp.maximum(cm, c8)
            cmax.append(jnp.max(cm, axis=0, keepdims=True))
        for h in range(n_heads):
            m_cur = jnp.maximum(m_ref[h], cmax[h])
            shift = jnp.where(m_cur == NEG_INF, 0.0, m_cur)
            for r in range(0, kb, rc):
                p_ref[h, r:r + rc, :] = jnp.exp2(s_ref[h, r:r + rc, :] - shift).astype(p_ref.dtype)
            alpha = jnp.where(first, 1.0, jnp.exp2(c_ref[h] - shift))
            acc_ref[h] = alpha * acc_ref[h] + weighted_values(h)
            c_ref[h] = shift
            m_ref[h] = m_cur

    @pl.when(kj == nk_ref[p] - 1)
    def _():
        for h in range(n_heads):
            cs = slice(h * HEAD_DIM, (h + 1) * HEAD_DIM)
            o = (acc_ref[h, 0:HEAD_DIM, :] / acc_ref[h, HEAD_DIM:HEAD_DIM + 1, :]).T
            o_ref[:, cs] = (o * _silu(zb_ref[:, cs])).astype(o_ref.dtype)


def _masked_attention_t(qrt, h_main, zb_col, krb, vt, st, thr, taken, ties, qn, kn, steps, *, tq, kb):
    qidx, kidx, nk = steps
    d_b, seq = qrt.shape
    n_heads = d_b // HEAD_DIM
    kv_w = N_KV * HEAD_DIM
    kn_blocks = kb * kn.shape[0] // seq
    assert kb % RANK_CHUNK == 0 and kb % ATTN_ROW_CHUNK == 0 and kn_blocks >= 1
    grid_spec = pltpu.PrefetchScalarGridSpec(
        num_scalar_prefetch=4,
        grid=(qidx.shape[0],),
        in_specs=[pl.BlockSpec((d_b, tq), lambda p, q, k, n, t: (0, q[p])),
                  pl.BlockSpec((tq, d_b), lambda p, q, k, n, t: (q[p], zb_col)),
                  pl.BlockSpec((kb, kv_w), lambda p, q, k, n, t: (k[p], 0)),
                  pl.BlockSpec((N_KV * VT_ROWS, kb), lambda p, q, k, n, t: (0, k[p])),
                  pl.BlockSpec((kb, tq), lambda p, q, k, n, t: (k[p], q[p])),
                  pl.BlockSpec((8, tq), lambda p, q, k, n, t: (0, q[p])),
                  pl.BlockSpec((8, tq), lambda p, q, k, n, t: (0, q[p])),
                  pl.BlockSpec((n_heads, tq), lambda p, q, k, n, t: (0, q[p])),
                  pl.BlockSpec((kn_blocks, 8, LANES), lambda p, q, k, n, t: (k[p], 0, 0))],
        out_specs=pl.BlockSpec((tq, d_b), lambda p, q, k, n, t: (q[p], 0)),
        scratch_shapes=[pltpu.VMEM((n_heads, 1, tq), jnp.float32),
                        pltpu.VMEM((n_heads, 1, tq), jnp.float32),
                        pltpu.VMEM((n_heads, VT_ROWS, tq), jnp.float32),
                        pltpu.VMEM((n_heads, kb, tq), jnp.float32),
                        pltpu.VMEM((n_heads, kb, tq), jnp.bfloat16),
                        pltpu.VMEM((kb, tq), jnp.float32),
                        pltpu.VMEM((1, tq), jnp.float32)],
    )
    return pl.pallas_call(
        functools.partial(_attn_kernel_t, n_heads=n_heads),
        grid_spec=grid_spec,
        out_shape=jax.ShapeDtypeStruct((seq, d_b), jnp.bfloat16),
        compiler_params=_params(("arbitrary",)),
        name="masked_attention_t",
    )(qidx, kidx, nk, ties, qrt, h_main, krb, vt, st, thr, taken, qn, kn)


def _attn_kernel(tie_ref, q_ref, zb_ref, kt_ref, v_ref, s_ref, thr_ref, taken_ref, o_ref, bias_ref, *, n_heads):
    b = pl.program_id(0)
    hpg = n_heads // N_KV
    sk = s_ref.shape[1]
    thr = thr_ref[:, 0:1]
    bias_ref[...] = jnp.where(s_ref[...] >= thr, 0.0, NEG_INF)

    @pl.when(tie_ref[b] != 0)
    def _():
        taken = taken_ref[:, 0:1]
        utri = _tie_rank_matrix(RANK_CHUNK, False)
        seen = jnp.zeros_like(thr)
        for c in range(0, sk, RANK_CHUNK):
            s = s_ref[:, c:c + RANK_CHUNK]
            eq = s == thr
            rank = jnp.dot(jnp.where(eq, 1.0, 0.0).astype(jnp.bfloat16), utri,
                           preferred_element_type=jnp.float32) + seen
            take = jnp.logical_or(s > thr, jnp.logical_and(eq, rank < taken))
            bias_ref[:, c:c + RANK_CHUNK] = jnp.where(take, 0.0, NEG_INF)
            seen = seen + jnp.sum(jnp.where(eq, 1.0, 0.0), axis=1, keepdims=True)

    for h in range(n_heads):
        g = h // hpg
        cs = slice(h * HEAD_DIM, (h + 1) * HEAD_DIM)
        s = jnp.dot(q_ref[:, cs], kt_ref[g], preferred_element_type=jnp.float32) + bias_ref[...]
        pr = jnp.exp2(s - jnp.max(s, axis=1, keepdims=True))
        o = jnp.dot(pr.astype(v_ref.dtype), v_ref[:, g * HEAD_DIM:(g + 1) * HEAD_DIM],
                    preferred_element_type=jnp.float32) / jnp.sum(pr, axis=1, keepdims=True)
        o_ref[:, cs] = (o * _silu(zb_ref[:, cs])).astype(o_ref.dtype)


def _masked_attention(qr, h_main, zb_col, kt, vb, scores, thr, taken, ties, *, zrow0, tq, d_b):
    batches, _, _, sk = kt.shape
    zrb0 = zrow0 // tq
    n_heads = d_b // HEAD_DIM
    kv_w = N_KV * HEAD_DIM
    assert sk % RANK_CHUNK == 0
    grid_spec = pltpu.PrefetchScalarGridSpec(
        num_scalar_prefetch=1,
        grid=(batches,),
        in_specs=[pl.BlockSpec((tq, d_b), lambda b, t: (b, 0)),
                  pl.BlockSpec((tq, d_b), lambda b, t: (zrb0 + b, zb_col)),
                  pl.BlockSpec((None, N_KV, HEAD_DIM, sk), lambda b, t: (b, 0, 0, 0)),
                  pl.BlockSpec((None, sk, kv_w), lambda b, t: (b, 0, 0)),
                  pl.BlockSpec((None, tq, sk), lambda b, t: (b, 0, 0)),
                  pl.BlockSpec((None, tq, LANES), lambda b, t: (b, 0, 0)),
                  pl.BlockSpec((None, tq, LANES), lambda b, t: (b, 0, 0))],
        out_specs=pl.BlockSpec((tq, d_b), lambda b, t: (b, 0)),
        scratch_shapes=[pltpu.VMEM((tq, sk), jnp.float32)],
    )
    return pl.pallas_call(
        functools.partial(_attn_kernel, n_heads=n_heads),
        grid_spec=grid_spec,
        out_shape=jax.ShapeDtypeStruct((batches * tq, d_b), jnp.bfloat16),
        compiler_params=_params(("arbitrary",)),
        name="masked_attention",
    )(ties, qr, h_main, kt, vb, scores, thr, taken)


def _out_kernel(mixa_ref, mixb_ref, w_ref, x_ref, g_ref, o_ref, ssq_ref, *, d_a, tn):
    j = pl.program_id(1)

    @pl.when(j == 0)
    def _():
        ssq_ref[...] = jnp.zeros(ssq_ref.shape, jnp.float32)

    y = (x_ref[...] + jnp.dot(mixa_ref[...], w_ref[0:d_a, :], preferred_element_type=jnp.float32)
         + jnp.dot(mixb_ref[...], w_ref[d_a:, :], preferred_element_type=jnp.float32))
    o_ref[:, pl.ds(pl.multiple_of(j * tn, tn), tn)] = y
    ssq_ref[...] += jnp.sum(y * y, axis=-1, keepdims=True)

    @pl.when(j == pl.num_programs(1) - 1)
    def _():
        inv = lax.rsqrt(ssq_ref[...] * (1.0 / o_ref.shape[1]) + EPS)
        o_ref[...] = o_ref[...] * inv * g_ref[...]


def _out_proj(mix_a, mix_b, w, x, g):
    n, d = x.shape
    d_a, d_b = mix_a.shape[1], mix_b.shape[1]
    tm = _largest_divisor(n, (512, 256, 128))
    tn = _largest_divisor(d, (1024, 512, 256, 128))
    return pl.pallas_call(
        functools.partial(_out_kernel, d_a=d_a, tn=tn),
        grid=(n // tm, d // tn),
        in_specs=[pl.BlockSpec((tm, d_a), lambda i, j: (i, 0)),
                  pl.BlockSpec((tm, d_b), lambda i, j: (i, 0)),
                  pl.BlockSpec((d_a + d_b, tn), lambda i, j: (0, j)),
                  pl.BlockSpec((tm, tn), lambda i, j: (i, j)), pl.BlockSpec((1, d), lambda i, j: (0, 0))],
        out_specs=pl.BlockSpec((tm, d), lambda i, j: (i, 0)),
        out_shape=jax.ShapeDtypeStruct((n, d), jnp.float32),
        scratch_shapes=[pltpu.VMEM((tm, 1), jnp.float32)],
        compiler_params=_params(("arbitrary", "arbitrary")),
        name="out_proj",
    )(mix_a, mix_b, w, x, g.reshape(1, d))


def _cdiv(a, b):
    return -(-a // b)


def _attention_steps(nq, tq, kb):
    q_l, k_l, n_l = [], [], []
    for q in range(nq):
        nk = _cdiv((q + 1) * tq, kb)
        for k in range(nk):
            q_l.append(q), k_l.append(k), n_l.append(nk)
    return tuple(jnp.asarray(np.asarray(v, np.int32)) for v in (q_l, k_l, n_l))


def _layer(x_prompt, x_sample, cache_k, cache_v, cache_ki, g_norm, w_in, w_s, b_s, g_v, w_out, final_g):
    f32, bf = jnp.float32, jnp.bfloat16
    _, seq, d_model = x_prompt.shape
    dec_b, dec_t, _ = x_sample.shape
    past = cache_k.shape[1]
    d_a = d_model // 2
    d_b = d_model - d_a
    groups = d_a // HEAD_DIM
    kv_w = N_KV * HEAD_DIM
    idx_w = IDX_HEADS * IDX_DIM
    n_s = dec_b * dec_t
    n_all = seq + n_s
    assert seq % 512 == 0 and n_s % MLP_CHUNK == 0 and dec_t <= CHUNK and past % MLP_CHUNK == 0
    assert idx_w % d_a == 0 and d_a % HEAD_DIM == 0

    x_p, x_s = x_prompt[0], x_sample.reshape(n_s, d_model)
    pts = np.cumsum((d_a, d_a, d_a, d_b, kv_w, kv_w, d_b, idx_w, IDX_DIM, IDX_HEADS))[:-1].tolist()
    w_u, w_va, w_za, w_q, w_k, w_v, w_zb, w_qi, w_ki, w_wi = jnp.split(w_in, pts, axis=1)
    w_main = jnp.concatenate([w_qi, w_u, w_va, w_za, w_q, w_zb], axis=1).astype(bf)
    small_w = 2 * kv_w + IDX_DIM + LANES
    w_small = jnp.concatenate([w_k, w_v, w_ki, w_wi, jnp.zeros((d_model, LANES - IDX_HEADS), f32)], axis=1).astype(bf)
    u_col = idx_w // d_a
    q_col, zb_col = u_col + 3, u_col + 4

    half = HEAD_DIM // 2
    inv = ROPE_THETA ** (-2.0 * jnp.arange(half, dtype=f32) / HEAD_DIM)

    def rope_tables(pos):
        ang = pos.astype(f32)[:, None] * inv[None, :]
        return (jnp.concatenate([jnp.cos(ang), jnp.cos(ang)], axis=1),
                jnp.concatenate([-jnp.sin(ang), jnp.sin(ang)], axis=1))

    pos_s = past + jnp.tile(jnp.arange(dec_t), dec_b)
    cos_p, sin_p = rope_tables(jnp.arange(seq))
    cos_s, sin_s = rope_tables(pos_s)

    hm_p, hs_p = _in_proj(x_p, g_norm, w_main, f32), _in_proj(x_p, g_norm, w_small, f32)
    hm_s, hs_s = _in_proj(x_s, g_norm, w_main, f32), _in_proj(x_s, g_norm, w_small, f32)

    pidx = np.arange(MLP_CHUNK)
    mask_p = (pidx[None, :] // CHUNK) <= (pidx[:, None] // CHUNK)
    wm_p = jnp.where(mask_p[None], w_s, 0.0).astype(bf)
    b_p = b_s[:, :, None]
    tidx = pidx % dec_t
    same = (pidx[None, :] // dec_t) == (pidx[:, None] // dec_t)
    mask_s = same & ((tidx[None, :] // CHUNK) <= (tidx[:, None] // CHUNK))
    wm_s = jnp.where(mask_s[None], w_s[:, tidx][:, :, tidx], 0.0).astype(bf)
    b_sm = b_s[:, tidx][:, :, None]
    gv = g_v.reshape(1, d_a)
    (mix_a_p,) = _mlp_group(hm_p, u_col, d_a, 0, seq, 512, wm_p, b_p, gv, False)
    mix_a_s, vn_s = _mlp_group(hm_s, u_col, d_a, 0, n_s, MLP_CHUNK, wm_s, b_sm, gv, True)

    qirt, qrt, kr_p, krb_p, vt_p, kir_p, kirb_p, wist, qn_p, kn_p = _prep(
        hm_p, hs_p, cos_p, sin_p, d_b, q_col, row0=0, nrows=seq, transposed=True)
    qir_s, qr_s, kr_s, krb_s, vb_s, kir_s, kirb_s, wis_s = _prep(
        hm_s, hs_s, cos_s, sin_s, d_b, q_col, row0=0, nrows=n_s, transposed=False)
    v_p, v_s_new = hs_p[:, kv_w:2 * kv_w], hs_s[:, kv_w:2 * kv_w]

    tq_p, kb_i, kb_a = PROMPT_TQ, PROMPT_KB_INDEX, PROMPT_KB_ATTN
    k_sel_p = min(TOPK_MAX, seq // 4)
    st_p, thr_p, taken_p, ties_p = _index_scores_t(qirt, wist, kirb_p, tq=tq_p, kb=kb_i, k_sel=k_sel_p)
    steps_p = _attention_steps(seq // tq_p, tq_p, kb_a)
    mix_b_p = _masked_attention_t(qrt, hm_p, zb_col, krb_p, vt_p, st_p, thr_p, taken_p, ties_p, qn_p, kn_p,
                                  steps_p, tq=tq_p, kb=kb_a)

    kb_s = SAMPLE_KB_INDEX
    n_keys = past + dec_t
    sk_s = _cdiv(n_keys, kb_s) * kb_s
    padk = lambda a: jnp.pad(a, ((0, 0), (0, sk_s - n_keys), (0, 0)))
    k_s = padk(jnp.concatenate([cache_k.reshape(dec_b, past, kv_w).astype(bf), krb_s.reshape(dec_b, dec_t, kv_w)], axis=1))
    v_s = padk(jnp.concatenate([cache_v.reshape(dec_b, past, kv_w).astype(bf), vb_s.reshape(dec_b, dec_t, kv_w)], axis=1))
    ki_s = padk(jnp.concatenate([cache_ki.astype(bf), kirb_s.reshape(dec_b, dec_t, IDX_DIM)], axis=1))
    kit_s = ki_s.transpose(0, 2, 1)
    kt_s = k_s.reshape(dec_b, sk_s, N_KV, HEAD_DIM).transpose(0, 2, 3, 1)
    kpos = jnp.arange(sk_s, dtype=jnp.int32)
    kc_s = jnp.broadcast_to(jnp.where(kpos < n_keys, kpos // CHUNK, FAR_CHUNK)[None, None, :], (dec_b, 1, sk_s))
    qc_s = (pos_s // CHUNK).astype(jnp.int32)[:, None]
    k_sel_s = min(TOPK_MAX, n_keys // 4)
    nkb_s = jnp.full((dec_b,), sk_s // kb_s, jnp.int32)
    s_s, thr_s, taken_s, ties_s = _index_scores(qir_s, wis_s, qc_s, kit_s, kc_s, nkb_s, row0=0, batches=dec_b, nq=1,
                                                tq=dec_t, kb=kb_s, k_sel=k_sel_s)
    mix_b_s = _masked_attention(qr_s, hm_s, zb_col, kt_s, v_s, s_s, thr_s, taken_s, ties_s,
                                zrow0=0, tq=dec_t, d_b=d_b)

    w_out_b = w_out.astype(bf)
    y_p = _out_proj(mix_a_p, mix_b_p, w_out_b, x_p, final_g)
    y_s = _out_proj(mix_a_s, mix_b_s, w_out_b, x_s, final_g)

    shp_p = (1, 1, seq, N_KV, HEAD_DIM)
    shp_s = (1, dec_b, dec_t, N_KV, HEAD_DIM)
    return (y_p[None], y_s.reshape(dec_b, dec_t, d_model),
            kr_p.reshape(shp_p), v_p.reshape(shp_p), kir_p.reshape(1, 1, seq, IDX_DIM),
            kr_s.reshape(shp_s), v_s_new.reshape(shp_s), kir_s.reshape(1, dec_b, dec_t, IDX_DIM),
            vn_s.reshape(1, dec_b, dec_t, groups, HEAD_DIM))


def kernel(x_prompt, x_sample, cache_k, cache_v, cache_idx_k, norm_g, w_in, w_s, b_s, v_norm_g, w_out, final_norm_g):
    assert x_prompt.shape[0] == 1 and norm_g.shape[0] == 1, "one prompt stream and one layer"
    return _layer(x_prompt, x_sample, cache_k[0], cache_v[0], cache_idx_k[0], norm_g[0], w_in[0], w_s[0], b_s[0],
                  v_norm_g[0], w_out[0], final_norm_g)
```

```python
import functools

import numpy as np
import jax
import jax.numpy as jnp
from jax import lax
from jax.experimental import pallas as pl
from jax.experimental.pallas import tpu as pltpu

CHUNK = 64
MLP_CHUNK = 128
HEAD_DIM = 128
N_KV = 4
IDX_HEADS = 32
IDX_DIM = 128
TOPK_MAX = 256
ROPE_THETA = 10000.0
EPS = 1e-6

LANES = 128
VMEM_LIMIT = 56 * 1024 * 1024
NEG_INF = float("-inf")
POS_INF = float("inf")
FAR_CHUNK = 1 << 20
LOG2_E = 1.4426950408889634
CHUNK_SHIFT = 6
PROMPT_TQ, PROMPT_KB_INDEX, PROMPT_KB_ATTN = 256, 256, 1024
ONES_ROWS = 16
VT_ROWS = HEAD_DIM + ONES_ROWS
ATTN_ROW_CHUNK = 128
COUNT_TILES = 1
TAKE_ALL = 1.0e9
RANK_CHUNK = 256
EXP_RANGE = 60.0
NORM_MARGIN = 1.001
SAMPLE_KB_INDEX = 256


def _largest_divisor(n, candidates):
    for c in candidates:
        if n % c == 0:
            return c
    raise ValueError(f"no tile in {candidates} divides {n}")


def _params(sem, flags=None):
    return pltpu.CompilerParams(dimension_semantics=sem, vmem_limit_bytes=VMEM_LIMIT, flags=flags)


def _cast_kernel(src_ref, w_ref, o_ref):
    o_ref[...] = w_ref[...].astype(o_ref.dtype)


def _regroup_cast(w, src_blocks, bw, out_dtype):
    d = w.shape[0]
    src = jnp.asarray(np.asarray(src_blocks, np.int32))
    grid_spec = pltpu.PrefetchScalarGridSpec(
        num_scalar_prefetch=1,
        grid=(len(src_blocks),),
        in_specs=[pl.BlockSpec((d, bw), lambda j, s: (0, s[j]))],
        out_specs=pl.BlockSpec((d, bw), lambda j, s: (0, j)),
    )
    return pl.pallas_call(
        _cast_kernel,
        grid_spec=grid_spec,
        out_shape=jax.ShapeDtypeStruct((d, len(src_blocks) * bw), out_dtype),
        compiler_params=_params(("arbitrary",)),
        name="regroup_cast",
    )(src, w)


def _in_proj_kernel(x_ref, g_ref, w_ref, o_ref, xn_ref):
    @pl.when(pl.program_id(1) == 0)
    def _():
        x = x_ref[...]
        ms = jnp.mean(x * x, axis=-1, keepdims=True)
        xn_ref[...] = (x * lax.rsqrt(ms + EPS) * g_ref[...]).astype(xn_ref.dtype)

    o_ref[...] = jnp.dot(xn_ref[...], w_ref[...], preferred_element_type=jnp.float32).astype(o_ref.dtype)


def _in_proj(x, g, w, out_dtype):
    m, k = x.shape
    n = w.shape[1]
    tm = _largest_divisor(m, (512, 256, 128))
    tn = _largest_divisor(n, (1024, 1280, 512, 256, 128))
    return pl.pallas_call(
        _in_proj_kernel,
        grid=(m // tm, n // tn),
        in_specs=[pl.BlockSpec((tm, k), lambda i, j: (i, 0)), pl.BlockSpec((1, k), lambda i, j: (0, 0)),
                  pl.BlockSpec((k, tn), lambda i, j: (0, j))],
        out_specs=[pl.BlockSpec((tm, tn), lambda i, j: (i, j)), pl.BlockSpec((tm, k), lambda i, j: (i, 0))],
        out_shape=[jax.ShapeDtypeStruct((m, n), out_dtype), jax.ShapeDtypeStruct((m, k), w.dtype)],
        compiler_params=_params(("arbitrary", "arbitrary")),
        name="in_proj",
    )(x, g.reshape(1, k), w)


def _mm_kernel(x_ref, w_ref, o_ref):
    o_ref[...] = jnp.dot(x_ref[...], w_ref[...], preferred_element_type=jnp.float32).astype(o_ref.dtype)


def _matmul(x, w, out_dtype):
    m, k = x.shape
    n = w.shape[1]
    tm = _largest_divisor(m, (1024, 512, 256, 128))
    tn = _largest_divisor(n, (1024, 1280, 512, 256, 128))
    return pl.pallas_call(
        _mm_kernel,
        grid=(m // tm, n // tn),
        in_specs=[pl.BlockSpec((tm, k), lambda i, j: (i, 0)), pl.BlockSpec((k, tn), lambda i, j: (0, j))],
        out_specs=pl.BlockSpec((tm, tn), lambda i, j: (i, j)),
        out_shape=jax.ShapeDtypeStruct((m, n), out_dtype),
        compiler_params=_params(("arbitrary", "arbitrary")),
        name="in_proj_small",
    )(x, w)


def _silu(z):
    return z * (1.0 / (1.0 + jnp.exp(-z)))


def _mlp_kernel(u_ref, va_ref, za_ref, wm_ref, b_ref, gv_ref, mix_ref, *vn_refs, rows, groups):
    for c in range(rows // MLP_CHUNK):
        rs = slice(c * MLP_CHUNK, (c + 1) * MLP_CHUNK)
        for g in range(groups):
            cs = slice(g * HEAD_DIM, (g + 1) * HEAD_DIM)
            va = va_ref[rs, cs]
            mu = jnp.mean(va, axis=-1, keepdims=True)
            cen = va - mu
            var = jnp.mean(cen * cen, axis=-1, keepdims=True)
            vn = cen * lax.rsqrt(var + EPS) * gv_ref[:, cs]
            if vn_refs:
                vn_refs[0][rs, cs] = vn
            mixed = jnp.dot(wm_ref[g], vn.astype(jnp.bfloat16), preferred_element_type=jnp.float32) + b_ref[g]
            a = u_ref[rs, cs] * mixed
            mix_ref[rs, cs] = (a * _silu(za_ref[rs, cs])).astype(mix_ref.dtype)


def _mlp_group(h_main, col0, d_a, row0, nrows, rows, wm, bias, gv, emit_vn):
    groups = d_a // HEAD_DIM
    rb0 = row0 // rows
    out_shape = [jax.ShapeDtypeStruct((nrows, d_a), jnp.bfloat16)]
    out_specs = [pl.BlockSpec((rows, d_a), lambda i: (i, 0))]
    if emit_vn:
        out_shape.append(jax.ShapeDtypeStruct((nrows, d_a), jnp.float32))
        out_specs.append(pl.BlockSpec((rows, d_a), lambda i: (i, 0)))
    col = lambda c: pl.BlockSpec((rows, d_a), lambda i: (rb0 + i, c))
    return pl.pallas_call(
        functools.partial(_mlp_kernel, rows=rows, groups=groups),
        grid=(nrows // rows,),
        in_specs=[col(col0), col(col0 + 1), col(col0 + 2),
                  pl.BlockSpec((groups, MLP_CHUNK, MLP_CHUNK), lambda i: (0, 0, 0)),
                  pl.BlockSpec((groups, MLP_CHUNK, 1), lambda i: (0, 0, 0)),
                  pl.BlockSpec((1, d_a), lambda i: (0, 0))],
        out_specs=out_specs,
        out_shape=out_shape,
        compiler_params=_params(("arbitrary",)),
        name="chunk_mlp",
    )(h_main, h_main, h_main, wm, bias, gv)


def _rope(x, cos, sin):
    return x * cos + pltpu.roll(x, HEAD_DIM // 2, 1) * sin


def _prep_kernel(qi_ref, q_ref, k_ref, v_ref, ki_ref, wi_ref, cos_ref, sin_ref,
                 qir_ref, qr_ref, kr_ref, krb_ref, vb_ref, kir_ref, kirb_ref, wis_ref, *norm_refs,
                 n_heads, transposed):
    cos = cos_ref[...]
    sin = sin_ref[...]

    def put(ref, h, width, val):
        if transposed:
            ref[h * width:(h + 1) * width, :] = val.T.astype(ref.dtype)
        else:
            ref[:, h * width:(h + 1) * width] = val.astype(ref.dtype)

    for h in range(IDX_HEADS):
        put(qir_ref, h, IDX_DIM, _rope(qi_ref[:, h * IDX_DIM:(h + 1) * IDX_DIM], cos, sin))
    q_scale = HEAD_DIM ** -0.5 * LOG2_E
    for h in range(n_heads):
        qs = _rope(q_ref[:, h * HEAD_DIM:(h + 1) * HEAD_DIM], cos, sin) * q_scale
        put(qr_ref, h, HEAD_DIM, qs)
        if transposed:
            qb = qs.T.astype(qr_ref.dtype).astype(jnp.float32)
            norm_refs[0][h:h + 1, :] = jnp.sqrt(jnp.sum(qb * qb, axis=0, keepdims=True)) * NORM_MARGIN
    lane = lax.broadcasted_iota(jnp.int32, (8, LANES), 1)
    kn = jnp.zeros((8, LANES), jnp.float32)
    for h in range(N_KV):
        cs = slice(h * HEAD_DIM, (h + 1) * HEAD_DIM)
        kr = _rope(k_ref[:, cs], cos, sin)
        kr_ref[:, cs] = kr
        krb_ref[:, cs] = kr.astype(krb_ref.dtype)
        if transposed:
            vb_ref[h * VT_ROWS:h * VT_ROWS + HEAD_DIM, :] = v_ref[:, cs].T.astype(vb_ref.dtype)
            vb_ref[h * VT_ROWS + HEAD_DIM:(h + 1) * VT_ROWS, :] = jnp.ones((ONES_ROWS, v_ref.shape[0]), vb_ref.dtype)
            kb16 = kr.astype(krb_ref.dtype).astype(jnp.float32)
            kmax = jnp.sqrt(jnp.max(jnp.sum(kb16 * kb16, axis=1, keepdims=True), axis=0, keepdims=True))
            kn = jnp.where(lane == h, kmax, kn)
        else:
            vb_ref[:, cs] = v_ref[:, cs].astype(vb_ref.dtype)
    if transposed:
        norm_refs[1][0] = kn
    kir = _rope(ki_ref[...], cos, sin)
    kir_ref[...] = kir
    kirb_ref[...] = kir.astype(kirb_ref.dtype)
    put(wis_ref, 0, LANES, wi_ref[...] * ((IDX_HEADS * IDX_DIM) ** -0.5))


def _prep(h_main, h_small, cos_t, sin_t, d_b, q_col, *, row0, nrows, transposed):
    rows = _largest_divisor(nrows, (256, 128))
    rb0 = row0 // rows
    n_heads = d_b // HEAD_DIM
    idx_w = IDX_HEADS * IDX_DIM
    kv_w = N_KV * HEAD_DIM
    inspec = lambda w, c: pl.BlockSpec((rows, w), lambda i: (rb0 + i, c))
    bf, f32 = jnp.bfloat16, jnp.float32
    outs = [(idx_w, bf, True), (d_b, bf, True), (kv_w, f32, False), (kv_w, bf, False), (kv_w, bf, True),
            (IDX_DIM, f32, False), (IDX_DIM, bf, False), (LANES, f32, True)]
    out_specs, out_shape = [], []
    for idx, (w, dt, feature_major) in enumerate(outs):
        if transposed and feature_major:
            w = N_KV * VT_ROWS if idx == 4 else w
            out_specs.append(pl.BlockSpec((w, rows), lambda i: (0, i)))
            out_shape.append(jax.ShapeDtypeStruct((w, nrows), dt))
        else:
            out_specs.append(pl.BlockSpec((rows, w), lambda i: (i, 0)))
            out_shape.append(jax.ShapeDtypeStruct((nrows, w), dt))
    if transposed:
        out_specs += [pl.BlockSpec((n_heads, rows), lambda i: (0, i)), pl.BlockSpec((1, 8, LANES), lambda i: (i, 0, 0))]
        out_shape += [jax.ShapeDtypeStruct((n_heads, nrows), f32), jax.ShapeDtypeStruct((nrows // rows, 8, LANES), f32)]
    return pl.pallas_call(
        functools.partial(_prep_kernel, n_heads=n_heads, transposed=transposed),
        grid=(nrows // rows,),
        in_specs=[inspec(idx_w, 0), inspec(d_b, q_col),
                  inspec(kv_w, 0), inspec(kv_w, 1), inspec(IDX_DIM, 2 * kv_w // IDX_DIM),
                  inspec(LANES, 2 * kv_w // LANES + 1), inspec(HEAD_DIM, 0), inspec(HEAD_DIM, 0)],
        out_specs=out_specs,
        out_shape=out_shape,
        compiler_params=_params(("arbitrary",)),
        name="rope_prep",
    )(h_main, h_main, h_small, h_small, h_small, h_small, cos_t, sin_t)


def _kth_threshold(count, mn, mx, nf, k_sel):
    kk = jnp.minimum(nf, float(k_sel))
    done0 = jnp.where(nf <= float(k_sel), 1.0, 0.0)

    def cond(st):
        return st[5] > 0.5

    def body(st):
        lo, hi, thr, done, tie, _ = st
        mid = 0.5 * lo + 0.5 * hi
        cnt = count(mid, False)
        active = done < 0.5
        hit = jnp.logical_and(active, cnt == kk)
        no_room = jnp.logical_or(mid <= lo, mid >= hi)
        stuck = jnp.logical_and(active, jnp.logical_and(no_room, cnt != kk))
        thr = jnp.where(hit, mid, thr)
        tie = jnp.where(stuck, 1.0, tie)
        lo = jnp.where(jnp.logical_and(active, cnt > kk), mid, lo)
        hi = jnp.where(jnp.logical_and(active, cnt < kk), mid, hi)
        done = jnp.where(jnp.logical_or(hit, stuck), 1.0, done)
        return lo, hi, thr, done, tie, jnp.sum(1.0 - done)

    init = (mn, mx, mn, done0, jnp.zeros_like(mn), jnp.sum(1.0 - done0))
    lo, hi, thr, _, tie, _ = lax.while_loop(cond, body, init)
    any_tie = jnp.sum(tie) > 0.5

    def resolve(_):
        is_tie = tie > 0.5
        kth = jnp.where(count(hi, False) >= kk, hi, lo)
        thr_t = jnp.where(is_tie, kth, thr)
        taken = jnp.where(is_tie, kk - count(thr_t, True), TAKE_ALL)
        return thr_t, taken

    thr, taken = lax.cond(any_tie, resolve, lambda _: (thr, jnp.full_like(thr, TAKE_ALL)), 0)
    return thr, taken, any_tie


def _index_kernel(nkb_ref, qi_ref, wi_ref, qc_ref, kit_ref, kc_ref, s_ref, thr_ref, taken_ref, tie_ref, wb_ref,
                  *, tq, kb, sk, k_sel):
    b = pl.program_id(0)
    i = pl.program_id(1)
    nkb = nkb_ref[b * pl.num_programs(1) + i]
    lane_tiles = kb // LANES
    rt = min(tq, 128)

    for h in range(IDX_HEADS):
        wb_ref[h] = jnp.broadcast_to(wi_ref[:, h:h + 1], (tq, LANES))

    def score_tile(j, carry):
        off = pl.multiple_of(j * kb, kb)
        kt = kit_ref[:, pl.ds(off, kb)]
        kc = kc_ref[:, pl.ds(off, kb)]
        for r0 in range(0, tq, rt):
            accs = [jnp.zeros((rt, LANES), jnp.float32) for _ in range(lane_tiles)]
            for h in range(IDX_HEADS):
                r = jnp.dot(qi_ref[r0:r0 + rt, h * IDX_DIM:(h + 1) * IDX_DIM], kt,
                            preferred_element_type=jnp.float32)
                w = wb_ref[h, r0:r0 + rt, :]
                for l in range(lane_tiles):
                    accs[l] = accs[l] + jnp.maximum(r[:, l * LANES:(l + 1) * LANES], 0.0) * w
            acc = jnp.concatenate(accs, axis=1)
            adm = kc <= qc_ref[r0:r0 + rt, :]
            s_ref[r0:r0 + rt, pl.ds(off, kb)] = jnp.where(adm, acc, NEG_INF)
        return carry

    lax.fori_loop(0, nkb, score_tile, 0)

    def fill_tile(j, carry):
        off = pl.multiple_of(j * kb, kb)
        s_ref[:, pl.ds(off, kb)] = jnp.full((tq, kb), NEG_INF, jnp.float32)
        return carry

    lax.fori_loop(nkb, sk // kb, fill_tile, 0)

    def stats_tile(j, carry):
        mx, mn, nf = carry
        off = pl.multiple_of(j * kb, kb)
        for l in range(lane_tiles):
            s = s_ref[:, pl.ds(off + l * LANES, LANES)]
            fin = s > NEG_INF
            mx = jnp.maximum(mx, s)
            mn = jnp.minimum(mn, jnp.where(fin, s, POS_INF))
            nf = nf + jnp.where(fin, 1.0, 0.0)
        return mx, mn, nf

    mx, mn, nf = lax.fori_loop(
        0, nkb, stats_tile,
        (jnp.full((tq, LANES), NEG_INF, jnp.float32), jnp.full((tq, LANES), POS_INF, jnp.float32),
         jnp.zeros((tq, LANES), jnp.float32)))
    mx = jnp.max(mx, axis=1, keepdims=True)
    mn = jnp.min(mn, axis=1, keepdims=True)
    nf = jnp.sum(nf, axis=1, keepdims=True)

    def count(v, strict):
        def tile(j, cnt):
            off = pl.multiple_of(j * kb, kb)
            for l in range(lane_tiles):
                s = s_ref[:, pl.ds(off + l * LANES, LANES)]
                cnt = cnt + jnp.where(s > v if strict else s >= v, 1.0, 0.0)
            return cnt
        cnt = lax.fori_loop(0, nkb, tile, jnp.zeros((tq, LANES), jnp.float32))
        return jnp.sum(cnt, axis=1, keepdims=True)

    thr, taken, any_tie = _kth_threshold(count, mn, mx, nf, k_sel)
    thr_ref[...] = jnp.broadcast_to(thr, (tq, LANES))
    taken_ref[...] = jnp.broadcast_to(taken, (tq, LANES))
    tie_ref[b * pl.num_programs(1) + i] = jnp.where(any_tie, 1, 0).astype(jnp.int32)


def _index_scores(qir, wis, qc, kit, kc, nkb, *, row0, batches, nq, tq, kb, k_sel):
    sk = kit.shape[-1]
    rb0 = row0 // tq
    grid_spec = pltpu.PrefetchScalarGridSpec(
        num_scalar_prefetch=1,
        grid=(batches, nq),
        in_specs=[pl.BlockSpec((tq, IDX_HEADS * IDX_DIM), lambda b, i, n: (rb0 + b * nq + i, 0)),
                  pl.BlockSpec((tq, LANES), lambda b, i, n: (rb0 + b * nq + i, 0)),
                  pl.BlockSpec((tq, 1), lambda b, i, n: (rb0 + b * nq + i, 0)),
                  pl.BlockSpec((None, IDX_DIM, sk), lambda b, i, n: (b, 0, 0)),
                  pl.BlockSpec((None, 1, sk), lambda b, i, n: (b, 0, 0))],
        out_specs=[pl.BlockSpec((None, tq, sk), lambda b, i, n: (b, i, 0)),
                   pl.BlockSpec((None, tq, LANES), lambda b, i, n: (b, i, 0)),
                   pl.BlockSpec((None, tq, LANES), lambda b, i, n: (b, i, 0)),
                   pl.BlockSpec(memory_space=pltpu.SMEM)],
        scratch_shapes=[pltpu.VMEM((IDX_HEADS, tq, LANES), jnp.float32)],
    )
    return pl.pallas_call(
        functools.partial(_index_kernel, tq=tq, kb=kb, sk=sk, k_sel=k_sel),
        grid_spec=grid_spec,
        out_shape=[jax.ShapeDtypeStruct((batches, nq * tq, sk), jnp.float32),
                   jax.ShapeDtypeStruct((batches, nq * tq, LANES), jnp.float32),
                   jax.ShapeDtypeStruct((batches, nq * tq, LANES), jnp.float32),
                   jax.ShapeDtypeStruct((batches * nq,), jnp.int32)],
        compiler_params=_params(("arbitrary", "arbitrary")),
        name="index_select",
    )(nkb, qir, wis, qc, kit, kc)


def _fold8(x, op):
    parts = [x[g * 8:(g + 1) * 8, :] for g in range(x.shape[0] // 8)]
    while len(parts) > 1:
        parts = [op(parts[a], parts[a + 1]) for a in range(0, len(parts) - 1, 2)] + (parts[-1:] if len(parts) % 2 else [])
    return parts[0]


def _index_kernel_t(qit_ref, wit_ref, ki_ref, st_ref, thr_ref, taken_ref, tie_ref, *, tq, kb, k_sel):
    i = pl.program_id(0)
    nkb = lax.div((i + 1) * tq + (kb - 1), kb)
    total = st_ref.shape[0] // kb
    qchunk = lax.shift_right_logical(i * tq + lax.broadcasted_iota(jnp.int32, (kb, tq), 1), CHUNK_SHIFT)
    krow = lax.broadcasted_iota(jnp.int32, (kb, tq), 0)

    def score_tile(j, carry):
        mx, mn, nf = carry
        off = pl.multiple_of(j * kb, kb)
        k_t = ki_ref[pl.ds(off, kb), :]
        acc = jnp.zeros((kb, tq), jnp.float32)
        for h in range(IDX_HEADS):
            r = jnp.dot(k_t, qit_ref[h * IDX_DIM:(h + 1) * IDX_DIM, :], preferred_element_type=jnp.float32)
            acc = acc + jnp.maximum(r, 0.0) * wit_ref[h:h + 1, :]
        adm = lax.shift_right_logical(off + krow, CHUNK_SHIFT) <= qchunk
        s = jnp.where(adm, acc, NEG_INF)
        st_ref[pl.ds(off, kb), :] = s
        fin = s > NEG_INF
        mx = jnp.maximum(mx, _fold8(s, jnp.maximum))
        mn = jnp.minimum(mn, _fold8(jnp.where(fin, s, POS_INF), jnp.minimum))
        nf = nf + _fold8(jnp.where(fin, 1.0, 0.0), jnp.add)
        return mx, mn, nf

    mx, mn, nf = lax.fori_loop(
        0, nkb, score_tile,
        (jnp.full((8, tq), NEG_INF, jnp.float32), jnp.full((8, tq), POS_INF, jnp.float32),
         jnp.zeros((8, tq), jnp.float32)))

    def fill_tile(j, carry):
        off = pl.multiple_of(j * kb, kb)
        st_ref[pl.ds(off, kb), :] = jnp.full((kb, tq), NEG_INF, jnp.float32)
        return carry

    lax.fori_loop(nkb, total, fill_tile, 0)
    mx = jnp.max(mx, axis=0, keepdims=True)
    mn = jnp.min(mn, axis=0, keepdims=True)
    nf = jnp.sum(nf, axis=0, keepdims=True)

    cb = COUNT_TILES * kb

    def count(v, strict):
        def tile(j, cnt):
            off = pl.multiple_of(j * cb, cb)
            s = st_ref[pl.ds(off, cb), :]
            return cnt + _fold8(jnp.where(s > v if strict else s >= v, 1.0, 0.0), jnp.add)
        trips = lax.div(nkb + (COUNT_TILES - 1), COUNT_TILES)
        cnt = lax.fori_loop(0, trips, tile, jnp.zeros((8, tq), jnp.float32))
        return jnp.sum(cnt, axis=0, keepdims=True)

    thr, taken, any_tie = _kth_threshold(count, mn, mx, nf, k_sel)
    thr_ref[...] = jnp.broadcast_to(thr, (8, tq))
    taken_ref[...] = jnp.broadcast_to(taken, (8, tq))
    tie_ref[i] = jnp.where(any_tie, 1, 0).astype(jnp.int32)


def _index_scores_t(qirt, wist, kirb, *, tq, kb, k_sel):
    idx_w, seq = qirt.shape
    nq = seq // tq
    assert (seq // kb) % COUNT_TILES == 0
    return pl.pallas_call(
        functools.partial(_index_kernel_t, tq=tq, kb=kb, k_sel=k_sel),
        grid=(nq,),
        in_specs=[pl.BlockSpec((idx_w, tq), lambda i: (0, i)),
                  pl.BlockSpec((LANES, tq), lambda i: (0, i)),
                  pl.BlockSpec((seq, IDX_DIM), lambda i: (0, 0))],
        out_specs=[pl.BlockSpec((seq, tq), lambda i: (0, i)),
                   pl.BlockSpec((8, tq), lambda i: (0, i)),
                   pl.BlockSpec((8, tq), lambda i: (0, i)),
                   pl.BlockSpec(memory_space=pltpu.SMEM)],
        out_shape=[jax.ShapeDtypeStruct((seq, seq), jnp.float32),
                   jax.ShapeDtypeStruct((8, seq), jnp.float32),
                   jax.ShapeDtypeStruct((8, seq), jnp.float32),
                   jax.ShapeDtypeStruct((nq,), jnp.int32)],
        compiler_params=_params(("arbitrary",)),
        name="index_select_t",
    )(qirt, wist, kirb)


def _tie_rank_matrix(n, lower):
    rows = lax.broadcasted_iota(jnp.int32, (n, n), 0)
    cols = lax.broadcasted_iota(jnp.int32, (n, n), 1)
    return jnp.where(cols < rows if lower else rows < cols, 1.0, 0.0).astype(jnp.bfloat16)


def _attn_kernel_t(qi_ref, kj_ref, nk_ref, tie_ref, qt_ref, zb_ref, k_ref, vt_ref, st_ref, thr_ref, taken_ref,
                   qn_ref, kn_ref, o_ref, m_ref, c_ref, acc_ref, s_ref, p_ref, bias_ref, seen_ref, *, n_heads):
    p = pl.program_id(0)
    kj = kj_ref[p]
    hpg = n_heads // N_KV

    @pl.when(kj == 0)
    def _():
        m_ref[...] = jnp.full(m_ref.shape, NEG_INF, jnp.float32)
        c_ref[...] = jnp.zeros(c_ref.shape, jnp.float32)
        acc_ref[...] = jnp.zeros(acc_ref.shape, jnp.float32)
        seen_ref[...] = jnp.zeros(seen_ref.shape, jnp.float32)

    kb = st_ref.shape[0]
    rc = ATTN_ROW_CHUNK
    thr = thr_ref[0:1, :]
    for r in range(0, kb, rc):
        bias_ref[r:r + rc, :] = jnp.where(st_ref[r:r + rc, :] >= thr, 0.0, NEG_INF)

    @pl.when(tie_ref[qi_ref[p]] != 0)
    def _():
        taken = taken_ref[0:1, :]
        ltri = _tie_rank_matrix(RANK_CHUNK, True)
        seen = seen_ref[...]
        for r in range(0, kb, RANK_CHUNK):
            s = st_ref[r:r + RANK_CHUNK, :]
            eq = s == thr
            rank = jnp.dot(ltri, jnp.where(eq, 1.0, 0.0).astype(jnp.bfloat16),
                           preferred_element_type=jnp.float32) + seen
            take = jnp.logical_or(s > thr, jnp.logical_and(eq, rank < taken))
            bias_ref[r:r + RANK_CHUNK, :] = jnp.where(take, 0.0, NEG_INF)
            seen = seen + jnp.sum(jnp.where(eq, 1.0, 0.0), axis=0, keepdims=True)
        seen_ref[...] = seen

    def logits(h, r):
        g = h // hpg
        return jnp.dot(k_ref[r:r + rc, g * HEAD_DIM:(g + 1) * HEAD_DIM], qt_ref[h * HEAD_DIM:(h + 1) * HEAD_DIM, :],
                       preferred_element_type=jnp.float32) + bias_ref[r:r + rc, :]

    def weighted_values(h):
        g = h // hpg
        return jnp.dot(vt_ref[g * VT_ROWS:(g + 1) * VT_ROWS, :], p_ref[h], preferred_element_type=jnp.float32)

    first = kj == 0
    shift0 = [jnp.where(m_ref[h] == NEG_INF, 0.0, m_ref[h]) for h in range(n_heads)]
    kn = kn_ref[0]
    for blk in range(1, kn_ref.shape[0]):
        kn = jnp.maximum(kn, kn_ref[blk])
    excess = None
    for h in range(n_heads):
        bound = qn_ref[h:h + 1, :] * kn[0:1, h // hpg:h // hpg + 1]
        e = bound - shift0[h]
        excess = e if excess is None else jnp.maximum(excess, e)
    in_range = jnp.max(excess) <= EXP_RANGE

    @pl.when(in_range)
    def _():
        for h in range(n_heads):
            cm = None
            for r in range(0, kb, rc):
                t = logits(h, r) - shift0[h]
                p_ref[h, r:r + rc, :] = jnp.exp2(t).astype(p_ref.dtype)
                c8 = _fold8(t, jnp.maximum)
                cm = c8 if cm is None else jnp.maximum(cm, c8)
            m_ref[h] = jnp.maximum(m_ref[h], shift0[h] + jnp.max(cm, axis=0, keepdims=True))
        for h in range(n_heads):
            alpha = jnp.where(first, 1.0, jnp.exp2(c_ref[h] - shift0[h]))
            acc_ref[h] = alpha * acc_ref[h] + weighted_values(h)
            c_ref[h] = shift0[h]

    @pl.when(jnp.logical_not(in_range))
    def _():
        cmax = []
        for h in range(n_heads):
            cm = None
            for r in range(0, kb, rc):
                s = logits(h, r)
                s_ref[h, r:r + rc, :] = s
                c8 = _fold8(s, jnp.maximum)
                cm = c8 if cm is None else jnp.maximum(cm, c8)
            cmax.append(jnp.max(cm, axis=0, keepdims=True))
        for h in range(n_heads):
            m_cur = jnp.maximum(m_ref[h], cmax[h])
            shift = jnp.where(m_cur == NEG_INF, 0.0, m_cur)
            for r in range(0, kb, rc):
                p_ref[h, r:r + rc, :] = jnp.exp2(s_ref[h, r:r + rc, :] - shift).astype(p_ref.dtype)
            alpha = jnp.where(first, 1.0, jnp.exp2(c_ref[h] - shift))
            acc_ref[h] = alpha * acc_ref[h] + weighted_values(h)
            c_ref[h] = shift
            m_ref[h] = m_cur

    @pl.when(kj == nk_ref[p] - 1)
    def _():
        for h in range(n_heads):
            cs = slice(h * HEAD_DIM, (h + 1) * HEAD_DIM)
            o = (acc_ref[h, 0:HEAD_DIM, :] / acc_ref[h, HEAD_DIM:HEAD_DIM + 1, :]).T
            o_ref[:, cs] = (o * _silu(zb_ref[:, cs])).astype(o_ref.dtype)


def _masked_attention_t(qrt, h_main, zb_col, krb, vt, st, thr, taken, ties, qn, kn, steps, *, tq, kb):
    qidx, kidx, nk = steps
    d_b, seq = qrt.shape
    n_heads = d_b // HEAD_DIM
    kv_w = N_KV * HEAD_DIM
    kn_blocks = kb * kn.shape[0] // seq
    assert kb % RANK_CHUNK == 0 and kb % ATTN_ROW_CHUNK == 0 and kn_blocks >= 1
    grid_spec = pltpu.PrefetchScalarGridSpec(
        num_scalar_prefetch=4,
        grid=(qidx.shape[0],),
        in_specs=[pl.BlockSpec((d_b, tq), lambda p, q, k, n, t: (0, q[p])),
                  pl.BlockSpec((tq, d_b), lambda p, q, k, n, t: (q[p], zb_col)),
                  pl.BlockSpec((kb, kv_w), lambda p, q, k, n, t: (k[p], 0)),
                  pl.BlockSpec((N_KV * VT_ROWS, kb), lambda p, q, k, n, t: (0, k[p])),
                  pl.BlockSpec((kb, tq), lambda p, q, k, n, t: (k[p], q[p])),
                  pl.BlockSpec((8, tq), lambda p, q, k, n, t: (0, q[p])),
                  pl.BlockSpec((8, tq), lambda p, q, k, n, t: (0, q[p])),
                  pl.BlockSpec((n_heads, tq), lambda p, q, k, n, t: (0, q[p])),
                  pl.BlockSpec((kn_blocks, 8, LANES), lambda p, q, k, n, t: (k[p], 0, 0))],
        out_specs=pl.BlockSpec((tq, d_b), lambda p, q, k, n, t: (q[p], 0)),
        scratch_shapes=[pltpu.VMEM((n_heads, 1, tq), jnp.float32),
                        pltpu.VMEM((n_heads, 1, tq), jnp.float32),
                        pltpu.VMEM((n_heads, VT_ROWS, tq), jnp.float32),
                        pltpu.VMEM((n_heads, kb, tq), jnp.float32),
                        pltpu.VMEM((n_heads, kb, tq), jnp.bfloat16),
                        pltpu.VMEM((kb, tq), jnp.float32),
                        pltpu.VMEM((1, tq), jnp.float32)],
    )
    return pl.pallas_call(
        functools.partial(_attn_kernel_t, n_heads=n_heads),
        grid_spec=grid_spec,
        out_shape=jax.ShapeDtypeStruct((seq, d_b), jnp.bfloat16),
        compiler_params=_params(("arbitrary",)),
        name="masked_attention_t",
    )(qidx, kidx, nk, ties, qrt, h_main, krb, vt, st, thr, taken, qn, kn)


def _attn_kernel(tie_ref, q_ref, zb_ref, kt_ref, v_ref, s_ref, thr_ref, taken_ref, o_ref, bias_ref, *, n_heads):
    b = pl.program_id(0)
    hpg = n_heads // N_KV
    sk = s_ref.shape[1]
    thr = thr_ref[:, 0:1]
    bias_ref[...] = jnp.where(s_ref[...] >= thr, 0.0, NEG_INF)

    @pl.when(tie_ref[b] != 0)
    def _():
        taken = taken_ref[:, 0:1]
        utri = _tie_rank_matrix(RANK_CHUNK, False)
        seen = jnp.zeros_like(thr)
        for c in range(0, sk, RANK_CHUNK):
            s = s_ref[:, c:c + RANK_CHUNK]
            eq = s == thr
            rank = jnp.dot(jnp.where(eq, 1.0, 0.0).astype(jnp.bfloat16), utri,
                           preferred_element_type=jnp.float32) + seen
            take = jnp.logical_or(s > thr, jnp.logical_and(eq, rank < taken))
            bias_ref[:, c:c + RANK_CHUNK] = jnp.where(take, 0.0, NEG_INF)
            seen = seen + jnp.sum(jnp.where(eq, 1.0, 0.0), axis=1, keepdims=True)

    for h in range(n_heads):
        g = h // hpg
        cs = slice(h * HEAD_DIM, (h + 1) * HEAD_DIM)
        s = jnp.dot(q_ref[:, cs], kt_ref[g], preferred_element_type=jnp.float32) + bias_ref[...]
        pr = jnp.exp2(s - jnp.max(s, axis=1, keepdims=True))
        o = jnp.dot(pr.astype(v_ref.dtype), v_ref[:, g * HEAD_DIM:(g + 1) * HEAD_DIM],
                    preferred_element_type=jnp.float32) / jnp.sum(pr, axis=1, keepdims=True)
        o_ref[:, cs] = (o * _silu(zb_ref[:, cs])).astype(o_ref.dtype)


def _masked_attention(qr, h_main, zb_col, kt, vb, scores, thr, taken, ties, *, zrow0, tq, d_b):
    batches, _, _, sk = kt.shape
    zrb0 = zrow0 // tq
    n_heads = d_b // HEAD_DIM
    kv_w = N_KV * HEAD_DIM
    assert sk % RANK_CHUNK == 0
    grid_spec = pltpu.PrefetchScalarGridSpec(
        num_scalar_prefetch=1,
        grid=(batches,),
        in_specs=[pl.BlockSpec((tq, d_b), lambda b, t: (b, 0)),
                  pl.BlockSpec((tq, d_b), lambda b, t: (zrb0 + b, zb_col)),
                  pl.BlockSpec((None, N_KV, HEAD_DIM, sk), lambda b, t: (b, 0, 0, 0)),
                  pl.BlockSpec((None, sk, kv_w), lambda b, t: (b, 0, 0)),
                  pl.BlockSpec((None, tq, sk), lambda b, t: (b, 0, 0)),
                  pl.BlockSpec((None, tq, LANES), lambda b, t: (b, 0, 0)),
                  pl.BlockSpec((None, tq, LANES), lambda b, t: (b, 0, 0))],
        out_specs=pl.BlockSpec((tq, d_b), lambda b, t: (b, 0)),
        scratch_shapes=[pltpu.VMEM((tq, sk), jnp.float32)],
    )
    return pl.pallas_call(
        functools.partial(_attn_kernel, n_heads=n_heads),
        grid_spec=grid_spec,
        out_shape=jax.ShapeDtypeStruct((batches * tq, d_b), jnp.bfloat16),
        compiler_params=_params(("arbitrary",)),
        name="masked_attention",
    )(ties, qr, h_main, kt, vb, scores, thr, taken)


def _out_kernel(mixa_ref, mixb_ref, w_ref, x_ref, g_ref, o_ref, ssq_ref, *, d_a, tn):
    j = pl.program_id(1)

    @pl.when(j == 0)
    def _():
        ssq_ref[...] = jnp.zeros(ssq_ref.shape, jnp.float32)

    y = (x_ref[...] + jnp.dot(mixa_ref[...], w_ref[0:d_a, :], preferred_element_type=jnp.float32)
         + jnp.dot(mixb_ref[...], w_ref[d_a:, :], preferred_element_type=jnp.float32))
    o_ref[:, pl.ds(pl.multiple_of(j * tn, tn), tn)] = y
    ssq_ref[...] += jnp.sum(y * y, axis=-1, keepdims=True)

    @pl.when(j == pl.num_programs(1) - 1)
    def _():
        inv = lax.rsqrt(ssq_ref[...] * (1.0 / o_ref.shape[1]) + EPS)
        o_ref[...] = o_ref[...] * inv * g_ref[...]


def _out_proj(mix_a, mix_b, w, x, g):
    n, d = x.shape
    d_a, d_b = mix_a.shape[1], mix_b.shape[1]
    tm = _largest_divisor(n, (512, 256, 128))
    tn = _largest_divisor(d, (1024, 512, 256, 128))
    return pl.pallas_call(
        functools.partial(_out_kernel, d_a=d_a, tn=tn),
        grid=(n // tm, d // tn),
        in_specs=[pl.BlockSpec((tm, d_a), lambda i, j: (i, 0)),
                  pl.BlockSpec((tm, d_b), lambda i, j: (i, 0)),
                  pl.BlockSpec((d_a + d_b, tn), lambda i, j: (0, j)),
                  pl.BlockSpec((tm, tn), lambda i, j: (i, j)), pl.BlockSpec((1, d), lambda i, j: (0, 0))],
        out_specs=pl.BlockSpec((tm, d), lambda i, j: (i, 0)),
        out_shape=jax.ShapeDtypeStruct((n, d), jnp.float32),
        scratch_shapes=[pltpu.VMEM((tm, 1), jnp.float32)],
        compiler_params=_params(("arbitrary", "arbitrary")),
        name="out_proj",
    )(mix_a, mix_b, w, x, g.reshape(1, d))


def _cdiv(a, b):
    return -(-a // b)


def _attention_steps(nq, tq, kb):
    q_l, k_l, n_l = [], [], []
    for q in range(nq):
        nk = _cdiv((q + 1) * tq, kb)
        for k in range(nk):
            q_l.append(q), k_l.append(k), n_l.append(nk)
    return tuple(jnp.asarray(np.asarray(v, np.int32)) for v in (q_l, k_l, n_l))


def _layer(x_prompt, x_sample, cache_k, cache_v, cache_ki, g_norm, w_in, w_s, b_s, g_v, w_out, final_g):
    f32, bf = jnp.float32, jnp.bfloat16
    _, seq, d_model = x_prompt.shape
    dec_b, dec_t, _ = x_sample.shape
    past = cache_k.shape[1]
    d_a = d_model // 2
    d_b = d_model - d_a
    groups = d_a // HEAD_DIM
    kv_w = N_KV * HEAD_DIM
    idx_w = IDX_HEADS * IDX_DIM
    n_s = dec_b * dec_t
    n_all = seq + n_s
    assert seq % 512 == 0 and n_s % MLP_CHUNK == 0 and dec_t <= CHUNK and past % MLP_CHUNK == 0
    assert idx_w % d_a == 0 and d_a % HEAD_DIM == 0

    x_p, x_s = x_prompt[0], x_sample.reshape(n_s, d_model)
    widths = (d_a, d_a, d_a, d_b, kv_w, kv_w, d_b, idx_w, IDX_DIM, IDX_HEADS)
    starts = np.concatenate([[0], np.cumsum(widths)]).tolist()
    (o_u, o_va, o_za, o_q, o_k, o_v, o_zb, o_qi, o_ki, o_wi) = starts[:-1]
    bw = int(np.gcd.reduce([d_a, d_b, kv_w, idx_w, o_q, o_zb, o_qi]))
    main_src = [(off + c) // bw for off, wd in ((o_qi, idx_w), (o_u, d_a), (o_va, d_a), (o_za, d_a), (o_q, d_b),
                                                (o_zb, d_b)) for c in range(0, wd, bw)]
    w_main = _regroup_cast(w_in, main_src, bw, bf)
    w_small = jnp.concatenate([w_in[:, o_k:o_k + 2 * kv_w], w_in[:, o_ki:],
                               jnp.zeros((d_model, LANES - IDX_HEADS), f32)], axis=1).astype(bf)
    u_col = idx_w // d_a
    q_col, zb_col = u_col + 3, u_col + 4

    half = HEAD_DIM // 2
    inv = ROPE_THETA ** (-2.0 * jnp.arange(half, dtype=f32) / HEAD_DIM)

    def rope_tables(pos):
        ang = pos.astype(f32)[:, None] * inv[None, :]
        return (jnp.concatenate([jnp.cos(ang), jnp.cos(ang)], axis=1),
                jnp.concatenate([-jnp.sin(ang), jnp.sin(ang)], axis=1))

    pos_s = past + jnp.tile(jnp.arange(dec_t), dec_b)
    cos_p, sin_p = rope_tables(jnp.arange(seq))
    cos_s, sin_s = rope_tables(pos_s)

    hm_p, xn_p = _in_proj(x_p, g_norm, w_main, f32)
    hm_s, xn_s = _in_proj(x_s, g_norm, w_main, f32)
    hs_p, hs_s = _matmul(xn_p, w_small, f32), _matmul(xn_s, w_small, f32)

    pidx = np.arange(MLP_CHUNK)
    mask_p = (pidx[None, :] // CHUNK) <= (pidx[:, None] // CHUNK)
    wm_p = jnp.where(mask_p[None], w_s, 0.0).astype(bf)
    b_p = b_s[:, :, None]
    tidx = pidx % dec_t
    same = (pidx[None, :] // dec_t) == (pidx[:, None] // dec_t)
    mask_s = same & ((tidx[None, :] // CHUNK) <= (tidx[:, None] // CHUNK))
    wm_s = jnp.where(mask_s[None], w_s[:, tidx][:, :, tidx], 0.0).astype(bf)
    b_sm = b_s[:, tidx][:, :, None]
    gv = g_v.reshape(1, d_a)
    (mix_a_p,) = _mlp_group(hm_p, u_col, d_a, 0, seq, 512, wm_p, b_p, gv, False)
    mix_a_s, vn_s = _mlp_group(hm_s, u_col, d_a, 0, n_s, MLP_CHUNK, wm_s, b_sm, gv, True)

    qirt, qrt, kr_p, krb_p, vt_p, kir_p, kirb_p, wist, qn_p, kn_p = _prep(
        hm_p, hs_p, cos_p, sin_p, d_b, q_col, row0=0, nrows=seq, transposed=True)
    qir_s, qr_s, kr_s, krb_s, vb_s, kir_s, kirb_s, wis_s = _prep(
        hm_s, hs_s, cos_s, sin_s, d_b, q_col, row0=0, nrows=n_s, transposed=False)
    v_p, v_s_new = hs_p[:, kv_w:2 * kv_w], hs_s[:, kv_w:2 * kv_w]

    tq_p, kb_i, kb_a = PROMPT_TQ, PROMPT_KB_INDEX, PROMPT_KB_ATTN
    k_sel_p = min(TOPK_MAX, seq // 4)
    st_p, thr_p, taken_p, ties_p = _index_scores_t(qirt, wist, kirb_p, tq=tq_p, kb=kb_i, k_sel=k_sel_p)
    steps_p = _attention_steps(seq // tq_p, tq_p, kb_a)
    mix_b_p = _masked_attention_t(qrt, hm_p, zb_col, krb_p, vt_p, st_p, thr_p, taken_p, ties_p, qn_p, kn_p,
                                  steps_p, tq=tq_p, kb=kb_a)

    kb_s = SAMPLE_KB_INDEX
    n_keys = past + dec_t
    sk_s = _cdiv(n_keys, kb_s) * kb_s
    padk = lambda a: jnp.pad(a, ((0, 0), (0, sk_s - n_keys), (0, 0)))
    k_s = padk(jnp.concatenate([cache_k.reshape(dec_b, past, kv_w).astype(bf), krb_s.reshape(dec_b, dec_t, kv_w)], axis=1))
    v_s = padk(jnp.concatenate([cache_v.reshape(dec_b, past, kv_w).astype(bf), vb_s.reshape(dec_b, dec_t, kv_w)], axis=1))
    ki_s = padk(jnp.concatenate([cache_ki.astype(bf), kirb_s.reshape(dec_b, dec_t, IDX_DIM)], axis=1))
    kit_s = ki_s.transpose(0, 2, 1)
    kt_s = k_s.reshape(dec_b, sk_s, N_KV, HEAD_DIM).transpose(0, 2, 3, 1)
    kpos = jnp.arange(sk_s, dtype=jnp.int32)
    kc_s = jnp.broadcast_to(jnp.where(kpos < n_keys, kpos // CHUNK, FAR_CHUNK)[None, None, :], (dec_b, 1, sk_s))
    qc_s = (pos_s // CHUNK).astype(jnp.int32)[:, None]
    k_sel_s = min(TOPK_MAX, n_keys // 4)
    nkb_s = jnp.full((dec_b,), sk_s // kb_s, jnp.int32)
    s_s, thr_s, taken_s, ties_s = _index_scores(qir_s, wis_s, qc_s, kit_s, kc_s, nkb_s, row0=0, batches=dec_b, nq=1,
                                                tq=dec_t, kb=kb_s, k_sel=k_sel_s)
    mix_b_s = _masked_attention(qr_s, hm_s, zb_col, kt_s, v_s, s_s, thr_s, taken_s, ties_s,
                                zrow0=0, tq=dec_t, d_b=d_b)

    w_out_b = w_out.astype(bf)
    y_p = _out_proj(mix_a_p, mix_b_p, w_out_b, x_p, final_g)
    y_s = _out_proj(mix_a_s, mix_b_s, w_out_b, x_s, final_g)

    shp_p = (1, 1, seq, N_KV, HEAD_DIM)
    shp_s = (1, dec_b, dec_t, N_KV, HEAD_DIM)
    return (y_p[None], y_s.reshape(dec_b, dec_t, d_model),
            kr_p.reshape(shp_p), v_p.reshape(shp_p), kir_p.reshape(1, 1, seq, IDX_DIM),
            kr_s.reshape(shp_s), v_s_new.reshape(shp_s), kir_s.reshape(1, dec_b, dec_t, IDX_DIM),
            vn_s.reshape(1, dec_b, dec_t, groups, HEAD_DIM))


def kernel(x_prompt, x_sample, cache_k, cache_v, cache_idx_k, norm_g, w_in, w_s, b_s, v_norm_g, w_out, final_norm_g):
    assert x_prompt.shape[0] == 1 and norm_g.shape[0] == 1, "one prompt stream and one layer"
    return _layer(x_prompt, x_sample, cache_k[0], cache_v[0], cache_idx_k[0], norm_g[0], w_in[0], w_s[0], b_s[0],
                  v_norm_g[0], w_out[0], final_norm_g)
```

```python
import functools

import numpy as np
import jax
import jax.numpy as jnp
from jax import lax
from jax.experimental import pallas as pl
from jax.experimental.pallas import tpu as pltpu

CHUNK = 64
MLP_CHUNK = 128
HEAD_DIM = 128
N_KV = 4
IDX_HEADS = 32
IDX_DIM = 128
TOPK_MAX = 256
ROPE_THETA = 10000.0
EPS = 1e-6

LANES = 128
VMEM_LIMIT = 56 * 1024 * 1024
NEG_INF = float("-inf")
POS_INF = float("inf")
FAR_CHUNK = 1 << 20
LOG2_E = 1.4426950408889634
CHUNK_SHIFT = 6
PROMPT_TQ, PROMPT_KB_INDEX, PROMPT_KB_ATTN = 256, 256, 1024
ONES_ROWS = 16
VT_ROWS = HEAD_DIM + ONES_ROWS
ATTN_ROW_CHUNK = 128
TAKE_ALL = 1.0e9
RANK_CHUNK = 256
EXP_RANGE = 60.0
NORM_MARGIN = 1.001
SAMPLE_KB_INDEX = 256


def _largest_divisor(n, candidates):
    for c in candidates:
        if n % c == 0:
            return c
    raise ValueError(f"no tile in {candidates} divides {n}")


def _params(sem, flags=None):
    return pltpu.CompilerParams(dimension_semantics=sem, vmem_limit_bytes=VMEM_LIMIT, flags=flags)


def _cast_kernel(src_ref, w_ref, o_ref, *, bw, tail_valid):
    w = w_ref[...]
    if tail_valid < bw:
        col = lax.broadcasted_iota(jnp.int32, w.shape, 1)
        last = pl.program_id(0) == pl.num_programs(0) - 1
        w = jnp.where(jnp.logical_and(last, col >= tail_valid), 0.0, w)
    o_ref[...] = w.astype(o_ref.dtype)


def _regroup_cast(w, src_blocks, bw, out_dtype, tail_valid=None):
    d = w.shape[0]
    tail_valid = bw if tail_valid is None else tail_valid
    src = jnp.asarray(np.asarray(src_blocks, np.int32))
    grid_spec = pltpu.PrefetchScalarGridSpec(
        num_scalar_prefetch=1,
        grid=(len(src_blocks),),
        in_specs=[pl.BlockSpec((d, bw), lambda j, s: (0, s[j]))],
        out_specs=pl.BlockSpec((d, bw), lambda j, s: (0, j)),
    )
    return pl.pallas_call(
        functools.partial(_cast_kernel, bw=bw, tail_valid=tail_valid),
        grid_spec=grid_spec,
        out_shape=jax.ShapeDtypeStruct((d, len(src_blocks) * bw), out_dtype),
        compiler_params=_params(("arbitrary",)),
        name="regroup_cast",
    )(src, w)


def _in_proj_kernel(x_ref, g_ref, w_ref, o_ref, xn_ref):
    @pl.when(pl.program_id(1) == 0)
    def _():
        x = x_ref[...]
        ms = jnp.mean(x * x, axis=-1, keepdims=True)
        xn_ref[...] = (x * lax.rsqrt(ms + EPS) * g_ref[...]).astype(xn_ref.dtype)

    o_ref[...] = jnp.dot(xn_ref[...], w_ref[...], preferred_element_type=jnp.float32).astype(o_ref.dtype)


def _in_proj(x, g, w, out_dtype):
    m, k = x.shape
    n = w.shape[1]
    tm = _largest_divisor(m, (512, 256, 128))
    tn = _largest_divisor(n, (1024, 1280, 512, 256, 128))
    return pl.pallas_call(
        _in_proj_kernel,
        grid=(m // tm, n // tn),
        in_specs=[pl.BlockSpec((tm, k), lambda i, j: (i, 0)), pl.BlockSpec((1, k), lambda i, j: (0, 0)),
                  pl.BlockSpec((k, tn), lambda i, j: (0, j))],
        out_specs=pl.BlockSpec((tm, tn), lambda i, j: (i, j)),
        out_shape=jax.ShapeDtypeStruct((m, n), out_dtype),
        scratch_shapes=[pltpu.VMEM((tm, k), w.dtype)],
        compiler_params=_params(("arbitrary", "arbitrary")),
        name="in_proj",
    )(x, g.reshape(1, k), w)


def _silu(z):
    return z * (1.0 / (1.0 + jnp.exp(-z)))


def _mlp_kernel(u_ref, va_ref, za_ref, wm_ref, b_ref, gv_ref, mix_ref, *vn_refs, rows, groups):
    for c in range(rows // MLP_CHUNK):
        rs = slice(c * MLP_CHUNK, (c + 1) * MLP_CHUNK)
        for g in range(groups):
            cs = slice(g * HEAD_DIM, (g + 1) * HEAD_DIM)
            va = va_ref[rs, cs]
            mu = jnp.mean(va, axis=-1, keepdims=True)
            cen = va - mu
            var = jnp.mean(cen * cen, axis=-1, keepdims=True)
            vn = cen * lax.rsqrt(var + EPS) * gv_ref[:, cs]
            if vn_refs:
                vn_refs[0][rs, cs] = vn
            mixed = jnp.dot(wm_ref[g], vn.astype(jnp.bfloat16), preferred_element_type=jnp.float32) + b_ref[g]
            a = u_ref[rs, cs] * mixed
            mix_ref[rs, cs] = (a * _silu(za_ref[rs, cs])).astype(mix_ref.dtype)


def _mlp_group(h_main, col0, d_a, row0, nrows, rows, wm, bias, gv, emit_vn):
    groups = d_a // HEAD_DIM
    rb0 = row0 // rows
    out_shape = [jax.ShapeDtypeStruct((nrows, d_a), jnp.bfloat16)]
    out_specs = [pl.BlockSpec((rows, d_a), lambda i: (i, 0))]
    if emit_vn:
        out_shape.append(jax.ShapeDtypeStruct((nrows, d_a), jnp.float32))
        out_specs.append(pl.BlockSpec((rows, d_a), lambda i: (i, 0)))
    col = lambda c: pl.BlockSpec((rows, d_a), lambda i: (rb0 + i, c))
    return pl.pallas_call(
        functools.partial(_mlp_kernel, rows=rows, groups=groups),
        grid=(nrows // rows,),
        in_specs=[col(col0), col(col0 + 1), col(col0 + 2),
                  pl.BlockSpec((groups, MLP_CHUNK, MLP_CHUNK), lambda i: (0, 0, 0)),
                  pl.BlockSpec((groups, MLP_CHUNK, 1), lambda i: (0, 0, 0)),
                  pl.BlockSpec((1, d_a), lambda i: (0, 0))],
        out_specs=out_specs,
        out_shape=out_shape,
        compiler_params=_params(("arbitrary",)),
        name="chunk_mlp",
    )(h_main, h_main, h_main, wm, bias, gv)


def _rope(x, cos, sin):
    return x * cos + pltpu.roll(x, HEAD_DIM // 2, 1) * sin


def _prep_kernel(qi_ref, q_ref, k_ref, v_ref, ki_ref, wi_ref, cos_ref, sin_ref,
                 qir_ref, qr_ref, kr_ref, krb_ref, vb_ref, kir_ref, kirb_ref, wis_ref, *norm_refs,
                 n_heads, transposed):
    cos = cos_ref[...]
    sin = sin_ref[...]

    def put(ref, h, width, val):
        if transposed:
            ref[h * width:(h + 1) * width, :] = val.T.astype(ref.dtype)
        else:
            ref[:, h * width:(h + 1) * width] = val.astype(ref.dtype)

    for h in range(IDX_HEADS):
        put(qir_ref, h, IDX_DIM, _rope(qi_ref[:, h * IDX_DIM:(h + 1) * IDX_DIM], cos, sin))
    q_scale = HEAD_DIM ** -0.5 * LOG2_E
    for h in range(n_heads):
        qs = _rope(q_ref[:, h * HEAD_DIM:(h + 1) * HEAD_DIM], cos, sin) * q_scale
        put(qr_ref, h, HEAD_DIM, qs)
        if transposed:
            qb = qs.T.astype(qr_ref.dtype).astype(jnp.float32)
            norm_refs[0][h:h + 1, :] = jnp.sqrt(jnp.sum(qb * qb, axis=0, keepdims=True)) * NORM_MARGIN
    lane = lax.broadcasted_iota(jnp.int32, (8, LANES), 1)
    kn = jnp.zeros((8, LANES), jnp.float32)
    for h in range(N_KV):
        cs = slice(h * HEAD_DIM, (h + 1) * HEAD_DIM)
        kr = _rope(k_ref[:, cs], cos, sin)
        kr_ref[:, cs] = kr
        krb_ref[:, cs] = kr.astype(krb_ref.dtype)
        if transposed:
            vb_ref[h * VT_ROWS:h * VT_ROWS + HEAD_DIM, :] = v_ref[:, cs].T.astype(vb_ref.dtype)
            vb_ref[h * VT_ROWS + HEAD_DIM:(h + 1) * VT_ROWS, :] = jnp.ones((ONES_ROWS, v_ref.shape[0]), vb_ref.dtype)
            kb16 = kr.astype(krb_ref.dtype).astype(jnp.float32)
            kmax = jnp.sqrt(jnp.max(jnp.sum(kb16 * kb16, axis=1, keepdims=True), axis=0, keepdims=True))
            kn = jnp.where(lane == h, kmax, kn)
        else:
            vb_ref[:, cs] = v_ref[:, cs].astype(vb_ref.dtype)
    if transposed:
        norm_refs[1][0] = kn
    kir = _rope(ki_ref[...], cos, sin)
    kir_ref[...] = kir
    kirb_ref[...] = kir.astype(kirb_ref.dtype)
    put(wis_ref, 0, LANES, wi_ref[...] * ((IDX_HEADS * IDX_DIM) ** -0.5))


def _prep(h_main, h_small, cos_t, sin_t, d_b, q_col, *, row0, nrows, transposed):
    rows = _largest_divisor(nrows, (256, 128))
    rb0 = row0 // rows
    n_heads = d_b // HEAD_DIM
    idx_w = IDX_HEADS * IDX_DIM
    kv_w = N_KV * HEAD_DIM
    inspec = lambda w, c: pl.BlockSpec((rows, w), lambda i: (rb0 + i, c))
    bf, f32 = jnp.bfloat16, jnp.float32
    outs = [(idx_w, bf, True), (d_b, bf, True), (kv_w, f32, False), (kv_w, bf, False), (kv_w, bf, True),
            (IDX_DIM, f32, False), (IDX_DIM, bf, False), (LANES, f32, True)]
    out_specs, out_shape = [], []
    for idx, (w, dt, feature_major) in enumerate(outs):
        if transposed and feature_major:
            w = N_KV * VT_ROWS if idx == 4 else w
            out_specs.append(pl.BlockSpec((w, rows), lambda i: (0, i)))
            out_shape.append(jax.ShapeDtypeStruct((w, nrows), dt))
        else:
            out_specs.append(pl.BlockSpec((rows, w), lambda i: (i, 0)))
            out_shape.append(jax.ShapeDtypeStruct((nrows, w), dt))
    if transposed:
        out_specs += [pl.BlockSpec((n_heads, rows), lambda i: (0, i)), pl.BlockSpec((1, 8, LANES), lambda i: (i, 0, 0))]
        out_shape += [jax.ShapeDtypeStruct((n_heads, nrows), f32), jax.ShapeDtypeStruct((nrows // rows, 8, LANES), f32)]
    return pl.pallas_call(
        functools.partial(_prep_kernel, n_heads=n_heads, transposed=transposed),
        grid=(nrows // rows,),
        in_specs=[inspec(idx_w, 0), inspec(d_b, q_col),
                  inspec(kv_w, 0), inspec(kv_w, 1), inspec(IDX_DIM, 2 * kv_w // IDX_DIM),
                  inspec(LANES, 2 * kv_w // LANES + 1), inspec(HEAD_DIM, 0), inspec(HEAD_DIM, 0)],
        out_specs=out_specs,
        out_shape=out_shape,
        compiler_params=_params(("arbitrary",)),
        name="rope_prep",
    )(h_main, h_main, h_small, h_small, h_small, h_small, cos_t, sin_t)


def _bisect_start(mn, mx, nf, k_sel):
    return mn, mx, mn, jnp.where(nf <= float(k_sel), 1.0, 0.0), jnp.zeros_like(mn)


def _bisect_step(state, mid, cnt, kk):
    lo, hi, thr, done, tie = state
    active = done < 0.5
    hit = jnp.logical_and(active, cnt == kk)
    no_room = jnp.logical_or(mid <= lo, mid >= hi)
    stuck = jnp.logical_and(active, jnp.logical_and(no_room, cnt != kk))
    thr = jnp.where(hit, mid, thr)
    tie = jnp.where(stuck, 1.0, tie)
    lo = jnp.where(jnp.logical_and(active, cnt > kk), mid, lo)
    hi = jnp.where(jnp.logical_and(active, cnt < kk), mid, hi)
    done = jnp.where(jnp.logical_or(hit, stuck), 1.0, done)
    return lo, hi, thr, done, tie


def _kth_threshold(count, mn, mx, nf, k_sel, start=None):
    kk = jnp.minimum(nf, float(k_sel))

    def cond(st):
        return st[5] > 0.5

    def body(st):
        lo, hi = st[0], st[1]
        mid = 0.5 * lo + 0.5 * hi
        new = _bisect_step(st[:5], mid, count(mid, False), kk)
        return new + (jnp.sum(1.0 - new[3]),)

    start = _bisect_start(mn, mx, nf, k_sel) if start is None else start
    lo, hi, thr, _, tie, _ = lax.while_loop(cond, body, tuple(start) + (jnp.sum(1.0 - start[3]),))
    any_tie = jnp.sum(tie) > 0.5

    def resolve(_):
        is_tie = tie > 0.5
        kth = jnp.where(count(hi, False) >= kk, hi, lo)
        thr_t = jnp.where(is_tie, kth, thr)
        taken = jnp.where(is_tie, kk - count(thr_t, True), TAKE_ALL)
        return thr_t, taken

    thr, taken = lax.cond(any_tie, resolve, lambda _: (thr, jnp.full_like(thr, TAKE_ALL)), 0)
    return thr, taken, any_tie


def _index_kernel(nkb_ref, qi_ref, wi_ref, qc_ref, kit_ref, kc_ref, s_ref, thr_ref, taken_ref, tie_ref, wb_ref,
                  *, tq, kb, sk, k_sel):
    b = pl.program_id(0)
    i = pl.program_id(1)
    nkb = nkb_ref[b * pl.num_programs(1) + i]
    lane_tiles = kb // LANES
    rt = min(tq, 128)

    for h in range(IDX_HEADS):
        wb_ref[h] = jnp.broadcast_to(wi_ref[:, h:h + 1], (tq, LANES))

    def score_tile(j, carry):
        off = pl.multiple_of(j * kb, kb)
        kt = kit_ref[:, pl.ds(off, kb)]
        kc = kc_ref[:, pl.ds(off, kb)]
        for r0 in range(0, tq, rt):
            accs = [jnp.zeros((rt, LANES), jnp.float32) for _ in range(lane_tiles)]
            for h in range(IDX_HEADS):
                r = jnp.dot(qi_ref[r0:r0 + rt, h * IDX_DIM:(h + 1) * IDX_DIM], kt,
                            preferred_element_type=jnp.float32)
                w = wb_ref[h, r0:r0 + rt, :]
                for l in range(lane_tiles):
                    accs[l] = accs[l] + jnp.maximum(r[:, l * LANES:(l + 1) * LANES], 0.0) * w
            acc = jnp.concatenate(accs, axis=1)
            adm = kc <= qc_ref[r0:r0 + rt, :]
            s_ref[r0:r0 + rt, pl.ds(off, kb)] = jnp.where(adm, acc, NEG_INF)
        return carry

    lax.fori_loop(0, nkb, score_tile, 0)

    def fill_tile(j, carry):
        off = pl.multiple_of(j * kb, kb)
        s_ref[:, pl.ds(off, kb)] = jnp.full((tq, kb), NEG_INF, jnp.float32)
        return carry

    lax.fori_loop(nkb, sk // kb, fill_tile, 0)

    def stats_tile(j, carry):
        mx, mn, nf = carry
        off = pl.multiple_of(j * kb, kb)
        for l in range(lane_tiles):
            s = s_ref[:, pl.ds(off + l * LANES, LANES)]
            fin = s > NEG_INF
            mx = jnp.maximum(mx, s)
            mn = jnp.minimum(mn, jnp.where(fin, s, POS_INF))
            nf = nf + jnp.where(fin, 1.0, 0.0)
        return mx, mn, nf

    mx, mn, nf = lax.fori_loop(
        0, nkb, stats_tile,
        (jnp.full((tq, LANES), NEG_INF, jnp.float32), jnp.full((tq, LANES), POS_INF, jnp.float32),
         jnp.zeros((tq, LANES), jnp.float32)))
    mx = jnp.max(mx, axis=1, keepdims=True)
    mn = jnp.min(mn, axis=1, keepdims=True)
    nf = jnp.sum(nf, axis=1, keepdims=True)

    def count(v, strict):
        def tile(j, cnt):
            off = pl.multiple_of(j * kb, kb)
            for l in range(lane_tiles):
                s = s_ref[:, pl.ds(off + l * LANES, LANES)]
                cnt = cnt + jnp.where(s > v if strict else s >= v, 1.0, 0.0)
            return cnt
        cnt = lax.fori_loop(0, nkb, tile, jnp.zeros((tq, LANES), jnp.float32))
        return jnp.sum(cnt, axis=1, keepdims=True)

    thr, taken, any_tie = _kth_threshold(count, mn, mx, nf, k_sel)
    thr_ref[...] = jnp.broadcast_to(thr, (tq, LANES))
    taken_ref[...] = jnp.broadcast_to(taken, (tq, LANES))
    tie_ref[b * pl.num_programs(1) + i] = jnp.where(any_tie, 1, 0).astype(jnp.int32)


def _index_scores(qir, wis, qc, kit, kc, nkb, *, row0, batches, nq, tq, kb, k_sel):
    sk = kit.shape[-1]
    rb0 = row0 // tq
    grid_spec = pltpu.PrefetchScalarGridSpec(
        num_scalar_prefetch=1,
        grid=(batches, nq),
        in_specs=[pl.BlockSpec((tq, IDX_HEADS * IDX_DIM), lambda b, i, n: (rb0 + b * nq + i, 0)),
                  pl.BlockSpec((tq, LANES), lambda b, i, n: (rb0 + b * nq + i, 0)),
                  pl.BlockSpec((tq, 1), lambda b, i, n: (rb0 + b * nq + i, 0)),
                  pl.BlockSpec((None, IDX_DIM, sk), lambda b, i, n: (b, 0, 0)),
                  pl.BlockSpec((None, 1, sk), lambda b, i, n: (b, 0, 0))],
        out_specs=[pl.BlockSpec((None, tq, sk), lambda b, i, n: (b, i, 0)),
                   pl.BlockSpec((None, tq, LANES), lambda b, i, n: (b, i, 0)),
                   pl.BlockSpec((None, tq, LANES), lambda b, i, n: (b, i, 0)),
                   pl.BlockSpec(memory_space=pltpu.SMEM)],
        scratch_shapes=[pltpu.VMEM((IDX_HEADS, tq, LANES), jnp.float32)],
    )
    return pl.pallas_call(
        functools.partial(_index_kernel, tq=tq, kb=kb, sk=sk, k_sel=k_sel),
        grid_spec=grid_spec,
        out_shape=[jax.ShapeDtypeStruct((batches, nq * tq, sk), jnp.float32),
                   jax.ShapeDtypeStruct((batches, nq * tq, LANES), jnp.float32),
                   jax.ShapeDtypeStruct((batches, nq * tq, LANES), jnp.float32),
                   jax.ShapeDtypeStruct((batches * nq,), jnp.int32)],
        compiler_params=_params(("arbitrary", "arbitrary")),
        name="index_select",
    )(nkb, qir, wis, qc, kit, kc)


def _fold8(x, op):
    parts = [x[g * 8:(g + 1) * 8, :] for g in range(x.shape[0] // 8)]
    while len(parts) > 1:
        parts = [op(parts[a], parts[a + 1]) for a in range(0, len(parts) - 1, 2)] + (parts[-1:] if len(parts) % 2 else [])
    return parts[0]


def _index_kernel_t(qit_ref, wit_ref, ki_ref, st_ref, thr_ref, taken_ref, tie_ref, *, tq, kb, k_sel):
    i = pl.program_id(0)
    nkb = lax.div((i + 1) * tq + (kb - 1), kb)
    total = st_ref.shape[0] // kb
    qchunk = lax.shift_right_logical(i * tq + lax.broadcasted_iota(jnp.int32, (kb, tq), 1), CHUNK_SHIFT)
    krow = lax.broadcasted_iota(jnp.int32, (kb, tq), 0)

    def score_tile(j, carry):
        mx, mn, nf = carry
        off = pl.multiple_of(j * kb, kb)
        k_t = ki_ref[pl.ds(off, kb), :]
        acc = jnp.zeros((kb, tq), jnp.float32)
        for h in range(IDX_HEADS):
            r = jnp.dot(k_t, qit_ref[h * IDX_DIM:(h + 1) * IDX_DIM, :], preferred_element_type=jnp.float32)
            acc = acc + jnp.maximum(r, 0.0) * wit_ref[h:h + 1, :]
        adm = lax.shift_right_logical(off + krow, CHUNK_SHIFT) <= qchunk
        s = jnp.where(adm, acc, NEG_INF)
        st_ref[pl.ds(off, kb), :] = s
        fin = s > NEG_INF
        mx = jnp.maximum(mx, _fold8(s, jnp.maximum))
        mn = jnp.minimum(mn, _fold8(jnp.where(fin, s, POS_INF), jnp.minimum))
        nf = nf + _fold8(jnp.where(fin, 1.0, 0.0), jnp.add)
        return mx, mn, nf

    mx, mn, nf = lax.fori_loop(
        0, nkb, score_tile,
        (jnp.full((8, tq), NEG_INF, jnp.float32), jnp.full((8, tq), POS_INF, jnp.float32),
         jnp.zeros((8, tq), jnp.float32)))

    def fill_tile(j, carry):
        off = pl.multiple_of(j * kb, kb)
        st_ref[pl.ds(off, kb), :] = jnp.full((kb, tq), NEG_INF, jnp.float32)
        return carry

    lax.fori_loop(nkb, total, fill_tile, 0)
    mx = jnp.max(mx, axis=0, keepdims=True)
    mn = jnp.min(mn, axis=0, keepdims=True)
    nf = jnp.sum(nf, axis=0, keepdims=True)

    def count(v, strict):
        def tile(j, cnt):
            off = pl.multiple_of(j * kb, kb)
            s = st_ref[pl.ds(off, kb), :]
            return cnt + _fold8(jnp.where(s > v if strict else s >= v, 1.0, 0.0), jnp.add)
        cnt = lax.fori_loop(0, nkb, tile, jnp.zeros((8, tq), jnp.float32))
        return jnp.sum(cnt, axis=0, keepdims=True)

    thr, taken, any_tie = _kth_threshold(count, mn, mx, nf, k_sel)
    thr_ref[...] = jnp.broadcast_to(thr, (8, tq))
    taken_ref[...] = jnp.broadcast_to(taken, (8, tq))
    tie_ref[i] = jnp.where(any_tie, 1, 0).astype(jnp.int32)


def _index_scores_t(qirt, wist, kirb, *, tq, kb, k_sel):
    idx_w, seq = qirt.shape
    nq = seq // tq
    return pl.pallas_call(
        functools.partial(_index_kernel_t, tq=tq, kb=kb, k_sel=k_sel),
        grid=(nq,),
        in_specs=[pl.BlockSpec((idx_w, tq), lambda i: (0, i)),
                  pl.BlockSpec((LANES, tq), lambda i: (0, i)),
                  pl.BlockSpec((seq, IDX_DIM), lambda i: (0, 0))],
        out_specs=[pl.BlockSpec((seq, tq), lambda i: (0, i)),
                   pl.BlockSpec((8, tq), lambda i: (0, i)),
                   pl.BlockSpec((8, tq), lambda i: (0, i)),
                   pl.BlockSpec(memory_space=pltpu.SMEM)],
        out_shape=[jax.ShapeDtypeStruct((seq, seq), jnp.float32),
                   jax.ShapeDtypeStruct((8, seq), jnp.float32),
                   jax.ShapeDtypeStruct((8, seq), jnp.float32),
                   jax.ShapeDtypeStruct((nq,), jnp.int32)],
        compiler_params=_params(("arbitrary",)),
        name="index_select_t",
    )(qirt, wist, kirb)


def _tie_rank_matrix(n, lower):
    rows = lax.broadcasted_iota(jnp.int32, (n, n), 0)
    cols = lax.broadcasted_iota(jnp.int32, (n, n), 1)
    return jnp.where(cols < rows if lower else rows < cols, 1.0, 0.0).astype(jnp.bfloat16)


def _attn_kernel_t(qi_ref, kj_ref, nk_ref, tie_ref, qt_ref, zb_ref, k_ref, vt_ref, st_ref, thr_ref, taken_ref,
                   qn_ref, kn_ref, o_ref, m_ref, c_ref, acc_ref, s_ref, p_ref, bias_ref, seen_ref, *, n_heads):
    p = pl.program_id(0)
    kj = kj_ref[p]
    hpg = n_heads // N_KV

    @pl.when(kj == 0)
    def _():
        m_ref[...] = jnp.full(m_ref.shape, NEG_INF, jnp.float32)
        c_ref[...] = jnp.zeros(c_ref.shape, jnp.float32)
        acc_ref[...] = jnp.zeros(acc_ref.shape, jnp.float32)
        seen_ref[...] = jnp.zeros(seen_ref.shape, jnp.float32)

    kb = st_ref.shape[0]
    rc = ATTN_ROW_CHUNK
    thr = thr_ref[0:1, :]
    for r in range(0, kb, rc):
        bias_ref[r:r + rc, :] = jnp.where(st_ref[r:r + rc, :] >= thr, 0.0, NEG_INF)

    @pl.when(tie_ref[qi_ref[p]] != 0)
    def _():
        taken = taken_ref[0:1, :]
        ltri = _tie_rank_matrix(RANK_CHUNK, True)
        seen = seen_ref[...]
        for r in range(0, kb, RANK_CHUNK):
            s = st_ref[r:r + RANK_CHUNK, :]
            eq = s == thr
            rank = jnp.dot(ltri, jnp.where(eq, 1.0, 0.0).astype(jnp.bfloat16),
                           preferred_element_type=jnp.float32) + seen
            take = jnp.logical_or(s > thr, jnp.logical_and(eq, rank < taken))
            bias_ref[r:r + RANK_CHUNK, :] = jnp.where(take, 0.0, NEG_INF)
            seen = seen + jnp.sum(jnp.where(eq, 1.0, 0.0), axis=0, keepdims=True)
        seen_ref[...] = seen

    def logits(h, r):
        g = h // hpg
        return jnp.dot(k_ref[r:r + rc, g * HEAD_DIM:(g + 1) * HEAD_DIM], qt_ref[h * HEAD_DIM:(h + 1) * HEAD_DIM, :],
                       preferred_element_type=jnp.float32) + bias_ref[r:r + rc, :]

    def weighted_values(h):
        g = h // hpg
        return jnp.dot(vt_ref[g * VT_ROWS:(g + 1) * VT_ROWS, :], p_ref[h], preferred_element_type=jnp.float32)

    first = kj == 0
    shift0 = [jnp.where(m_ref[h] == NEG_INF, 0.0, m_ref[h]) for h in range(n_heads)]
    kn = kn_ref[0]
    for blk in range(1, kn_ref.shape[0]):
        kn = jnp.maximum(kn, kn_ref[blk])
    excess = None
    for h in range(n_heads):
        bound = qn_ref[h:h + 1, :] * kn[0:1, h // hpg:h // hpg + 1]
        e = bound - shift0[h]
        excess = e if excess is None else jnp.maximum(excess, e)
    in_range = jnp.max(excess) <= EXP_RANGE

    @pl.when(in_range)
    def _():
        for h in range(n_heads):
            cm = None
            for r in range(0, kb, rc):
                t = logits(h, r) - shift0[h]
                p_ref[h, r:r + rc, :] = jnp.exp2(t).astype(p_ref.dtype)
                c8 = _fold8(t, jnp.maximum)
                cm = c8 if cm is None else jnp.maximum(cm, c8)
            m_ref[h] = jnp.maximum(m_ref[h], shift0[h] + jnp.max(cm, axis=0, keepdims=True))
        for h in range(n_heads):
            alpha = jnp.where(first, 1.0, jnp.exp2(c_ref[h] - shift0[h]))
            acc_ref[h] = alpha * acc_ref[h] + weighted_values(h)
            c_ref[h] = shift0[h]

    @pl.when(jnp.logical_not(in_range))
    def _():
        cmax = []
        for h in range(n_heads):
            cm = None
            for r in range(0, kb, rc):
                s = logits(h, r)
                s_ref[h, r:r + rc, :] = s
                c8 = _fold8(s, jnp.maximum)
                cm = c8 if cm is None else jnp.maximum(cm, c8)
            cmax.append(jnp.max(cm, axis=0, keepdims=True))
        for h in range(n_heads):
            m_cur = jnp.maximum(m_ref[h], cmax[h])
            shift = jnp.where(m_cur == NEG_INF, 0.0, m_cur)
            for r in range(0, kb, rc):
                p_ref[h, r:r + rc, :] = jnp.exp2(s_ref[h, r:r + rc, :] - shift).astype(p_ref.dtype)
            alpha = jnp.where(first, 1.0, jnp.exp2(c_ref[h] - shift))
            acc_ref[h] = alpha * acc_ref[h] + weighted_values(h)
            c_ref[h] = shift
            m_ref[h] = m_cur

    @pl.when(kj == nk_ref[p] - 1)
    def _():
        for h in range(n_heads):
            cs = slice(h * HEAD_DIM, (h + 1) * HEAD_DIM)
            o = (acc_ref[h, 0:HEAD_DIM, :] / acc_ref[h, HEAD_DIM:HEAD_DIM + 1, :]).T
            o_ref[:, cs] = (o * _silu(zb_ref[:, cs])).astype(o_ref.dtype)


def _masked_attention_t(qrt, h_main, zb_col, krb, vt, st, thr, taken, ties, qn, kn, steps, *, tq, kb):
    qidx, kidx, nk = steps
    d_b, seq = qrt.shape
    n_heads = d_b // HEAD_DIM
    kv_w = N_KV * HEAD_DIM
    kn_blocks = kb * kn.shape[0] // seq
    assert kb % RANK_CHUNK == 0 and kb % ATTN_ROW_CHUNK == 0 and kn_blocks >= 1
    grid_spec = pltpu.PrefetchScalarGridSpec(
        num_scalar_prefetch=4,
        grid=(qidx.shape[0],),
        in_specs=[pl.BlockSpec((d_b, tq), lambda p, q, k, n, t: (0, q[p])),
                  pl.BlockSpec((tq, d_b), lambda p, q, k, n, t: (q[p], zb_col)),
                  pl.BlockSpec((kb, kv_w), lambda p, q, k, n, t: (k[p], 0)),
                  pl.BlockSpec((N_KV * VT_ROWS, kb), lambda p, q, k, n, t: (0, k[p])),
                  pl.BlockSpec((kb, tq), lambda p, q, k, n, t: (k[p], q[p])),
                  pl.BlockSpec((8, tq), lambda p, q, k, n, t: (0, q[p])),
                  pl.BlockSpec((8, tq), lambda p, q, k, n, t: (0, q[p])),
                  pl.BlockSpec((n_heads, tq), lambda p, q, k, n, t: (0, q[p])),
                  pl.BlockSpec((kn_blocks, 8, LANES), lambda p, q, k, n, t: (k[p], 0, 0))],
        out_specs=pl.BlockSpec((tq, d_b), lambda p, q, k, n, t: (q[p], 0)),
        scratch_shapes=[pltpu.VMEM((n_heads, 1, tq), jnp.float32),
                        pltpu.VMEM((n_heads, 1, tq), jnp.float32),
                        pltpu.VMEM((n_heads, VT_ROWS, tq), jnp.float32),
                        pltpu.VMEM((n_heads, kb, tq), jnp.float32),
                        pltpu.VMEM((n_heads, kb, tq), jnp.bfloat16),
                        pltpu.VMEM((kb, tq), jnp.float32),
                        pltpu.VMEM((1, tq), jnp.float32)],
    )
    return pl.pallas_call(
        functools.partial(_attn_kernel_t, n_heads=n_heads),
        grid_spec=grid_spec,
        out_shape=jax.ShapeDtypeStruct((seq, d_b), jnp.bfloat16),
        compiler_params=_params(("arbitrary",)),
        name="masked_attention_t",
    )(qidx, kidx, nk, ties, qrt, h_main, krb, vt, st, thr, taken, qn, kn)


def _attn_kernel(tie_ref, q_ref, zb_ref, kt_ref, v_ref, s_ref, thr_ref, taken_ref, o_ref, bias_ref, *, n_heads):
    b = pl.program_id(0)
    hpg = n_heads // N_KV
    sk = s_ref.shape[1]
    thr = thr_ref[:, 0:1]
    bias_ref[...] = jnp.where(s_ref[...] >= thr, 0.0, NEG_INF)

    @pl.when(tie_ref[b] != 0)
    def _():
        taken = taken_ref[:, 0:1]
        utri = _tie_rank_matrix(RANK_CHUNK, False)
        seen = jnp.zeros_like(thr)
        for c in range(0, sk, RANK_CHUNK):
            s = s_ref[:, c:c + RANK_CHUNK]
            eq = s == thr
            rank = jnp.dot(jnp.where(eq, 1.0, 0.0).astype(jnp.bfloat16), utri,
                           preferred_element_type=jnp.float32) + seen
            take = jnp.logical_or(s > thr, jnp.logical_and(eq, rank < taken))
            bias_ref[:, c:c + RANK_CHUNK] = jnp.where(take, 0.0, NEG_INF)
            seen = seen + jnp.sum(jnp.where(eq, 1.0, 0.0), axis=1, keepdims=True)

    for h in range(n_heads):
        g = h // hpg
        cs = slice(h * HEAD_DIM, (h + 1) * HEAD_DIM)
        s = jnp.dot(q_ref[:, cs], kt_ref[g], preferred_element_type=jnp.float32) + bias_ref[...]
        pr = jnp.exp2(s - jnp.max(s, axis=1, keepdims=True))
        o = jnp.dot(pr.astype(v_ref.dtype), v_ref[:, g * HEAD_DIM:(g + 1) * HEAD_DIM],
                    preferred_element_type=jnp.float32) / jnp.sum(pr, axis=1, keepdims=True)
        o_ref[:, cs] = (o * _silu(zb_ref[:, cs])).astype(o_ref.dtype)


def _masked_attention(qr, h_main, zb_col, kt, vb, scores, thr, taken, ties, *, zrow0, tq, d_b):
    batches, _, _, sk = kt.shape
    zrb0 = zrow0 // tq
    n_heads = d_b // HEAD_DIM
    kv_w = N_KV * HEAD_DIM
    assert sk % RANK_CHUNK == 0
    grid_spec = pltpu.PrefetchScalarGridSpec(
        num_scalar_prefetch=1,
        grid=(batches,),
        in_specs=[pl.BlockSpec((tq, d_b), lambda b, t: (b, 0)),
                  pl.BlockSpec((tq, d_b), lambda b, t: (zrb0 + b, zb_col)),
                  pl.BlockSpec((None, N_KV, HEAD_DIM, sk), lambda b, t: (b, 0, 0, 0)),
                  pl.BlockSpec((None, sk, kv_w), lambda b, t: (b, 0, 0)),
                  pl.BlockSpec((None, tq, sk), lambda b, t: (b, 0, 0)),
                  pl.BlockSpec((None, tq, LANES), lambda b, t: (b, 0, 0)),
                  pl.BlockSpec((None, tq, LANES), lambda b, t: (b, 0, 0))],
        out_specs=pl.BlockSpec((tq, d_b), lambda b, t: (b, 0)),
        scratch_shapes=[pltpu.VMEM((tq, sk), jnp.float32)],
    )
    return pl.pallas_call(
        functools.partial(_attn_kernel, n_heads=n_heads),
        grid_spec=grid_spec,
        out_shape=jax.ShapeDtypeStruct((batches * tq, d_b), jnp.bfloat16),
        compiler_params=_params(("arbitrary",)),
        name="masked_attention",
    )(ties, qr, h_main, kt, vb, scores, thr, taken)


def _out_kernel(mixa_ref, mixb_ref, w_ref, x_ref, g_ref, o_ref, ssq_ref, *, d_a, tn):
    j = pl.program_id(1)

    @pl.when(j == 0)
    def _():
        ssq_ref[...] = jnp.zeros(ssq_ref.shape, jnp.float32)

    y = (x_ref[...] + jnp.dot(mixa_ref[...], w_ref[0:d_a, :], preferred_element_type=jnp.float32)
         + jnp.dot(mixb_ref[...], w_ref[d_a:, :], preferred_element_type=jnp.float32))
    o_ref[:, pl.ds(pl.multiple_of(j * tn, tn), tn)] = y
    ssq_ref[...] += jnp.sum(y * y, axis=-1, keepdims=True)

    @pl.when(j == pl.num_programs(1) - 1)
    def _():
        inv = lax.rsqrt(ssq_ref[...] * (1.0 / o_ref.shape[1]) + EPS)
        o_ref[...] = o_ref[...] * inv * g_ref[...]


def _out_proj(mix_a, mix_b, w, x, g):
    n, d = x.shape
    d_a, d_b = mix_a.shape[1], mix_b.shape[1]
    tm = _largest_divisor(n, (512, 256, 128))
    tn = _largest_divisor(d, (1024, 512, 256, 128))
    return pl.pallas_call(
        functools.partial(_out_kernel, d_a=d_a, tn=tn),
        grid=(n // tm, d // tn),
        in_specs=[pl.BlockSpec((tm, d_a), lambda i, j: (i, 0)),
                  pl.BlockSpec((tm, d_b), lambda i, j: (i, 0)),
                  pl.BlockSpec((d_a + d_b, tn), lambda i, j: (0, j)),
                  pl.BlockSpec((tm, tn), lambda i, j: (i, j)), pl.BlockSpec((1, d), lambda i, j: (0, 0))],
        out_specs=pl.BlockSpec((tm, d), lambda i, j: (i, 0)),
        out_shape=jax.ShapeDtypeStruct((n, d), jnp.float32),
        scratch_shapes=[pltpu.VMEM((tm, 1), jnp.float32)],
        compiler_params=_params(("arbitrary", "arbitrary")),
        name="out_proj",
    )(mix_a, mix_b, w, x, g.reshape(1, d))


def _cdiv(a, b):
    return -(-a // b)


def _attention_steps(nq, tq, kb):
    q_l, k_l, n_l = [], [], []
    for q in range(nq):
        nk = _cdiv((q + 1) * tq, kb)
        for k in range(nk):
            q_l.append(q), k_l.append(k), n_l.append(nk)
    return tuple(jnp.asarray(np.asarray(v, np.int32)) for v in (q_l, k_l, n_l))


def _layer(x_prompt, x_sample, cache_k, cache_v, cache_ki, g_norm, w_in, w_s, b_s, g_v, w_out, final_g):
    f32, bf = jnp.float32, jnp.bfloat16
    _, seq, d_model = x_prompt.shape
    dec_b, dec_t, _ = x_sample.shape
    past = cache_k.shape[1]
    d_a = d_model // 2
    d_b = d_model - d_a
    groups = d_a // HEAD_DIM
    kv_w = N_KV * HEAD_DIM
    idx_w = IDX_HEADS * IDX_DIM
    n_s = dec_b * dec_t
    n_all = seq + n_s
    assert seq % 512 == 0 and n_s % MLP_CHUNK == 0 and dec_t <= CHUNK and past % MLP_CHUNK == 0
    assert idx_w % d_a == 0 and d_a % HEAD_DIM == 0

    x_p, x_s = x_prompt[0], x_sample.reshape(n_s, d_model)
    widths = (d_a, d_a, d_a, d_b, kv_w, kv_w, d_b, idx_w, IDX_DIM, IDX_HEADS)
    o_u, o_va, o_za, o_q, o_k, o_v, o_zb, o_qi, o_ki, o_wi = np.concatenate([[0], np.cumsum(widths)])[:-1].tolist()

    def col_blocks(segments, bw):
        return [(off + c) // bw for off, wd in segments for c in range(0, wd, bw)]

    bw = int(np.gcd.reduce([d_a, d_b, kv_w, idx_w, o_q, o_zb, o_qi]))
    w_main = _regroup_cast(w_in, col_blocks(((o_qi, idx_w), (o_u, d_a), (o_va, d_a), (o_za, d_a), (o_q, d_b),
                                             (o_zb, d_b)), bw), bw, bf)
    assert o_k % LANES == 0 and o_ki % LANES == 0 and o_wi % LANES == 0
    w_small = _regroup_cast(w_in, col_blocks(((o_k, 2 * kv_w), (o_ki, IDX_DIM), (o_wi, LANES)), LANES), LANES, bf,
                            tail_valid=IDX_HEADS)
    u_col = idx_w // d_a
    q_col, zb_col = u_col + 3, u_col + 4

    half = HEAD_DIM // 2
    inv = ROPE_THETA ** (-2.0 * jnp.arange(half, dtype=f32) / HEAD_DIM)

    def rope_tables(pos):
        ang = pos.astype(f32)[:, None] * inv[None, :]
        return (jnp.concatenate([jnp.cos(ang), jnp.cos(ang)], axis=1),
                jnp.concatenate([-jnp.sin(ang), jnp.sin(ang)], axis=1))

    pos_s = past + jnp.tile(jnp.arange(dec_t), dec_b)
    cos_p, sin_p = rope_tables(jnp.arange(seq))
    cos_s, sin_s = rope_tables(pos_s)

    hm_p, hs_p = _in_proj(x_p, g_norm, w_main, f32), _in_proj(x_p, g_norm, w_small, f32)
    hm_s, hs_s = _in_proj(x_s, g_norm, w_main, f32), _in_proj(x_s, g_norm, w_small, f32)

    pidx = np.arange(MLP_CHUNK)
    mask_p = (pidx[None, :] // CHUNK) <= (pidx[:, None] // CHUNK)
    wm_p = jnp.where(mask_p[None], w_s, 0.0).astype(bf)
    b_p = b_s[:, :, None]
    tidx = pidx % dec_t
    same = (pidx[None, :] // dec_t) == (pidx[:, None] // dec_t)
    mask_s = same & ((tidx[None, :] // CHUNK) <= (tidx[:, None] // CHUNK))
    wm_s = jnp.where(mask_s[None], w_s[:, tidx][:, :, tidx], 0.0).astype(bf)
    b_sm = b_s[:, tidx][:, :, None]
    gv = g_v.reshape(1, d_a)
    (mix_a_p,) = _mlp_group(hm_p, u_col, d_a, 0, seq, 512, wm_p, b_p, gv, False)
    mix_a_s, vn_s = _mlp_group(hm_s, u_col, d_a, 0, n_s, MLP_CHUNK, wm_s, b_sm, gv, True)

    qirt, qrt, kr_p, krb_p, vt_p, kir_p, kirb_p, wist, qn_p, kn_p = _prep(
        hm_p, hs_p, cos_p, sin_p, d_b, q_col, row0=0, nrows=seq, transposed=True)
    qir_s, qr_s, kr_s, krb_s, vb_s, kir_s, kirb_s, wis_s = _prep(
        hm_s, hs_s, cos_s, sin_s, d_b, q_col, row0=0, nrows=n_s, transposed=False)
    v_p, v_s_new = hs_p[:, kv_w:2 * kv_w], hs_s[:, kv_w:2 * kv_w]

    tq_p, kb_i, kb_a = PROMPT_TQ, PROMPT_KB_INDEX, PROMPT_KB_ATTN
    k_sel_p = min(TOPK_MAX, seq // 4)
    st_p, thr_p, taken_p, ties_p = _index_scores_t(qirt, wist, kirb_p, tq=tq_p, kb=kb_i, k_sel=k_sel_p)
    steps_p = _attention_steps(seq // tq_p, tq_p, kb_a)
    mix_b_p = _masked_attention_t(qrt, hm_p, zb_col, krb_p, vt_p, st_p, thr_p, taken_p, ties_p, qn_p, kn_p,
                                  steps_p, tq=tq_p, kb=kb_a)

    kb_s = SAMPLE_KB_INDEX
    n_keys = past + dec_t
    sk_s = _cdiv(n_keys, kb_s) * kb_s
    padk = lambda a: jnp.pad(a, ((0, 0), (0, sk_s - n_keys), (0, 0)))
    k_s = padk(jnp.concatenate([cache_k.reshape(dec_b, past, kv_w).astype(bf), krb_s.reshape(dec_b, dec_t, kv_w)], axis=1))
    v_s = padk(jnp.concatenate([cache_v.reshape(dec_b, past, kv_w).astype(bf), vb_s.reshape(dec_b, dec_t, kv_w)], axis=1))
    ki_s = padk(jnp.concatenate([cache_ki.astype(bf), kirb_s.reshape(dec_b, dec_t, IDX_DIM)], axis=1))
    kit_s = ki_s.transpose(0, 2, 1)
    kt_s = k_s.reshape(dec_b, sk_s, N_KV, HEAD_DIM).transpose(0, 2, 3, 1)
    kpos = jnp.arange(sk_s, dtype=jnp.int32)
    kc_s = jnp.broadcast_to(jnp.where(kpos < n_keys, kpos // CHUNK, FAR_CHUNK)[None, None, :], (dec_b, 1, sk_s))
    qc_s = (pos_s // CHUNK).astype(jnp.int32)[:, None]
    k_sel_s = min(TOPK_MAX, n_keys // 4)
    nkb_s = jnp.full((dec_b,), sk_s // kb_s, jnp.int32)
    s_s, thr_s, taken_s, ties_s = _index_scores(qir_s, wis_s, qc_s, kit_s, kc_s, nkb_s, row0=0, batches=dec_b, nq=1,
                                                tq=dec_t, kb=kb_s, k_sel=k_sel_s)
    mix_b_s = _masked_attention(qr_s, hm_s, zb_col, kt_s, v_s, s_s, thr_s, taken_s, ties_s,
                                zrow0=0, tq=dec_t, d_b=d_b)

    w_out_b = w_out.astype(bf)
    y_p = _out_proj(mix_a_p, mix_b_p, w_out_b, x_p, final_g)
    y_s = _out_proj(mix_a_s, mix_b_s, w_out_b, x_s, final_g)

    shp_p = (1, 1, seq, N_KV, HEAD_DIM)
    shp_s = (1, dec_b, dec_t, N_KV, HEAD_DIM)
    return (y_p[None], y_s.reshape(dec_b, dec_t, d_model),
            kr_p.reshape(shp_p), v_p.reshape(shp_p), kir_p.reshape(1, 1, seq, IDX_DIM),
            kr_s.reshape(shp_s), v_s_new.reshape(shp_s), kir_s.reshape(1, dec_b, dec_t, IDX_DIM),
            vn_s.reshape(1, dec_b, dec_t, groups, HEAD_DIM))


def kernel(x_prompt, x_sample, cache_k, cache_v, cache_idx_k, norm_g, w_in, w_s, b_s, v_norm_g, w_out, final_norm_g):
    assert x_prompt.shape[0] == 1 and norm_g.shape[0] == 1, "one prompt stream and one layer"
    return _layer(x_prompt, x_sample, cache_k[0], cache_v[0], cache_idx_k[0], norm_g[0], w_in[0], w_s[0], b_s[0],
                  v_norm_g[0], w_out[0], final_norm_g)
```

```python
import functools

import numpy as np
import jax
import jax.numpy as jnp
from jax import lax
from jax.experimental import pallas as pl
from jax.experimental.pallas import tpu as pltpu

CHUNK = 64
MLP_CHUNK = 128
HEAD_DIM = 128
N_KV = 4
IDX_HEADS = 32
IDX_DIM = 128
TOPK_MAX = 256
ROPE_THETA = 10000.0
EPS = 1e-6

LANES = 128
VMEM_LIMIT = 56 * 1024 * 1024
NEG_INF = float("-inf")
POS_INF = float("inf")
FAR_CHUNK = 1 << 20
LOG2_E = 1.4426950408889634
CHUNK_SHIFT = 6
PROMPT_TQ, PROMPT_KB_INDEX, PROMPT_KB_ATTN = 256, 256, 1024
ONES_ROWS = 16
VT_ROWS = HEAD_DIM + ONES_ROWS
ATTN_ROW_CHUNK = 128
TAKE_ALL = 1.0e9
RANK_CHUNK = 256
EXP_RANGE = 60.0
NORM_MARGIN = 1.001
SAMPLE_KB_INDEX = 256


def _largest_divisor(n, candidates):
    for c in candidates:
        if n % c == 0:
            return c
    raise ValueError(f"no tile in {candidates} divides {n}")


def _params(sem, flags=None):
    return pltpu.CompilerParams(dimension_semantics=sem, vmem_limit_bytes=VMEM_LIMIT, flags=flags)


def _cast_kernel(src_ref, w_ref, o_ref, *, bw, tail_valid):
    w = w_ref[...]
    if tail_valid < bw:
        row = lax.broadcasted_iota(jnp.int32, w.shape, 0)
        last = pl.program_id(0) == pl.num_programs(0) - 1
        w = jnp.where(jnp.logical_and(last, row >= tail_valid), 0.0, w)
    o_ref[...] = w.astype(o_ref.dtype)


def _regroup_cast(wt, src_blocks, bw, out_dtype, tail_valid=None):
    d = wt.shape[1]
    tail_valid = bw if tail_valid is None else tail_valid
    src = jnp.asarray(np.asarray(src_blocks, np.int32))
    grid_spec = pltpu.PrefetchScalarGridSpec(
        num_scalar_prefetch=1,
        grid=(len(src_blocks),),
        in_specs=[pl.BlockSpec((bw, d), lambda j, s: (s[j], 0))],
        out_specs=pl.BlockSpec((bw, d), lambda j, s: (j, 0)),
    )
    return pl.pallas_call(
        functools.partial(_cast_kernel, bw=bw, tail_valid=tail_valid),
        grid_spec=grid_spec,
        out_shape=jax.ShapeDtypeStruct((len(src_blocks) * bw, d), out_dtype),
        compiler_params=_params(("arbitrary",)),
        name="regroup_cast",
    )(src, wt)


def _in_proj_kernel(x_ref, g_ref, wt_ref, o_ref, xn_ref):
    @pl.when(pl.program_id(1) == 0)
    def _():
        x = x_ref[...]
        ms = jnp.mean(x * x, axis=-1, keepdims=True)
        xn_ref[...] = (x * lax.rsqrt(ms + EPS) * g_ref[...]).astype(xn_ref.dtype)

    o_ref[...] = lax.dot_general(xn_ref[...], wt_ref[...], (((1,), (1,)), ((), ())),
                                 preferred_element_type=jnp.float32).astype(o_ref.dtype)


def _in_proj(x, g, wt, out_dtype):
    m, k = x.shape
    n = wt.shape[0]
    tm = _largest_divisor(m, (512, 256, 128))
    tn = _largest_divisor(n, (1024, 1280, 512, 256, 128))
    return pl.pallas_call(
        _in_proj_kernel,
        grid=(m // tm, n // tn),
        in_specs=[pl.BlockSpec((tm, k), lambda i, j: (i, 0)), pl.BlockSpec((1, k), lambda i, j: (0, 0)),
                  pl.BlockSpec((tn, k), lambda i, j: (j, 0))],
        out_specs=pl.BlockSpec((tm, tn), lambda i, j: (i, j)),
        out_shape=jax.ShapeDtypeStruct((m, n), out_dtype),
        scratch_shapes=[pltpu.VMEM((tm, k), wt.dtype)],
        compiler_params=_params(("arbitrary", "arbitrary")),
        name="in_proj",
    )(x, g.reshape(1, k), wt)


def _silu(z):
    return z * (1.0 / (1.0 + jnp.exp(-z)))


def _mlp_kernel(u_ref, va_ref, za_ref, wm_ref, b_ref, gv_ref, mix_ref, *vn_refs, rows, groups):
    for c in range(rows // MLP_CHUNK):
        rs = slice(c * MLP_CHUNK, (c + 1) * MLP_CHUNK)
        for g in range(groups):
            cs = slice(g * HEAD_DIM, (g + 1) * HEAD_DIM)
            va = va_ref[rs, cs]
            mu = jnp.mean(va, axis=-1, keepdims=True)
            cen = va - mu
            var = jnp.mean(cen * cen, axis=-1, keepdims=True)
            vn = cen * lax.rsqrt(var + EPS) * gv_ref[:, cs]
            if vn_refs:
                vn_refs[0][rs, cs] = vn
            mixed = jnp.dot(wm_ref[g], vn.astype(jnp.bfloat16), preferred_element_type=jnp.float32) + b_ref[g]
            a = u_ref[rs, cs] * mixed
            mix_ref[rs, cs] = (a * _silu(za_ref[rs, cs])).astype(mix_ref.dtype)


def _mlp_group(h_main, col0, d_a, row0, nrows, rows, wm, bias, gv, emit_vn):
    groups = d_a // HEAD_DIM
    rb0 = row0 // rows
    out_shape = [jax.ShapeDtypeStruct((nrows, d_a), jnp.bfloat16)]
    out_specs = [pl.BlockSpec((rows, d_a), lambda i: (i, 0))]
    if emit_vn:
        out_shape.append(jax.ShapeDtypeStruct((nrows, d_a), jnp.float32))
        out_specs.append(pl.BlockSpec((rows, d_a), lambda i: (i, 0)))
    col = lambda c: pl.BlockSpec((rows, d_a), lambda i: (rb0 + i, c))
    return pl.pallas_call(
        functools.partial(_mlp_kernel, rows=rows, groups=groups),
        grid=(nrows // rows,),
        in_specs=[col(col0), col(col0 + 1), col(col0 + 2),
                  pl.BlockSpec((groups, MLP_CHUNK, MLP_CHUNK), lambda i: (0, 0, 0)),
                  pl.BlockSpec((groups, MLP_CHUNK, 1), lambda i: (0, 0, 0)),
                  pl.BlockSpec((1, d_a), lambda i: (0, 0))],
        out_specs=out_specs,
        out_shape=out_shape,
        compiler_params=_params(("arbitrary",)),
        name="chunk_mlp",
    )(h_main, h_main, h_main, wm, bias, gv)


def _rope(x, cos, sin):
    return x * cos + pltpu.roll(x, HEAD_DIM // 2, 1) * sin


def _prep_kernel(qi_ref, q_ref, k_ref, v_ref, ki_ref, wi_ref, cos_ref, sin_ref,
                 qir_ref, qr_ref, kr_ref, krb_ref, vb_ref, kir_ref, kirb_ref, wis_ref, *norm_refs,
                 n_heads, transposed):
    cos = cos_ref[...]
    sin = sin_ref[...]

    def put(ref, h, width, val):
        if transposed:
            ref[h * width:(h + 1) * width, :] = val.T.astype(ref.dtype)
        else:
            ref[:, h * width:(h + 1) * width] = val.astype(ref.dtype)

    for h in range(IDX_HEADS):
        put(qir_ref, h, IDX_DIM, _rope(qi_ref[:, h * IDX_DIM:(h + 1) * IDX_DIM], cos, sin))
    q_scale = HEAD_DIM ** -0.5 * LOG2_E
    for h in range(n_heads):
        qs = _rope(q_ref[:, h * HEAD_DIM:(h + 1) * HEAD_DIM], cos, sin) * q_scale
        put(qr_ref, h, HEAD_DIM, qs)
        if transposed:
            qb = qs.T.astype(qr_ref.dtype).astype(jnp.float32)
            norm_refs[0][h:h + 1, :] = jnp.sqrt(jnp.sum(qb * qb, axis=0, keepdims=True)) * NORM_MARGIN
    lane = lax.broadcasted_iota(jnp.int32, (8, LANES), 1)
    kn = jnp.zeros((8, LANES), jnp.float32)
    for h in range(N_KV):
        cs = slice(h * HEAD_DIM, (h + 1) * HEAD_DIM)
        kr = _rope(k_ref[:, cs], cos, sin)
        kr_ref[:, cs] = kr
        krb_ref[:, cs] = kr.astype(krb_ref.dtype)
        if transposed:
            vb_ref[h * VT_ROWS:h * VT_ROWS + HEAD_DIM, :] = v_ref[:, cs].T.astype(vb_ref.dtype)
            vb_ref[h * VT_ROWS + HEAD_DIM:(h + 1) * VT_ROWS, :] = jnp.ones((ONES_ROWS, v_ref.shape[0]), vb_ref.dtype)
            kb16 = kr.astype(krb_ref.dtype).astype(jnp.float32)
            kmax = jnp.sqrt(jnp.max(jnp.sum(kb16 * kb16, axis=1, keepdims=True), axis=0, keepdims=True))
            kn = jnp.where(lane == h, kmax, kn)
        else:
            vb_ref[:, cs] = v_ref[:, cs].astype(vb_ref.dtype)
    if transposed:
        norm_refs[1][0] = kn
    kir = _rope(ki_ref[...], cos, sin)
    kir_ref[...] = kir
    kirb_ref[...] = kir.astype(kirb_ref.dtype)
    put(wis_ref, 0, LANES, wi_ref[...] * ((IDX_HEADS * IDX_DIM) ** -0.5))


def _prep(h_main, h_small, cos_t, sin_t, d_b, q_col, *, row0, nrows, transposed):
    rows = _largest_divisor(nrows, (256, 128))
    rb0 = row0 // rows
    n_heads = d_b // HEAD_DIM
    idx_w = IDX_HEADS * IDX_DIM
    kv_w = N_KV * HEAD_DIM
    inspec = lambda w, c: pl.BlockSpec((rows, w), lambda i: (rb0 + i, c))
    bf, f32 = jnp.bfloat16, jnp.float32
    outs = [(idx_w, bf, True), (d_b, bf, True), (kv_w, f32, False), (kv_w, bf, False), (kv_w, bf, True),
            (IDX_DIM, f32, False), (IDX_DIM, bf, False), (LANES, f32, True)]
    out_specs, out_shape = [], []
    for idx, (w, dt, feature_major) in enumerate(outs):
        if transposed and feature_major:
            w = N_KV * VT_ROWS if idx == 4 else w
            out_specs.append(pl.BlockSpec((w, rows), lambda i: (0, i)))
            out_shape.append(jax.ShapeDtypeStruct((w, nrows), dt))
        else:
            out_specs.append(pl.BlockSpec((rows, w), lambda i: (i, 0)))
            out_shape.append(jax.ShapeDtypeStruct((nrows, w), dt))
    if transposed:
        out_specs += [pl.BlockSpec((n_heads, rows), lambda i: (0, i)), pl.BlockSpec((1, 8, LANES), lambda i: (i, 0, 0))]
        out_shape += [jax.ShapeDtypeStruct((n_heads, nrows), f32), jax.ShapeDtypeStruct((nrows // rows, 8, LANES), f32)]
    return pl.pallas_call(
        functools.partial(_prep_kernel, n_heads=n_heads, transposed=transposed),
        grid=(nrows // rows,),
        in_specs=[inspec(idx_w, 0), inspec(d_b, q_col),
                  inspec(kv_w, 0), inspec(kv_w, 1), inspec(IDX_DIM, 2 * kv_w // IDX_DIM),
                  inspec(LANES, 2 * kv_w // LANES + 1), inspec(HEAD_DIM, 0), inspec(HEAD_DIM, 0)],
        out_specs=out_specs,
        out_shape=out_shape,
        compiler_params=_params(("arbitrary",)),
        name="rope_prep",
    )(h_main, h_main, h_small, h_small, h_small, h_small, cos_t, sin_t)


def _bisect_start(mn, mx, nf, k_sel):
    return mn, mx, mn, jnp.where(nf <= float(k_sel), 1.0, 0.0), jnp.zeros_like(mn)


def _bisect_step(state, mid, cnt, kk):
    lo, hi, thr, done, tie = state
    active = done < 0.5
    hit = jnp.logical_and(active, cnt == kk)
    no_room = jnp.logical_or(mid <= lo, mid >= hi)
    stuck = jnp.logical_and(active, jnp.logical_and(no_room, cnt != kk))
    thr = jnp.where(hit, mid, thr)
    tie = jnp.where(stuck, 1.0, tie)
    lo = jnp.where(jnp.logical_and(active, cnt > kk), mid, lo)
    hi = jnp.where(jnp.logical_and(active, cnt < kk), mid, hi)
    done = jnp.where(jnp.logical_or(hit, stuck), 1.0, done)
    return lo, hi, thr, done, tie


def _kth_threshold(count, mn, mx, nf, k_sel, start=None):
    kk = jnp.minimum(nf, float(k_sel))

    def cond(st):
        return st[5] > 0.5

    def body(st):
        lo, hi = st[0], st[1]
        mid = 0.5 * lo + 0.5 * hi
        new = _bisect_step(st[:5], mid, count(mid, False), kk)
        return new + (jnp.sum(1.0 - new[3]),)

    start = _bisect_start(mn, mx, nf, k_sel) if start is None else start
    lo, hi, thr, _, tie, _ = lax.while_loop(cond, body, tuple(start) + (jnp.sum(1.0 - start[3]),))
    any_tie = jnp.sum(tie) > 0.5

    def resolve(_):
        is_tie = tie > 0.5
        kth = jnp.where(count(hi, False) >= kk, hi, lo)
        thr_t = jnp.where(is_tie, kth, thr)
        taken = jnp.where(is_tie, kk - count(thr_t, True), TAKE_ALL)
        return thr_t, taken

    thr, taken = lax.cond(any_tie, resolve, lambda _: (thr, jnp.full_like(thr, TAKE_ALL)), 0)
    return thr, taken, any_tie


def _index_kernel(nkb_ref, qi_ref, wi_ref, qc_ref, kit_ref, kc_ref, s_ref, thr_ref, taken_ref, tie_ref, wb_ref,
                  *, tq, kb, sk, k_sel):
    b = pl.program_id(0)
    i = pl.program_id(1)
    nkb = nkb_ref[b * pl.num_programs(1) + i]
    lane_tiles = kb // LANES
    rt = min(tq, 128)

    for h in range(IDX_HEADS):
        wb_ref[h] = jnp.broadcast_to(wi_ref[:, h:h + 1], (tq, LANES))

    def score_tile(j, carry):
        off = pl.multiple_of(j * kb, kb)
        kt = kit_ref[:, pl.ds(off, kb)]
        kc = kc_ref[:, pl.ds(off, kb)]
        for r0 in range(0, tq, rt):
            accs = [jnp.zeros((rt, LANES), jnp.float32) for _ in range(lane_tiles)]
            for h in range(IDX_HEADS):
                r = jnp.dot(qi_ref[r0:r0 + rt, h * IDX_DIM:(h + 1) * IDX_DIM], kt,
                            preferred_element_type=jnp.float32)
                w = wb_ref[h, r0:r0 + rt, :]
                for l in range(lane_tiles):
                    accs[l] = accs[l] + jnp.maximum(r[:, l * LANES:(l + 1) * LANES], 0.0) * w
            acc = jnp.concatenate(accs, axis=1)
            adm = kc <= qc_ref[r0:r0 + rt, :]
            s_ref[r0:r0 + rt, pl.ds(off, kb)] = jnp.where(adm, acc, NEG_INF)
        return carry

    lax.fori_loop(0, nkb, score_tile, 0)

    def fill_tile(j, carry):
        off = pl.multiple_of(j * kb, kb)
        s_ref[:, pl.ds(off, kb)] = jnp.full((tq, kb), NEG_INF, jnp.float32)
        return carry

    lax.fori_loop(nkb, sk // kb, fill_tile, 0)

    def stats_tile(j, carry):
        mx, mn, nf = carry
        off = pl.multiple_of(j * kb, kb)
        for l in range(lane_tiles):
            s = s_ref[:, pl.ds(off + l * LANES, LANES)]
            fin = s > NEG_INF
            mx = jnp.maximum(mx, s)
            mn = jnp.minimum(mn, jnp.where(fin, s, POS_INF))
            nf = nf + jnp.where(fin, 1.0, 0.0)
        return mx, mn, nf

    mx, mn, nf = lax.fori_loop(
        0, nkb, stats_tile,
        (jnp.full((tq, LANES), NEG_INF, jnp.float32), jnp.full((tq, LANES), POS_INF, jnp.float32),
         jnp.zeros((tq, LANES), jnp.float32)))
    mx = jnp.max(mx, axis=1, keepdims=True)
    mn = jnp.min(mn, axis=1, keepdims=True)
    nf = jnp.sum(nf, axis=1, keepdims=True)

    def count(v, strict):
        def tile(j, cnt):
            off = pl.multiple_of(j * kb, kb)
            for l in range(lane_tiles):
                s = s_ref[:, pl.ds(off + l * LANES, LANES)]
                cnt = cnt + jnp.where(s > v if strict else s >= v, 1.0, 0.0)
            return cnt
        cnt = lax.fori_loop(0, nkb, tile, jnp.zeros((tq, LANES), jnp.float32))
        return jnp.sum(cnt, axis=1, keepdims=True)

    thr, taken, any_tie = _kth_threshold(count, mn, mx, nf, k_sel)
    thr_ref[...] = jnp.broadcast_to(thr, (tq, LANES))
    taken_ref[...] = jnp.broadcast_to(taken, (tq, LANES))
    tie_ref[b * pl.num_programs(1) + i] = jnp.where(any_tie, 1, 0).astype(jnp.int32)


def _index_scores(qir, wis, qc, kit, kc, nkb, *, row0, batches, nq, tq, kb, k_sel):
    sk = kit.shape[-1]
    rb0 = row0 // tq
    grid_spec = pltpu.PrefetchScalarGridSpec(
        num_scalar_prefetch=1,
        grid=(batches, nq),
        in_specs=[pl.BlockSpec((tq, IDX_HEADS * IDX_DIM), lambda b, i, n: (rb0 + b * nq + i, 0)),
                  pl.BlockSpec((tq, LANES), lambda b, i, n: (rb0 + b * nq + i, 0)),
                  pl.BlockSpec((tq, 1), lambda b, i, n: (rb0 + b * nq + i, 0)),
                  pl.BlockSpec((None, IDX_DIM, sk), lambda b, i, n: (b, 0, 0)),
                  pl.BlockSpec((None, 1, sk), lambda b, i, n: (b, 0, 0))],
        out_specs=[pl.BlockSpec((None, tq, sk), lambda b, i, n: (b, i, 0)),
                   pl.BlockSpec((None, tq, LANES), lambda b, i, n: (b, i, 0)),
                   pl.BlockSpec((None, tq, LANES), lambda b, i, n: (b, i, 0)),
                   pl.BlockSpec(memory_space=pltpu.SMEM)],
        scratch_shapes=[pltpu.VMEM((IDX_HEADS, tq, LANES), jnp.float32)],
    )
    return pl.pallas_call(
        functools.partial(_index_kernel, tq=tq, kb=kb, sk=sk, k_sel=k_sel),
        grid_spec=grid_spec,
        out_shape=[jax.ShapeDtypeStruct((batches, nq * tq, sk), jnp.float32),
                   jax.ShapeDtypeStruct((batches, nq * tq, LANES), jnp.float32),
                   jax.ShapeDtypeStruct((batches, nq * tq, LANES), jnp.float32),
                   jax.ShapeDtypeStruct((batches * nq,), jnp.int32)],
        compiler_params=_params(("arbitrary", "arbitrary")),
        name="index_select",
    )(nkb, qir, wis, qc, kit, kc)


def _fold8(x, op):
    parts = [x[g * 8:(g + 1) * 8, :] for g in range(x.shape[0] // 8)]
    while len(parts) > 1:
        parts = [op(parts[a], parts[a + 1]) for a in range(0, len(parts) - 1, 2)] + (parts[-1:] if len(parts) % 2 else [])
    return parts[0]


def _index_kernel_t(qit_ref, wit_ref, ki_ref, st_ref, thr_ref, taken_ref, tie_ref, *, tq, kb, k_sel):
    i = pl.program_id(0)
    nkb = lax.div((i + 1) * tq + (kb - 1), kb)
    total = st_ref.shape[0] // kb
    qchunk = lax.shift_right_logical(i * tq + lax.broadcasted_iota(jnp.int32, (kb, tq), 1), CHUNK_SHIFT)
    krow = lax.broadcasted_iota(jnp.int32, (kb, tq), 0)

    def score_tile(j, carry):
        mx, mn, nf = carry
        off = pl.multiple_of(j * kb, kb)
        k_t = ki_ref[pl.ds(off, kb), :]
        acc = jnp.zeros((kb, tq), jnp.float32)
        for h in range(IDX_HEADS):
            r = jnp.dot(k_t, qit_ref[h * IDX_DIM:(h + 1) * IDX_DIM, :], preferred_element_type=jnp.float32)
            acc = acc + jnp.maximum(r, 0.0) * wit_ref[h:h + 1, :]
        adm = lax.shift_right_logical(off + krow, CHUNK_SHIFT) <= qchunk
        s = jnp.where(adm, acc, NEG_INF)
        st_ref[pl.ds(off, kb), :] = s
        fin = s > NEG_INF
        mx = jnp.maximum(mx, _fold8(s, jnp.maximum))
        mn = jnp.minimum(mn, _fold8(jnp.where(fin, s, POS_INF), jnp.minimum))
        nf = nf + _fold8(jnp.where(fin, 1.0, 0.0), jnp.add)
        return mx, mn, nf

    mx, mn, nf = lax.fori_loop(
        0, nkb, score_tile,
        (jnp.full((8, tq), NEG_INF, jnp.float32), jnp.full((8, tq), POS_INF, jnp.float32),
         jnp.zeros((8, tq), jnp.float32)))

    def fill_tile(j, carry):
        off = pl.multiple_of(j * kb, kb)
        st_ref[pl.ds(off, kb), :] = jnp.full((kb, tq), NEG_INF, jnp.float32)
        return carry

    lax.fori_loop(nkb, total, fill_tile, 0)
    mx = jnp.max(mx, axis=0, keepdims=True)
    mn = jnp.min(mn, axis=0, keepdims=True)
    nf = jnp.sum(nf, axis=0, keepdims=True)

    def count(v, strict):
        def tile(j, cnt):
            off = pl.multiple_of(j * kb, kb)
            s = st_ref[pl.ds(off, kb), :]
            return cnt + _fold8(jnp.where(s > v if strict else s >= v, 1.0, 0.0), jnp.add)
        cnt = lax.fori_loop(0, nkb, tile, jnp.zeros((8, tq), jnp.float32))
        return jnp.sum(cnt, axis=0, keepdims=True)

    thr, taken, any_tie = _kth_threshold(count, mn, mx, nf, k_sel)
    thr_ref[...] = jnp.broadcast_to(thr, (8, tq))
    taken_ref[...] = jnp.broadcast_to(taken, (8, tq))
    tie_ref[i] = jnp.where(any_tie, 1, 0).astype(jnp.int32)


def _index_scores_t(qirt, wist, kirb, *, tq, kb, k_sel):
    idx_w, seq = qirt.shape
    nq = seq // tq
    return pl.pallas_call(
        functools.partial(_index_kernel_t, tq=tq, kb=kb, k_sel=k_sel),
        grid=(nq,),
        in_specs=[pl.BlockSpec((idx_w, tq), lambda i: (0, i)),
                  pl.BlockSpec((LANES, tq), lambda i: (0, i)),
                  pl.BlockSpec((seq, IDX_DIM), lambda i: (0, 0))],
        out_specs=[pl.BlockSpec((seq, tq), lambda i: (0, i)),
                   pl.BlockSpec((8, tq), lambda i: (0, i)),
                   pl.BlockSpec((8, tq), lambda i: (0, i)),
                   pl.BlockSpec(memory_space=pltpu.SMEM)],
        out_shape=[jax.ShapeDtypeStruct((seq, seq), jnp.float32),
                   jax.ShapeDtypeStruct((8, seq), jnp.float32),
                   jax.ShapeDtypeStruct((8, seq), jnp.float32),
                   jax.ShapeDtypeStruct((nq,), jnp.int32)],
        compiler_params=_params(("arbitrary",)),
        name="index_select_t",
    )(qirt, wist, kirb)


def _tie_rank_matrix(n, lower):
    rows = lax.broadcasted_iota(jnp.int32, (n, n), 0)
    cols = lax.broadcasted_iota(jnp.int32, (n, n), 1)
    return jnp.where(cols < rows if lower else rows < cols, 1.0, 0.0).astype(jnp.bfloat16)


def _attn_kernel_t(qi_ref, kj_ref, nk_ref, tie_ref, qt_ref, zb_ref, k_ref, vt_ref, st_ref, thr_ref, taken_ref,
                   qn_ref, kn_ref, o_ref, m_ref, c_ref, acc_ref, s_ref, p_ref, bias_ref, seen_ref, *, n_heads):
    p = pl.program_id(0)
    kj = kj_ref[p]
    hpg = n_heads // N_KV

    @pl.when(kj == 0)
    def _():
        m_ref[...] = jnp.full(m_ref.shape, NEG_INF, jnp.float32)
        c_ref[...] = jnp.zeros(c_ref.shape, jnp.float32)
        acc_ref[...] = jnp.zeros(acc_ref.shape, jnp.float32)
        seen_ref[...] = jnp.zeros(seen_ref.shape, jnp.float32)

    kb = st_ref.shape[0]
    rc = ATTN_ROW_CHUNK
    thr = thr_ref[0:1, :]
    for r in range(0, kb, rc):
        bias_ref[r:r + rc, :] = jnp.where(st_ref[r:r + rc, :] >= thr, 0.0, NEG_INF)

    @pl.when(tie_ref[qi_ref[p]] != 0)
    def _():
        taken = taken_ref[0:1, :]
        ltri = _tie_rank_matrix(RANK_CHUNK, True)
        seen = seen_ref[...]
        for r in range(0, kb, RANK_CHUNK):
            s = st_ref[r:r + RANK_CHUNK, :]
            eq = s == thr
            rank = jnp.dot(ltri, jnp.where(eq, 1.0, 0.0).astype(jnp.bfloat16),
                           preferred_element_type=jnp.float32) + seen
            take = jnp.logical_or(s > thr, jnp.logical_and(eq, rank < taken))
            bias_ref[r:r + RANK_CHUNK, :] = jnp.where(take, 0.0, NEG_INF)
            seen = seen + jnp.sum(jnp.where(eq, 1.0, 0.0), axis=0, keepdims=True)
        seen_ref[...] = seen

    def logits(h, r):
        g = h // hpg
        return jnp.dot(k_ref[r:r + rc, g * HEAD_DIM:(g + 1) * HEAD_DIM], qt_ref[h * HEAD_DIM:(h + 1) * HEAD_DIM, :],
                       preferred_element_type=jnp.float32) + bias_ref[r:r + rc, :]

    def weighted_values(h):
        g = h // hpg
        return jnp.dot(vt_ref[g * VT_ROWS:(g + 1) * VT_ROWS, :], p_ref[h], preferred_element_type=jnp.float32)

    first = kj == 0
    shift0 = [jnp.where(m_ref[h] == NEG_INF, 0.0, m_ref[h]) for h in range(n_heads)]
    kn = kn_ref[0]
    for blk in range(1, kn_ref.shape[0]):
        kn = jnp.maximum(kn, kn_ref[blk])
    excess = None
    for h in range(n_heads):
        bound = qn_ref[h:h + 1, :] * kn[0:1, h // hpg:h // hpg + 1]
        e = bound - shift0[h]
        excess = e if excess is None else jnp.maximum(excess, e)
    in_range = jnp.max(excess) <= EXP_RANGE

    @pl.when(in_range)
    def _():
        for h in range(n_heads):
            cm = None
            for r in range(0, kb, rc):
                t = logits(h, r) - shift0[h]
                p_ref[h, r:r + rc, :] = jnp.exp2(t).astype(p_ref.dtype)
                c8 = _fold8(t, jnp.maximum)
                cm = c8 if cm is None else jnp.maximum(cm, c8)
            m_ref[h] = jnp.maximum(m_ref[h], shift0[h] + jnp.max(cm, axis=0, keepdims=True))
        for h in range(n_heads):
            alpha = jnp.where(first, 1.0, jnp.exp2(c_ref[h] - shift0[h]))
            acc_ref[h] = alpha * acc_ref[h] + weighted_values(h)
            c_ref[h] = shift0[h]

    @pl.when(jnp.logical_not(in_range))
    def _():
        cmax = []
        for h in range(n_heads):
            cm = None
            for r in range(0, kb, rc):
                s = logits(h, r)
                s_ref[h, r:r + rc, :] = s
                c8 = _fold8(s, jnp.maximum)
                cm = c8 if cm is None else jnp.maximum(cm, c8)
            cmax.append(jnp.max(cm, axis=0, keepdims=True))
        for h in range(n_heads):
            m_cur = jnp.maximum(m_ref[h], cmax[h])
            shift = jnp.where(m_cur == NEG_INF, 0.0, m_cur)
            for r in range(0, kb, rc):
                p_ref[h, r:r + rc, :] = jnp.exp2(s_ref[h, r:r + rc, :] - shift).astype(p_ref.dtype)
            alpha = jnp.where(first, 1.0, jnp.exp2(c_ref[h] - shift))
            acc_ref[h] = alpha * acc_ref[h] + weighted_values(h)
            c_ref[h] = shift
            m_ref[h] = m_cur

    @pl.when(kj == nk_ref[p] - 1)
    def _():
        for h in range(n_heads):
            cs = slice(h * HEAD_DIM, (h + 1) * HEAD_DIM)
            o = (acc_ref[h, 0:HEAD_DIM, :] / acc_ref[h, HEAD_DIM:HEAD_DIM + 1, :]).T
            o_ref[:, cs] = (o * _silu(zb_ref[:, cs])).astype(o_ref.dtype)


def _masked_attention_t(qrt, h_main, zb_col, krb, vt, st, thr, taken, ties, qn, kn, steps, *, tq, kb):
    qidx, kidx, nk = steps
    d_b, seq = qrt.shape
    n_heads = d_b // HEAD_DIM
    kv_w = N_KV * HEAD_DIM
    kn_blocks = kb * kn.shape[0] // seq
    assert kb % RANK_CHUNK == 0 and kb % ATTN_ROW_CHUNK == 0 and kn_blocks >= 1
    grid_spec = pltpu.PrefetchScalarGridSpec(
        num_scalar_prefetch=4,
        grid=(qidx.shape[0],),
        in_specs=[pl.BlockSpec((d_b, tq), lambda p, q, k, n, t: (0, q[p])),
                  pl.BlockSpec((tq, d_b), lambda p, q, k, n, t: (q[p], zb_col)),
                  pl.BlockSpec((kb, kv_w), lambda p, q, k, n, t: (k[p], 0)),
                  pl.BlockSpec((N_KV * VT_ROWS, kb), lambda p, q, k, n, t: (0, k[p])),
                  pl.BlockSpec((kb, tq), lambda p, q, k, n, t: (k[p], q[p])),
                  pl.BlockSpec((8, tq), lambda p, q, k, n, t: (0, q[p])),
                  pl.BlockSpec((8, tq), lambda p, q, k, n, t: (0, q[p])),
                  pl.BlockSpec((n_heads, tq), lambda p, q, k, n, t: (0, q[p])),
                  pl.BlockSpec((kn_blocks, 8, LANES), lambda p, q, k, n, t: (k[p], 0, 0))],
        out_specs=pl.BlockSpec((tq, d_b), lambda p, q, k, n, t: (q[p], 0)),
        scratch_shapes=[pltpu.VMEM((n_heads, 1, tq), jnp.float32),
                        pltpu.VMEM((n_heads, 1, tq), jnp.float32),
                        pltpu.VMEM((n_heads, VT_ROWS, tq), jnp.float32),
                        pltpu.VMEM((n_heads, kb, tq), jnp.float32),
                        pltpu.VMEM((n_heads, kb, tq), jnp.bfloat16),
                        pltpu.VMEM((kb, tq), jnp.float32),
                        pltpu.VMEM((1, tq), jnp.float32)],
    )
    return pl.pallas_call(
        functools.partial(_attn_kernel_t, n_heads=n_heads),
        grid_spec=grid_spec,
        out_shape=jax.ShapeDtypeStruct((seq, d_b), jnp.bfloat16),
        compiler_params=_params(("arbitrary",)),
        name="masked_attention_t",
    )(qidx, kidx, nk, ties, qrt, h_main, krb, vt, st, thr, taken, qn, kn)


def _attn_kernel(tie_ref, q_ref, zb_ref, kt_ref, v_ref, s_ref, thr_ref, taken_ref, o_ref, bias_ref, *, n_heads):
    b = pl.program_id(0)
    hpg = n_heads // N_KV
    sk = s_ref.shape[1]
    thr = thr_ref[:, 0:1]
    bias_ref[...] = jnp.where(s_ref[...] >= thr, 0.0, NEG_INF)

    @pl.when(tie_ref[b] != 0)
    def _():
        taken = taken_ref[:, 0:1]
        utri = _tie_rank_matrix(RANK_CHUNK, False)
        seen = jnp.zeros_like(thr)
        for c in range(0, sk, RANK_CHUNK):
            s = s_ref[:, c:c + RANK_CHUNK]
            eq = s == thr
            rank = jnp.dot(jnp.where(eq, 1.0, 0.0).astype(jnp.bfloat16), utri,
                           preferred_element_type=jnp.float32) + seen
            take = jnp.logical_or(s > thr, jnp.logical_and(eq, rank < taken))
            bias_ref[:, c:c + RANK_CHUNK] = jnp.where(take, 0.0, NEG_INF)
            seen = seen + jnp.sum(jnp.where(eq, 1.0, 0.0), axis=1, keepdims=True)

    for h in range(n_heads):
        g = h // hpg
        cs = slice(h * HEAD_DIM, (h + 1) * HEAD_DIM)
        s = jnp.dot(q_ref[:, cs], kt_ref[g], preferred_element_type=jnp.float32) + bias_ref[...]
        pr = jnp.exp2(s - jnp.max(s, axis=1, keepdims=True))
        o = jnp.dot(pr.astype(v_ref.dtype), v_ref[:, g * HEAD_DIM:(g + 1) * HEAD_DIM],
                    preferred_element_type=jnp.float32) / jnp.sum(pr, axis=1, keepdims=True)
        o_ref[:, cs] = (o * _silu(zb_ref[:, cs])).astype(o_ref.dtype)


def _masked_attention(qr, h_main, zb_col, kt, vb, scores, thr, taken, ties, *, zrow0, tq, d_b):
    batches, _, _, sk = kt.shape
    zrb0 = zrow0 // tq
    n_heads = d_b // HEAD_DIM
    kv_w = N_KV * HEAD_DIM
    assert sk % RANK_CHUNK == 0
    grid_spec = pltpu.PrefetchScalarGridSpec(
        num_scalar_prefetch=1,
        grid=(batches,),
        in_specs=[pl.BlockSpec((tq, d_b), lambda b, t: (b, 0)),
                  pl.BlockSpec((tq, d_b), lambda b, t: (zrb0 + b, zb_col)),
                  pl.BlockSpec((None, N_KV, HEAD_DIM, sk), lambda b, t: (b, 0, 0, 0)),
                  pl.BlockSpec((None, sk, kv_w), lambda b, t: (b, 0, 0)),
                  pl.BlockSpec((None, tq, sk), lambda b, t: (b, 0, 0)),
                  pl.BlockSpec((None, tq, LANES), lambda b, t: (b, 0, 0)),
                  pl.BlockSpec((None, tq, LANES), lambda b, t: (b, 0, 0))],
        out_specs=pl.BlockSpec((tq, d_b), lambda b, t: (b, 0)),
        scratch_shapes=[pltpu.VMEM((tq, sk), jnp.float32)],
    )
    return pl.pallas_call(
        functools.partial(_attn_kernel, n_heads=n_heads),
        grid_spec=grid_spec,
        out_shape=jax.ShapeDtypeStruct((batches * tq, d_b), jnp.bfloat16),
        compiler_params=_params(("arbitrary",)),
        name="masked_attention",
    )(ties, qr, h_main, kt, vb, scores, thr, taken)


def _out_kernel(mixa_ref, mixb_ref, w_ref, x_ref, g_ref, o_ref, ssq_ref, *, d_a, tn):
    j = pl.program_id(1)

    @pl.when(j == 0)
    def _():
        ssq_ref[...] = jnp.zeros(ssq_ref.shape, jnp.float32)

    y = (x_ref[...] + jnp.dot(mixa_ref[...], w_ref[0:d_a, :], preferred_element_type=jnp.float32)
         + jnp.dot(mixb_ref[...], w_ref[d_a:, :], preferred_element_type=jnp.float32))
    o_ref[:, pl.ds(pl.multiple_of(j * tn, tn), tn)] = y
    ssq_ref[...] += jnp.sum(y * y, axis=-1, keepdims=True)

    @pl.when(j == pl.num_programs(1) - 1)
    def _():
        inv = lax.rsqrt(ssq_ref[...] * (1.0 / o_ref.shape[1]) + EPS)
        o_ref[...] = o_ref[...] * inv * g_ref[...]


def _out_proj(mix_a, mix_b, w, x, g):
    n, d = x.shape
    d_a, d_b = mix_a.shape[1], mix_b.shape[1]
    tm = _largest_divisor(n, (512, 256, 128))
    tn = _largest_divisor(d, (1024, 512, 256, 128))
    return pl.pallas_call(
        functools.partial(_out_kernel, d_a=d_a, tn=tn),
        grid=(n // tm, d // tn),
        in_specs=[pl.BlockSpec((tm, d_a), lambda i, j: (i, 0)),
                  pl.BlockSpec((tm, d_b), lambda i, j: (i, 0)),
                  pl.BlockSpec((d_a + d_b, tn), lambda i, j: (0, j)),
                  pl.BlockSpec((tm, tn), lambda i, j: (i, j)), pl.BlockSpec((1, d), lambda i, j: (0, 0))],
        out_specs=pl.BlockSpec((tm, d), lambda i, j: (i, 0)),
        out_shape=jax.ShapeDtypeStruct((n, d), jnp.float32),
        scratch_shapes=[pltpu.VMEM((tm, 1), jnp.float32)],
        compiler_params=_params(("arbitrary", "arbitrary")),
        name="out_proj",
    )(mix_a, mix_b, w, x, g.reshape(1, d))


def _cdiv(a, b):
    return -(-a // b)


def _attention_steps(nq, tq, kb):
    q_l, k_l, n_l = [], [], []
    for q in range(nq):
        nk = _cdiv((q + 1) * tq, kb)
        for k in range(nk):
            q_l.append(q), k_l.append(k), n_l.append(nk)
    return tuple(jnp.asarray(np.asarray(v, np.int32)) for v in (q_l, k_l, n_l))


def _layer(x_prompt, x_sample, cache_k, cache_v, cache_ki, g_norm, w_in, w_s, b_s, g_v, w_out, final_g):
    f32, bf = jnp.float32, jnp.bfloat16
    _, seq, d_model = x_prompt.shape
    dec_b, dec_t, _ = x_sample.shape
    past = cache_k.shape[1]
    d_a = d_model // 2
    d_b = d_model - d_a
    groups = d_a // HEAD_DIM
    kv_w = N_KV * HEAD_DIM
    idx_w = IDX_HEADS * IDX_DIM
    n_s = dec_b * dec_t
    n_all = seq + n_s
    assert seq % 512 == 0 and n_s % MLP_CHUNK == 0 and dec_t <= CHUNK and past % MLP_CHUNK == 0
    assert idx_w % d_a == 0 and d_a % HEAD_DIM == 0

    x_p, x_s = x_prompt[0], x_sample.reshape(n_s, d_model)
    widths = (d_a, d_a, d_a, d_b, kv_w, kv_w, d_b, idx_w, IDX_DIM, IDX_HEADS)
    o_u, o_va, o_za, o_q, o_k, o_v, o_zb, o_qi, o_ki, o_wi = np.concatenate([[0], np.cumsum(widths)])[:-1].tolist()

    def col_blocks(segments, bw):
        return [(off + c) // bw for off, wd in segments for c in range(0, wd, bw)]

    w_in_t = w_in.T
    bw = int(np.gcd.reduce([d_a, d_b, kv_w, idx_w, o_q, o_zb, o_qi]))
    w_main = _regroup_cast(w_in_t, col_blocks(((o_qi, idx_w), (o_u, d_a), (o_va, d_a), (o_za, d_a), (o_q, d_b),
                                               (o_zb, d_b)), bw), bw, bf)
    assert o_k % LANES == 0 and o_ki % LANES == 0 and o_wi % LANES == 0
    w_small = _regroup_cast(w_in_t, col_blocks(((o_k, 2 * kv_w), (o_ki, IDX_DIM), (o_wi, LANES)), LANES), LANES, bf,
                            tail_valid=IDX_HEADS)
    u_col = idx_w // d_a
    q_col, zb_col = u_col + 3, u_col + 4

    half = HEAD_DIM // 2
    inv = ROPE_THETA ** (-2.0 * jnp.arange(half, dtype=f32) / HEAD_DIM)

    def rope_tables(pos):
        ang = pos.astype(f32)[:, None] * inv[None, :]
        return (jnp.concatenate([jnp.cos(ang), jnp.cos(ang)], axis=1),
                jnp.concatenate([-jnp.sin(ang), jnp.sin(ang)], axis=1))

    pos_s = past + jnp.tile(jnp.arange(dec_t), dec_b)
    cos_p, sin_p = rope_tables(jnp.arange(seq))
    cos_s, sin_s = rope_tables(pos_s)

    hm_p, hs_p = _in_proj(x_p, g_norm, w_main, f32), _in_proj(x_p, g_norm, w_small, f32)
    hm_s, hs_s = _in_proj(x_s, g_norm, w_main, f32), _in_proj(x_s, g_norm, w_small, f32)

    pidx = np.arange(MLP_CHUNK)
    mask_p = (pidx[None, :] // CHUNK) <= (pidx[:, None] // CHUNK)
    wm_p = jnp.where(mask_p[None], w_s, 0.0).astype(bf)
    b_p = b_s[:, :, None]
    tidx = pidx % dec_t
    same = (pidx[None, :] // dec_t) == (pidx[:, None] // dec_t)
    mask_s = same & ((tidx[None, :] // CHUNK) <= (tidx[:, None] // CHUNK))
    wm_s = jnp.where(mask_s[None], w_s[:, tidx][:, :, tidx], 0.0).astype(bf)
    b_sm = b_s[:, tidx][:, :, None]
    gv = g_v.reshape(1, d_a)
    (mix_a_p,) = _mlp_group(hm_p, u_col, d_a, 0, seq, 512, wm_p, b_p, gv, False)
    mix_a_s, vn_s = _mlp_group(hm_s, u_col, d_a, 0, n_s, MLP_CHUNK, wm_s, b_sm, gv, True)

    qirt, qrt, kr_p, krb_p, vt_p, kir_p, kirb_p, wist, qn_p, kn_p = _prep(
        hm_p, hs_p, cos_p, sin_p, d_b, q_col, row0=0, nrows=seq, transposed=True)
    qir_s, qr_s, kr_s, krb_s, vb_s, kir_s, kirb_s, wis_s = _prep(
        hm_s, hs_s, cos_s, sin_s, d_b, q_col, row0=0, nrows=n_s, transposed=False)
    v_p, v_s_new = hs_p[:, kv_w:2 * kv_w], hs_s[:, kv_w:2 * kv_w]

    tq_p, kb_i, kb_a = PROMPT_TQ, PROMPT_KB_INDEX, PROMPT_KB_ATTN
    k_sel_p = min(TOPK_MAX, seq // 4)
    st_p, thr_p, taken_p, ties_p = _index_scores_t(qirt, wist, kirb_p, tq=tq_p, kb=kb_i, k_sel=k_sel_p)
    steps_p = _attention_steps(seq // tq_p, tq_p, kb_a)
    mix_b_p = _masked_attention_t(qrt, hm_p, zb_col, krb_p, vt_p, st_p, thr_p, taken_p, ties_p, qn_p, kn_p,
                                  steps_p, tq=tq_p, kb=kb_a)

    kb_s = SAMPLE_KB_INDEX
    n_keys = past + dec_t
    sk_s = _cdiv(n_keys, kb_s) * kb_s
    padk = lambda a: jnp.pad(a, ((0, 0), (0, sk_s - n_keys), (0, 0)))
    k_s = padk(jnp.concatenate([cache_k.reshape(dec_b, past, kv_w).astype(bf), krb_s.reshape(dec_b, dec_t, kv_w)], axis=1))
    v_s = padk(jnp.concatenate([cache_v.reshape(dec_b, past, kv_w).astype(bf), vb_s.reshape(dec_b, dec_t, kv_w)], axis=1))
    ki_s = padk(jnp.concatenate([cache_ki.astype(bf), kirb_s.reshape(dec_b, dec_t, IDX_DIM)], axis=1))
    kit_s = ki_s.transpose(0, 2, 1)
    kt_s = k_s.reshape(dec_b, sk_s, N_KV, HEAD_DIM).transpose(0, 2, 3, 1)
    kpos = jnp.arange(sk_s, dtype=jnp.int32)
    kc_s = jnp.broadcast_to(jnp.where(kpos < n_keys, kpos // CHUNK, FAR_CHUNK)[None, None, :], (dec_b, 1, sk_s))
    qc_s = (pos_s // CHUNK).astype(jnp.int32)[:, None]
    k_sel_s = min(TOPK_MAX, n_keys // 4)
    nkb_s = jnp.full((dec_b,), sk_s // kb_s, jnp.int32)
    s_s, thr_s, taken_s, ties_s = _index_scores(qir_s, wis_s, qc_s, kit_s, kc_s, nkb_s, row0=0, batches=dec_b, nq=1,
                                                tq=dec_t, kb=kb_s, k_sel=k_sel_s)
    mix_b_s = _masked_attention(qr_s, hm_s, zb_col, kt_s, v_s, s_s, thr_s, taken_s, ties_s,
                                zrow0=0, tq=dec_t, d_b=d_b)

    w_out_b = w_out.astype(bf)
    y_p = _out_proj(mix_a_p, mix_b_p, w_out_b, x_p, final_g)
    y_s = _out_proj(mix_a_s, mix_b_s, w_out_b, x_s, final_g)

    shp_p = (1, 1, seq, N_KV, HEAD_DIM)
    shp_s = (1, dec_b, dec_t, N_KV, HEAD_DIM)
    return (y_p[None], y_s.reshape(dec_b, dec_t, d_model),
            kr_p.reshape(shp_p), v_p.reshape(shp_p), kir_p.reshape(1, 1, seq, IDX_DIM),
            kr_s.reshape(shp_s), v_s_new.reshape(shp_s), kir_s.reshape(1, dec_b, dec_t, IDX_DIM),
            vn_s.reshape(1, dec_b, dec_t, groups, HEAD_DIM))


def kernel(x_prompt, x_sample, cache_k, cache_v, cache_idx_k, norm_g, w_in, w_s, b_s, v_norm_g, w_out, final_norm_g):
    assert x_prompt.shape[0] == 1 and norm_g.shape[0] == 1, "one prompt stream and one layer"
    return _layer(x_prompt, x_sample, cache_k[0], cache_v[0], cache_idx_k[0], norm_g[0], w_in[0], w_s[0], b_s[0],
                  v_norm_g[0], w_out[0], final_norm_g)
```

```python
import functools

import numpy as np
import jax
import jax.numpy as jnp
from jax import lax
from jax.experimental import pallas as pl
from jax.experimental.pallas import tpu as pltpu

CHUNK = 64
MLP_CHUNK = 128
HEAD_DIM = 128
N_KV = 4
IDX_HEADS = 32
IDX_DIM = 128
TOPK_MAX = 256
ROPE_THETA = 10000.0
EPS = 1e-6

LANES = 128
VMEM_LIMIT = 56 * 1024 * 1024
NEG_INF = float("-inf")
POS_INF = float("inf")
FAR_CHUNK = 1 << 20
LOG2_E = 1.4426950408889634
CHUNK_SHIFT = 6
PROMPT_TQ, PROMPT_KB_INDEX, PROMPT_KB_ATTN = 256, 256, 1024
ONES_ROWS = 16
VT_ROWS = HEAD_DIM + ONES_ROWS
ATTN_ROW_CHUNK = 128
TAKE_ALL = 1.0e9
RANK_CHUNK = 256
EXP_RANGE = 60.0
NORM_MARGIN = 1.001
SAMPLE_KB_INDEX = 256


def _largest_divisor(n, candidates):
    for c in candidates:
        if n % c == 0:
            return c
    raise ValueError(f"no tile in {candidates} divides {n}")


def _params(sem, flags=None):
    return pltpu.CompilerParams(dimension_semantics=sem, vmem_limit_bytes=VMEM_LIMIT, flags=flags)


def _cast_kernel(src_ref, w_ref, o_ref, *, bw, tail_valid):
    w = w_ref[...]
    if tail_valid < bw:
        row = lax.broadcasted_iota(jnp.int32, w.shape, 0)
        last = pl.program_id(0) == pl.num_programs(0) - 1
        w = jnp.where(jnp.logical_and(last, row >= tail_valid), 0.0, w)
    o_ref[...] = w.astype(o_ref.dtype)


def _regroup_cast(wt, src_blocks, bw, out_dtype, tail_valid=None):
    d = wt.shape[1]
    tail_valid = bw if tail_valid is None else tail_valid
    src = jnp.asarray(np.asarray(src_blocks, np.int32))
    grid_spec = pltpu.PrefetchScalarGridSpec(
        num_scalar_prefetch=1,
        grid=(len(src_blocks),),
        in_specs=[pl.BlockSpec((bw, d), lambda j, s: (s[j], 0))],
        out_specs=pl.BlockSpec((bw, d), lambda j, s: (j, 0)),
    )
    return pl.pallas_call(
        functools.partial(_cast_kernel, bw=bw, tail_valid=tail_valid),
        grid_spec=grid_spec,
        out_shape=jax.ShapeDtypeStruct((len(src_blocks) * bw, d), out_dtype),
        compiler_params=_params(("arbitrary",)),
        name="regroup_cast",
    )(src, wt)


def _in_proj_kernel(x_ref, g_ref, wt_ref, o_ref, xn_ref):
    @pl.when(pl.program_id(1) == 0)
    def _():
        x = x_ref[...]
        ms = jnp.mean(x * x, axis=-1, keepdims=True)
        xn_ref[...] = (x * lax.rsqrt(ms + EPS) * g_ref[...]).astype(xn_ref.dtype)

    o_ref[...] = lax.dot_general(xn_ref[...], wt_ref[...], (((1,), (1,)), ((), ())),
                                 preferred_element_type=jnp.float32).astype(o_ref.dtype)


def _in_proj(x, g, wt, out_dtype, row0=0, emit_xn=False):
    m, k = x.shape
    n = wt.shape[0] - row0
    tm = _largest_divisor(m, (512, 256, 128))
    tn = _largest_divisor(int(np.gcd(n, row0)), (1024, 1280, 512, 256, 128))
    out_specs = [pl.BlockSpec((tm, tn), lambda i, j: (i, j))]
    out_shape = [jax.ShapeDtypeStruct((m, n), out_dtype)]
    scratch = [pltpu.VMEM((tm, k), wt.dtype)]
    if emit_xn:
        out_specs.append(pl.BlockSpec((tm, k), lambda i, j: (i, 0)))
        out_shape.append(jax.ShapeDtypeStruct((m, k), wt.dtype))
        scratch = []
    res = pl.pallas_call(
        _in_proj_kernel,
        grid=(m // tm, n // tn),
        in_specs=[pl.BlockSpec((tm, k), lambda i, j: (i, 0)), pl.BlockSpec((1, k), lambda i, j: (0, 0)),
                  pl.BlockSpec((tn, k), lambda i, j: (row0 // tn + j, 0))],
        out_specs=out_specs,
        out_shape=out_shape,
        scratch_shapes=scratch,
        compiler_params=_params(("arbitrary", "arbitrary")),
        name="in_proj",
    )(x, g.reshape(1, k), wt)
    return res if emit_xn else res[0]


def _silu(z):
    return z * (1.0 / (1.0 + jnp.exp(-z)))


def _mlp_kernel(u_ref, va_ref, za_ref, wm_ref, b_ref, gv_ref, mix_ref, *vn_refs, rows, groups):
    for c in range(rows // MLP_CHUNK):
        rs = slice(c * MLP_CHUNK, (c + 1) * MLP_CHUNK)
        for g in range(groups):
            cs = slice(g * HEAD_DIM, (g + 1) * HEAD_DIM)
            va = va_ref[rs, cs]
            mu = jnp.mean(va, axis=-1, keepdims=True)
            cen = va - mu
            var = jnp.mean(cen * cen, axis=-1, keepdims=True)
            vn = cen * lax.rsqrt(var + EPS) * gv_ref[:, cs]
            if vn_refs:
                vn_refs[0][rs, cs] = vn
            mixed = jnp.dot(wm_ref[g], vn.astype(jnp.bfloat16), preferred_element_type=jnp.float32) + b_ref[g]
            a = u_ref[rs, cs] * mixed
            mix_ref[rs, cs] = (a * _silu(za_ref[rs, cs])).astype(mix_ref.dtype)


def _mlp_group(h_main, col0, d_a, row0, nrows, rows, wm, bias, gv, emit_vn):
    groups = d_a // HEAD_DIM
    rb0 = row0 // rows
    out_shape = [jax.ShapeDtypeStruct((nrows, d_a), jnp.bfloat16)]
    out_specs = [pl.BlockSpec((rows, d_a), lambda i: (i, 0))]
    if emit_vn:
        out_shape.append(jax.ShapeDtypeStruct((nrows, d_a), jnp.float32))
        out_specs.append(pl.BlockSpec((rows, d_a), lambda i: (i, 0)))
    col = lambda c: pl.BlockSpec((rows, d_a), lambda i: (rb0 + i, c))
    return pl.pallas_call(
        functools.partial(_mlp_kernel, rows=rows, groups=groups),
        grid=(nrows // rows,),
        in_specs=[col(col0), col(col0 + 1), col(col0 + 2),
                  pl.BlockSpec((groups, MLP_CHUNK, MLP_CHUNK), lambda i: (0, 0, 0)),
                  pl.BlockSpec((groups, MLP_CHUNK, 1), lambda i: (0, 0, 0)),
                  pl.BlockSpec((1, d_a), lambda i: (0, 0))],
        out_specs=out_specs,
        out_shape=out_shape,
        compiler_params=_params(("arbitrary",)),
        name="chunk_mlp",
    )(h_main, h_main, h_main, wm, bias, gv)


def _rope(x, cos, sin):
    return x * cos + pltpu.roll(x, HEAD_DIM // 2, 1) * sin


def _qproj_kernel(xn_ref, wt_ref, cos_ref, sin_ref, qt_ref, *qn_refs, scale):
    acc = lax.dot_general(xn_ref[...], wt_ref[...], (((1,), (1,)), ((), ())), preferred_element_type=jnp.float32)
    cos, sin = cos_ref[...], sin_ref[...]
    for h in range(acc.shape[1] // HEAD_DIM):
        cs = slice(h * HEAD_DIM, (h + 1) * HEAD_DIM)
        qb = (_rope(acc[:, cs], cos, sin) * scale).T.astype(qt_ref.dtype)
        qt_ref[cs, :] = qb
        if qn_refs:
            qf = qb.astype(jnp.float32)
            qn_refs[0][h:h + 1, :] = jnp.sqrt(jnp.sum(qf * qf, axis=0, keepdims=True)) * NORM_MARGIN


def _qproj(xn, wt, row0, nrows_w, cos_t, sin_t, *, scale, with_norms):
    m, k = xn.shape
    tm = _largest_divisor(m, (512, 256, 128))
    tn = _largest_divisor(int(np.gcd(nrows_w, row0)) if row0 else nrows_w, (1024, 512, 256, 128))
    hpt = tn // HEAD_DIM
    out_specs = [pl.BlockSpec((tn, tm), lambda i, j: (j, i))]
    out_shape = [jax.ShapeDtypeStruct((nrows_w, m), jnp.bfloat16)]
    if with_norms:
        out_specs.append(pl.BlockSpec((hpt, tm), lambda i, j: (j, i)))
        out_shape.append(jax.ShapeDtypeStruct((nrows_w // HEAD_DIM, m), jnp.float32))
    return pl.pallas_call(
        functools.partial(_qproj_kernel, scale=scale),
        grid=(m // tm, nrows_w // tn),
        in_specs=[pl.BlockSpec((tm, k), lambda i, j: (i, 0)),
                  pl.BlockSpec((tn, k), lambda i, j: (row0 // tn + j, 0)),
                  pl.BlockSpec((tm, HEAD_DIM), lambda i, j: (i, 0)), pl.BlockSpec((tm, HEAD_DIM), lambda i, j: (i, 0))],
        out_specs=out_specs,
        out_shape=out_shape,
        compiler_params=_params(("arbitrary", "arbitrary")),
        name="q_proj",
    )(xn, wt, cos_t, sin_t)


def _keyprep_kernel(k_ref, v_ref, ki_ref, wi_ref, cos_ref, sin_ref,
                    kr_ref, krb_ref, vt_ref, kir_ref, kirb_ref, wist_ref, kn_ref):
    cos, sin = cos_ref[...], sin_ref[...]
    lane = lax.broadcasted_iota(jnp.int32, (8, LANES), 1)
    kn = jnp.zeros((8, LANES), jnp.float32)
    for h in range(N_KV):
        cs = slice(h * HEAD_DIM, (h + 1) * HEAD_DIM)
        kr = _rope(k_ref[:, cs], cos, sin)
        kr_ref[:, cs] = kr
        krb_ref[:, cs] = kr.astype(krb_ref.dtype)
        vt_ref[h * VT_ROWS:h * VT_ROWS + HEAD_DIM, :] = v_ref[:, cs].T.astype(vt_ref.dtype)
        vt_ref[h * VT_ROWS + HEAD_DIM:(h + 1) * VT_ROWS, :] = jnp.ones((ONES_ROWS, v_ref.shape[0]), vt_ref.dtype)
        kb16 = kr.astype(krb_ref.dtype).astype(jnp.float32)
        kmax = jnp.sqrt(jnp.max(jnp.sum(kb16 * kb16, axis=1, keepdims=True), axis=0, keepdims=True))
        kn = jnp.where(lane == h, kmax, kn)
    kn_ref[0] = kn
    kir = _rope(ki_ref[...], cos, sin)
    kir_ref[...] = kir
    kirb_ref[...] = kir.astype(kirb_ref.dtype)
    wist_ref[...] = (wi_ref[...] * ((IDX_HEADS * IDX_DIM) ** -0.5)).T


def _keyprep(h_small, cos_t, sin_t):
    nrows = h_small.shape[0]
    rows = _largest_divisor(nrows, (256, 128))
    kv_w = N_KV * HEAD_DIM
    inspec = lambda w, c: pl.BlockSpec((rows, w), lambda i: (i, c))
    rowspec = lambda w: pl.BlockSpec((rows, w), lambda i: (i, 0))
    colspec = lambda w: pl.BlockSpec((w, rows), lambda i: (0, i))
    bf, f32, sds = jnp.bfloat16, jnp.float32, jax.ShapeDtypeStruct
    return pl.pallas_call(
        _keyprep_kernel,
        grid=(nrows // rows,),
        in_specs=[inspec(kv_w, 0), inspec(kv_w, 1), inspec(IDX_DIM, 2 * kv_w // IDX_DIM),
                  inspec(LANES, 2 * kv_w // LANES + 1), inspec(HEAD_DIM, 0), inspec(HEAD_DIM, 0)],
        out_specs=[rowspec(kv_w), rowspec(kv_w), colspec(N_KV * VT_ROWS), rowspec(IDX_DIM), rowspec(IDX_DIM),
                   colspec(LANES), pl.BlockSpec((1, 8, LANES), lambda i: (i, 0, 0))],
        out_shape=[sds((nrows, kv_w), f32), sds((nrows, kv_w), bf), sds((N_KV * VT_ROWS, nrows), bf),
                   sds((nrows, IDX_DIM), f32), sds((nrows, IDX_DIM), bf), sds((LANES, nrows), f32),
                   sds((nrows // rows, 8, LANES), f32)],
        compiler_params=_params(("arbitrary",)),
        name="key_prep",
    )(h_small, h_small, h_small, h_small, cos_t, sin_t)


def _prep_kernel(qi_ref, q_ref, k_ref, v_ref, ki_ref, wi_ref, cos_ref, sin_ref,
                 qir_ref, qr_ref, kr_ref, krb_ref, vb_ref, kir_ref, kirb_ref, wis_ref, *norm_refs,
                 n_heads, transposed):
    cos = cos_ref[...]
    sin = sin_ref[...]

    def put(ref, h, width, val):
        if transposed:
            ref[h * width:(h + 1) * width, :] = val.T.astype(ref.dtype)
        else:
            ref[:, h * width:(h + 1) * width] = val.astype(ref.dtype)

    for h in range(IDX_HEADS):
        put(qir_ref, h, IDX_DIM, _rope(qi_ref[:, h * IDX_DIM:(h + 1) * IDX_DIM], cos, sin))
    q_scale = HEAD_DIM ** -0.5 * LOG2_E
    for h in range(n_heads):
        qs = _rope(q_ref[:, h * HEAD_DIM:(h + 1) * HEAD_DIM], cos, sin) * q_scale
        put(qr_ref, h, HEAD_DIM, qs)
        if transposed:
            qb = qs.T.astype(qr_ref.dtype).astype(jnp.float32)
            norm_refs[0][h:h + 1, :] = jnp.sqrt(jnp.sum(qb * qb, axis=0, keepdims=True)) * NORM_MARGIN
    lane = lax.broadcasted_iota(jnp.int32, (8, LANES), 1)
    kn = jnp.zeros((8, LANES), jnp.float32)
    for h in range(N_KV):
        cs = slice(h * HEAD_DIM, (h + 1) * HEAD_DIM)
        kr = _rope(k_ref[:, cs], cos, sin)
        kr_ref[:, cs] = kr
        krb_ref[:, cs] = kr.astype(krb_ref.dtype)
        if transposed:
            vb_ref[h * VT_ROWS:h * VT_ROWS + HEAD_DIM, :] = v_ref[:, cs].T.astype(vb_ref.dtype)
            vb_ref[h * VT_ROWS + HEAD_DIM:(h + 1) * VT_ROWS, :] = jnp.ones((ONES_ROWS, v_ref.shape[0]), vb_ref.dtype)
            kb16 = kr.astype(krb_ref.dtype).astype(jnp.float32)
            kmax = jnp.sqrt(jnp.max(jnp.sum(kb16 * kb16, axis=1, keepdims=True), axis=0, keepdims=True))
            kn = jnp.where(lane == h, kmax, kn)
        else:
            vb_ref[:, cs] = v_ref[:, cs].astype(vb_ref.dtype)
    if transposed:
        norm_refs[1][0] = kn
    kir = _rope(ki_ref[...], cos, sin)
    kir_ref[...] = kir
    kirb_ref[...] = kir.astype(kirb_ref.dtype)
    put(wis_ref, 0, LANES, wi_ref[...] * ((IDX_HEADS * IDX_DIM) ** -0.5))


def _prep(h_main, h_small, cos_t, sin_t, d_b, q_col, *, row0, nrows, transposed):
    rows = _largest_divisor(nrows, (256, 128))
    rb0 = row0 // rows
    n_heads = d_b // HEAD_DIM
    idx_w = IDX_HEADS * IDX_DIM
    kv_w = N_KV * HEAD_DIM
    inspec = lambda w, c: pl.BlockSpec((rows, w), lambda i: (rb0 + i, c))
    bf, f32 = jnp.bfloat16, jnp.float32
    outs = [(idx_w, bf, True), (d_b, bf, True), (kv_w, f32, False), (kv_w, bf, False), (kv_w, bf, True),
            (IDX_DIM, f32, False), (IDX_DIM, bf, False), (LANES, f32, True)]
    out_specs, out_shape = [], []
    for idx, (w, dt, feature_major) in enumerate(outs):
        if transposed and feature_major:
            w = N_KV * VT_ROWS if idx == 4 else w
            out_specs.append(pl.BlockSpec((w, rows), lambda i: (0, i)))
            out_shape.append(jax.ShapeDtypeStruct((w, nrows), dt))
        else:
            out_specs.append(pl.BlockSpec((rows, w), lambda i: (i, 0)))
            out_shape.append(jax.ShapeDtypeStruct((nrows, w), dt))
    if transposed:
        out_specs += [pl.BlockSpec((n_heads, rows), lambda i: (0, i)), pl.BlockSpec((1, 8, LANES), lambda i: (i, 0, 0))]
        out_shape += [jax.ShapeDtypeStruct((n_heads, nrows), f32), jax.ShapeDtypeStruct((nrows // rows, 8, LANES), f32)]
    return pl.pallas_call(
        functools.partial(_prep_kernel, n_heads=n_heads, transposed=transposed),
        grid=(nrows // rows,),
        in_specs=[inspec(idx_w, 0), inspec(d_b, q_col),
                  inspec(kv_w, 0), inspec(kv_w, 1), inspec(IDX_DIM, 2 * kv_w // IDX_DIM),
                  inspec(LANES, 2 * kv_w // LANES + 1), inspec(HEAD_DIM, 0), inspec(HEAD_DIM, 0)],
        out_specs=out_specs,
        out_shape=out_shape,
        compiler_params=_params(("arbitrary",)),
        name="rope_prep",
    )(h_main, h_main, h_small, h_small, h_small, h_small, cos_t, sin_t)


def _bisect_start(mn, mx, nf, k_sel):
    return mn, mx, mn, jnp.where(nf <= float(k_sel), 1.0, 0.0), jnp.zeros_like(mn)


def _bisect_step(state, mid, cnt, kk):
    lo, hi, thr, done, tie = state
    active = done < 0.5
    hit = jnp.logical_and(active, cnt == kk)
    no_room = jnp.logical_or(mid <= lo, mid >= hi)
    stuck = jnp.logical_and(active, jnp.logical_and(no_room, cnt != kk))
    thr = jnp.where(hit, mid, thr)
    tie = jnp.where(stuck, 1.0, tie)
    lo = jnp.where(jnp.logical_and(active, cnt > kk), mid, lo)
    hi = jnp.where(jnp.logical_and(active, cnt < kk), mid, hi)
    done = jnp.where(jnp.logical_or(hit, stuck), 1.0, done)
    return lo, hi, thr, done, tie


def _kth_threshold(count, mn, mx, nf, k_sel, start=None):
    kk = jnp.minimum(nf, float(k_sel))

    def cond(st):
        return st[5] > 0.5

    def body(st):
        lo, hi = st[0], st[1]
        mid = 0.5 * lo + 0.5 * hi
        new = _bisect_step(st[:5], mid, count(mid, False), kk)
        return new + (jnp.sum(1.0 - new[3]),)

    start = _bisect_start(mn, mx, nf, k_sel) if start is None else start
    lo, hi, thr, _, tie, _ = lax.while_loop(cond, body, tuple(start) + (jnp.sum(1.0 - start[3]),))
    any_tie = jnp.sum(tie) > 0.5

    def resolve(_):
        is_tie = tie > 0.5
        kth = jnp.where(count(hi, False) >= kk, hi, lo)
        thr_t = jnp.where(is_tie, kth, thr)
        taken = jnp.where(is_tie, kk - count(thr_t, True), TAKE_ALL)
        return thr_t, taken

    thr, taken = lax.cond(any_tie, resolve, lambda _: (thr, jnp.full_like(thr, TAKE_ALL)), 0)
    return thr, taken, any_tie


def _index_kernel(nkb_ref, qi_ref, wi_ref, qc_ref, kit_ref, kc_ref, s_ref, thr_ref, taken_ref, tie_ref, wb_ref,
                  *, tq, kb, sk, k_sel):
    b = pl.program_id(0)
    i = pl.program_id(1)
    nkb = nkb_ref[b * pl.num_programs(1) + i]
    lane_tiles = kb // LANES
    rt = min(tq, 128)

    for h in range(IDX_HEADS):
        wb_ref[h] = jnp.broadcast_to(wi_ref[:, h:h + 1], (tq, LANES))

    def score_tile(j, carry):
        off = pl.multiple_of(j * kb, kb)
        kt = kit_ref[:, pl.ds(off, kb)]
        kc = kc_ref[:, pl.ds(off, kb)]
        for r0 in range(0, tq, rt):
            accs = [jnp.zeros((rt, LANES), jnp.float32) for _ in range(lane_tiles)]
            for h in range(IDX_HEADS):
                r = jnp.dot(qi_ref[r0:r0 + rt, h * IDX_DIM:(h + 1) * IDX_DIM], kt,
                            preferred_element_type=jnp.float32)
                w = wb_ref[h, r0:r0 + rt, :]
                for l in range(lane_tiles):
                    accs[l] = accs[l] + jnp.maximum(r[:, l * LANES:(l + 1) * LANES], 0.0) * w
            acc = jnp.concatenate(accs, axis=1)
            adm = kc <= qc_ref[r0:r0 + rt, :]
            s_ref[r0:r0 + rt, pl.ds(off, kb)] = jnp.where(adm, acc, NEG_INF)
        return carry

    lax.fori_loop(0, nkb, score_tile, 0)

    def fill_tile(j, carry):
        off = pl.multiple_of(j * kb, kb)
        s_ref[:, pl.ds(off, kb)] = jnp.full((tq, kb), NEG_INF, jnp.float32)
        return carry

    lax.fori_loop(nkb, sk // kb, fill_tile, 0)

    def stats_tile(j, carry):
        mx, mn, nf = carry
        off = pl.multiple_of(j * kb, kb)
        for l in range(lane_tiles):
            s = s_ref[:, pl.ds(off + l * LANES, LANES)]
            fin = s > NEG_INF
            mx = jnp.maximum(mx, s)
            mn = jnp.minimum(mn, jnp.where(fin, s, POS_INF))
            nf = nf + jnp.where(fin, 1.0, 0.0)
        return mx, mn, nf

    mx, mn, nf = lax.fori_loop(
        0, nkb, stats_tile,
        (jnp.full((tq, LANES), NEG_INF, jnp.float32), jnp.full((tq, LANES), POS_INF, jnp.float32),
         jnp.zeros((tq, LANES), jnp.float32)))
    mx = jnp.max(mx, axis=1, keepdims=True)
    mn = jnp.min(mn, axis=1, keepdims=True)
    nf = jnp.sum(nf, axis=1, keepdims=True)

    def count(v, strict):
        def tile(j, cnt):
            off = pl.multiple_of(j * kb, kb)
            for l in range(lane_tiles):
                s = s_ref[:, pl.ds(off + l * LANES, LANES)]
                cnt = cnt + jnp.where(s > v if strict else s >= v, 1.0, 0.0)
            return cnt
        cnt = lax.fori_loop(0, nkb, tile, jnp.zeros((tq, LANES), jnp.float32))
        return jnp.sum(cnt, axis=1, keepdims=True)

    thr, taken, any_tie = _kth_threshold(count, mn, mx, nf, k_sel)
    thr_ref[...] = jnp.broadcast_to(thr, (tq, LANES))
    taken_ref[...] = jnp.broadcast_to(taken, (tq, LANES))
    tie_ref[b * pl.num_programs(1) + i] = jnp.where(any_tie, 1, 0).astype(jnp.int32)


def _index_scores(qir, wis, qc, kit, kc, nkb, *, row0, batches, nq, tq, kb, k_sel):
    sk = kit.shape[-1]
    rb0 = row0 // tq
    grid_spec = pltpu.PrefetchScalarGridSpec(
        num_scalar_prefetch=1,
        grid=(batches, nq),
        in_specs=[pl.BlockSpec((tq, IDX_HEADS * IDX_DIM), lambda b, i, n: (rb0 + b * nq + i, 0)),
                  pl.BlockSpec((tq, LANES), lambda b, i, n: (rb0 + b * nq + i, 0)),
                  pl.BlockSpec((tq, 1), lambda b, i, n: (rb0 + b * nq + i, 0)),
                  pl.BlockSpec((None, IDX_DIM, sk), lambda b, i, n: (b, 0, 0)),
                  pl.BlockSpec((None, 1, sk), lambda b, i, n: (b, 0, 0))],
        out_specs=[pl.BlockSpec((None, tq, sk), lambda b, i, n: (b, i, 0)),
                   pl.BlockSpec((None, tq, LANES), lambda b, i, n: (b, i, 0)),
                   pl.BlockSpec((None, tq, LANES), lambda b, i, n: (b, i, 0)),
                   pl.BlockSpec(memory_space=pltpu.SMEM)],
        scratch_shapes=[pltpu.VMEM((IDX_HEADS, tq, LANES), jnp.float32)],
    )
    return pl.pallas_call(
        functools.partial(_index_kernel, tq=tq, kb=kb, sk=sk, k_sel=k_sel),
        grid_spec=grid_spec,
        out_shape=[jax.ShapeDtypeStruct((batches, nq * tq, sk), jnp.float32),
                   jax.ShapeDtypeStruct((batches, nq * tq, LANES), jnp.float32),
                   jax.ShapeDtypeStruct((batches, nq * tq, LANES), jnp.float32),
                   jax.ShapeDtypeStruct((batches * nq,), jnp.int32)],
        compiler_params=_params(("arbitrary", "arbitrary")),
        name="index_select",
    )(nkb, qir, wis, qc, kit, kc)


def _fold8(x, op):
    parts = [x[g * 8:(g + 1) * 8, :] for g in range(x.shape[0] // 8)]
    while len(parts) > 1:
        parts = [op(parts[a], parts[a + 1]) for a in range(0, len(parts) - 1, 2)] + (parts[-1:] if len(parts) % 2 else [])
    return parts[0]


def _index_kernel_t(qit_ref, wit_ref, ki_ref, st_ref, thr_ref, taken_ref, tie_ref, *, tq, kb, k_sel):
    i = pl.program_id(0)
    nkb = lax.div((i + 1) * tq + (kb - 1), kb)
    total = st_ref.shape[0] // kb
    qchunk = lax.shift_right_logical(i * tq + lax.broadcasted_iota(jnp.int32, (kb, tq), 1), CHUNK_SHIFT)
    krow = lax.broadcasted_iota(jnp.int32, (kb, tq), 0)

    def score_tile(j, carry):
        mx, mn, nf = carry
        off = pl.multiple_of(j * kb, kb)
        k_t = ki_ref[pl.ds(off, kb), :]
        acc = jnp.zeros((kb, tq), jnp.float32)
        for h in range(IDX_HEADS):
            r = jnp.dot(k_t, qit_ref[h * IDX_DIM:(h + 1) * IDX_DIM, :], preferred_element_type=jnp.float32)
            acc = acc + jnp.maximum(r, 0.0) * wit_ref[h:h + 1, :]
        adm = lax.shift_right_logical(off + krow, CHUNK_SHIFT) <= qchunk
        s = jnp.where(adm, acc, NEG_INF)
        st_ref[pl.ds(off, kb), :] = s
        fin = s > NEG_INF
        mx = jnp.maximum(mx, _fold8(s, jnp.maximum))
        mn = jnp.minimum(mn, _fold8(jnp.where(fin, s, POS_INF), jnp.minimum))
        nf = nf + _fold8(jnp.where(fin, 1.0, 0.0), jnp.add)
        return mx, mn, nf

    mx, mn, nf = lax.fori_loop(
        0, nkb, score_tile,
        (jnp.full((8, tq), NEG_INF, jnp.float32), jnp.full((8, tq), POS_INF, jnp.float32),
         jnp.zeros((8, tq), jnp.float32)))

    def fill_tile(j, carry):
        off = pl.multiple_of(j * kb, kb)
        st_ref[pl.ds(off, kb), :] = jnp.full((kb, tq), NEG_INF, jnp.float32)
        return carry

    lax.fori_loop(nkb, total, fill_tile, 0)
    mx = jnp.max(mx, axis=0, keepdims=True)
    mn = jnp.min(mn, axis=0, keepdims=True)
    nf = jnp.sum(nf, axis=0, keepdims=True)

    def count(v, strict):
        def tile(j, cnt):
            off = pl.multiple_of(j * kb, kb)
            s = st_ref[pl.ds(off, kb), :]
            return cnt + _fold8(jnp.where(s > v if strict else s >= v, 1.0, 0.0), jnp.add)
        cnt = lax.fori_loop(0, nkb, tile, jnp.zeros((8, tq), jnp.float32))
        return jnp.sum(cnt, axis=0, keepdims=True)

    thr, taken, any_tie = _kth_threshold(count, mn, mx, nf, k_sel)
    thr_ref[...] = jnp.broadcast_to(thr, (8, tq))
    taken_ref[...] = jnp.broadcast_to(taken, (8, tq))
    tie_ref[i] = jnp.where(any_tie, 1, 0).astype(jnp.int32)


def _index_scores_t(qirt, wist, kirb, *, tq, kb, k_sel):
    idx_w, seq = qirt.shape
    nq = seq // tq
    return pl.pallas_call(
        functools.partial(_index_kernel_t, tq=tq, kb=kb, k_sel=k_sel),
        grid=(nq,),
        in_specs=[pl.BlockSpec((idx_w, tq), lambda i: (0, i)),
                  pl.BlockSpec((LANES, tq), lambda i: (0, i)),
                  pl.BlockSpec((seq, IDX_DIM), lambda i: (0, 0))],
        out_specs=[pl.BlockSpec((seq, tq), lambda i: (0, i)),
                   pl.BlockSpec((8, tq), lambda i: (0, i)),
                   pl.BlockSpec((8, tq), lambda i: (0, i)),
                   pl.BlockSpec(memory_space=pltpu.SMEM)],
        out_shape=[jax.ShapeDtypeStruct((seq, seq), jnp.float32),
                   jax.ShapeDtypeStruct((8, seq), jnp.float32),
                   jax.ShapeDtypeStruct((8, seq), jnp.float32),
                   jax.ShapeDtypeStruct((nq,), jnp.int32)],
        compiler_params=_params(("arbitrary",)),
        name="index_select_t",
    )(qirt, wist, kirb)


def _tie_rank_matrix(n, lower):
    rows = lax.broadcasted_iota(jnp.int32, (n, n), 0)
    cols = lax.broadcasted_iota(jnp.int32, (n, n), 1)
    return jnp.where(cols < rows if lower else rows < cols, 1.0, 0.0).astype(jnp.bfloat16)


def _attn_kernel_t(qi_ref, kj_ref, nk_ref, tie_ref, qt_ref, zb_ref, k_ref, vt_ref, st_ref, thr_ref, taken_ref,
                   qn_ref, kn_ref, o_ref, m_ref, c_ref, acc_ref, s_ref, p_ref, bias_ref, seen_ref, *, n_heads):
    p = pl.program_id(0)
    kj = kj_ref[p]
    hpg = n_heads // N_KV

    @pl.when(kj == 0)
    def _():
        m_ref[...] = jnp.full(m_ref.shape, NEG_INF, jnp.float32)
        c_ref[...] = jnp.zeros(c_ref.shape, jnp.float32)
        acc_ref[...] = jnp.zeros(acc_ref.shape, jnp.float32)
        seen_ref[...] = jnp.zeros(seen_ref.shape, jnp.float32)

    kb = st_ref.shape[0]
    rc = ATTN_ROW_CHUNK
    thr = thr_ref[0:1, :]
    for r in range(0, kb, rc):
        bias_ref[r:r + rc, :] = jnp.where(st_ref[r:r + rc, :] >= thr, 0.0, NEG_INF)

    @pl.when(tie_ref[qi_ref[p]] != 0)
    def _():
        taken = taken_ref[0:1, :]
        ltri = _tie_rank_matrix(RANK_CHUNK, True)
        seen = seen_ref[...]
        for r in range(0, kb, RANK_CHUNK):
            s = st_ref[r:r + RANK_CHUNK, :]
            eq = s == thr
            rank = jnp.dot(ltri, jnp.where(eq, 1.0, 0.0).astype(jnp.bfloat16),
                           preferred_element_type=jnp.float32) + seen
            take = jnp.logical_or(s > thr, jnp.logical_and(eq, rank < taken))
            bias_ref[r:r + RANK_CHUNK, :] = jnp.where(take, 0.0, NEG_INF)
            seen = seen + jnp.sum(jnp.where(eq, 1.0, 0.0), axis=0, keepdims=True)
        seen_ref[...] = seen

    def logits(h, r):
        g = h // hpg
        return jnp.dot(k_ref[r:r + rc, g * HEAD_DIM:(g + 1) * HEAD_DIM], qt_ref[h * HEAD_DIM:(h + 1) * HEAD_DIM, :],
                       preferred_element_type=jnp.float32) + bias_ref[r:r + rc, :]

    def weighted_values(h):
        g = h // hpg
        return jnp.dot(vt_ref[g * VT_ROWS:(g + 1) * VT_ROWS, :], p_ref[h], preferred_element_type=jnp.float32)

    first = kj == 0
    shift0 = [jnp.where(m_ref[h] == NEG_INF, 0.0, m_ref[h]) for h in range(n_heads)]
    kn = kn_ref[0]
    for blk in range(1, kn_ref.shape[0]):
        kn = jnp.maximum(kn, kn_ref[blk])
    excess = None
    for h in range(n_heads):
        bound = qn_ref[h:h + 1, :] * kn[0:1, h // hpg:h // hpg + 1]
        e = bound - shift0[h]
        excess = e if excess is None else jnp.maximum(excess, e)
    in_range = jnp.max(excess) <= EXP_RANGE

    @pl.when(in_range)
    def _():
        for h in range(n_heads):
            cm = None
            for r in range(0, kb, rc):
                t = logits(h, r) - shift0[h]
                p_ref[h, r:r + rc, :] = jnp.exp2(t).astype(p_ref.dtype)
                c8 = _fold8(t, jnp.maximum)
                cm = c8 if cm is None else jnp.maximum(cm, c8)
            m_ref[h] = jnp.maximum(m_ref[h], shift0[h] + jnp.max(cm, axis=0, keepdims=True))
        for h in range(n_heads):
            alpha = jnp.where(first, 1.0, jnp.exp2(c_ref[h] - shift0[h]))
            acc_ref[h] = alpha * acc_ref[h] + weighted_values(h)
            c_ref[h] = shift0[h]

    @pl.when(jnp.logical_not(in_range))
    def _():
        cmax = []
        for h in range(n_heads):
            cm = None
            for r in range(0, kb, rc):
                s = logits(h, r)
                s_ref[h, r:r + rc, :] = s
                c8 = _fold8(s, jnp.maximum)
                cm = c8 if cm is None else jnp.maximum(cm, c8)
            cmax.append(jnp.max(cm, axis=0, keepdims=True))
        for h in range(n_heads):
            m_cur = jnp.maximum(m_ref[h], cmax[h])
            shift = jnp.where(m_cur == NEG_INF, 0.0, m_cur)
            for r in range(0, kb, rc):
                p_ref[h, r:r + rc, :] = jnp.exp2(s_ref[h, r:r + rc, :] - shift).astype(p_ref.dtype)
            alpha = jnp.where(first, 1.0, jnp.exp2(c_ref[h] - shift))
            acc_ref[h] = alpha * acc_ref[h] + weighted_values(h)
            c_ref[h] = shift
            m_ref[h] = m_cur

    @pl.when(kj == nk_ref[p] - 1)
    def _():
        for h in range(n_heads):
            cs = slice(h * HEAD_DIM, (h + 1) * HEAD_DIM)
            o = (acc_ref[h, 0:HEAD_DIM, :] / acc_ref[h, HEAD_DIM:HEAD_DIM + 1, :]).T
            o_ref[:, cs] = (o * _silu(zb_ref[:, cs])).astype(o_ref.dtype)


def _masked_attention_t(qrt, h_main, zb_col, krb, vt, st, thr, taken, ties, qn, kn, steps, *, tq, kb):
    qidx, kidx, nk = steps
    d_b, seq = qrt.shape
    n_heads = d_b // HEAD_DIM
    kv_w = N_KV * HEAD_DIM
    kn_blocks = kb * kn.shape[0] // seq
    assert kb % RANK_CHUNK == 0 and kb % ATTN_ROW_CHUNK == 0 and kn_blocks >= 1
    grid_spec = pltpu.PrefetchScalarGridSpec(
        num_scalar_prefetch=4,
        grid=(qidx.shape[0],),
        in_specs=[pl.BlockSpec((d_b, tq), lambda p, q, k, n, t: (0, q[p])),
                  pl.BlockSpec((tq, d_b), lambda p, q, k, n, t: (q[p], zb_col)),
                  pl.BlockSpec((kb, kv_w), lambda p, q, k, n, t: (k[p], 0)),
                  pl.BlockSpec((N_KV * VT_ROWS, kb), lambda p, q, k, n, t: (0, k[p])),
                  pl.BlockSpec((kb, tq), lambda p, q, k, n, t: (k[p], q[p])),
                  pl.BlockSpec((8, tq), lambda p, q, k, n, t: (0, q[p])),
                  pl.BlockSpec((8, tq), lambda p, q, k, n, t: (0, q[p])),
                  pl.BlockSpec((n_heads, tq), lambda p, q, k, n, t: (0, q[p])),
                  pl.BlockSpec((kn_blocks, 8, LANES), lambda p, q, k, n, t: (k[p], 0, 0))],
        out_specs=pl.BlockSpec((tq, d_b), lambda p, q, k, n, t: (q[p], 0)),
        scratch_shapes=[pltpu.VMEM((n_heads, 1, tq), jnp.float32),
                        pltpu.VMEM((n_heads, 1, tq), jnp.float32),
                        pltpu.VMEM((n_heads, VT_ROWS, tq), jnp.float32),
                        pltpu.VMEM((n_heads, kb, tq), jnp.float32),
                        pltpu.VMEM((n_heads, kb, tq), jnp.bfloat16),
                        pltpu.VMEM((kb, tq), jnp.float32),
                        pltpu.VMEM((1, tq), jnp.float32)],
    )
    return pl.pallas_call(
        functools.partial(_attn_kernel_t, n_heads=n_heads),
        grid_spec=grid_spec,
        out_shape=jax.ShapeDtypeStruct((seq, d_b), jnp.bfloat16),
        compiler_params=_params(("arbitrary",)),
        name="masked_attention_t",
    )(qidx, kidx, nk, ties, qrt, h_main, krb, vt, st, thr, taken, qn, kn)


def _attn_kernel(tie_ref, q_ref, zb_ref, kt_ref, v_ref, s_ref, thr_ref, taken_ref, o_ref, bias_ref, *, n_heads):
    b = pl.program_id(0)
    hpg = n_heads // N_KV
    sk = s_ref.shape[1]
    thr = thr_ref[:, 0:1]
    bias_ref[...] = jnp.where(s_ref[...] >= thr, 0.0, NEG_INF)

    @pl.when(tie_ref[b] != 0)
    def _():
        taken = taken_ref[:, 0:1]
        utri = _tie_rank_matrix(RANK_CHUNK, False)
        seen = jnp.zeros_like(thr)
        for c in range(0, sk, RANK_CHUNK):
            s = s_ref[:, c:c + RANK_CHUNK]
            eq = s == thr
            rank = jnp.dot(jnp.where(eq, 1.0, 0.0).astype(jnp.bfloat16), utri,
                           preferred_element_type=jnp.float32) + seen
            take = jnp.logical_or(s > thr, jnp.logical_and(eq, rank < taken))
            bias_ref[:, c:c + RANK_CHUNK] = jnp.where(take, 0.0, NEG_INF)
            seen = seen + jnp.sum(jnp.where(eq, 1.0, 0.0), axis=1, keepdims=True)

    for h in range(n_heads):
        g = h // hpg
        cs = slice(h * HEAD_DIM, (h + 1) * HEAD_DIM)
        s = jnp.dot(q_ref[:, cs], kt_ref[g], preferred_element_type=jnp.float32) + bias_ref[...]
        pr = jnp.exp2(s - jnp.max(s, axis=1, keepdims=True))
        o = jnp.dot(pr.astype(v_ref.dtype), v_ref[:, g * HEAD_DIM:(g + 1) * HEAD_DIM],
                    preferred_element_type=jnp.float32) / jnp.sum(pr, axis=1, keepdims=True)
        o_ref[:, cs] = (o * _silu(zb_ref[:, cs])).astype(o_ref.dtype)


def _masked_attention(qr, h_main, zb_col, kt, vb, scores, thr, taken, ties, *, zrow0, tq, d_b):
    batches, _, _, sk = kt.shape
    zrb0 = zrow0 // tq
    n_heads = d_b // HEAD_DIM
    kv_w = N_KV * HEAD_DIM
    assert sk % RANK_CHUNK == 0
    grid_spec = pltpu.PrefetchScalarGridSpec(
        num_scalar_prefetch=1,
        grid=(batches,),
        in_specs=[pl.BlockSpec((tq, d_b), lambda b, t: (b, 0)),
                  pl.BlockSpec((tq, d_b), lambda b, t: (zrb0 + b, zb_col)),
                  pl.BlockSpec((None, N_KV, HEAD_DIM, sk), lambda b, t: (b, 0, 0, 0)),
                  pl.BlockSpec((None, sk, kv_w), lambda b, t: (b, 0, 0)),
                  pl.BlockSpec((None, tq, sk), lambda b, t: (b, 0, 0)),
                  pl.BlockSpec((None, tq, LANES), lambda b, t: (b, 0, 0)),
                  pl.BlockSpec((None, tq, LANES), lambda b, t: (b, 0, 0))],
        out_specs=pl.BlockSpec((tq, d_b), lambda b, t: (b, 0)),
        scratch_shapes=[pltpu.VMEM((tq, sk), jnp.float32)],
    )
    return pl.pallas_call(
        functools.partial(_attn_kernel, n_heads=n_heads),
        grid_spec=grid_spec,
        out_shape=jax.ShapeDtypeStruct((batches * tq, d_b), jnp.bfloat16),
        compiler_params=_params(("arbitrary",)),
        name="masked_attention",
    )(ties, qr, h_main, kt, vb, scores, thr, taken)


def _out_kernel(mixa_ref, mixb_ref, w_ref, x_ref, g_ref, o_ref, ssq_ref, *, d_a, tn):
    j = pl.program_id(1)

    @pl.when(j == 0)
    def _():
        ssq_ref[...] = jnp.zeros(ssq_ref.shape, jnp.float32)

    y = (x_ref[...] + jnp.dot(mixa_ref[...], w_ref[0:d_a, :], preferred_element_type=jnp.float32)
         + jnp.dot(mixb_ref[...], w_ref[d_a:, :], preferred_element_type=jnp.float32))
    o_ref[:, pl.ds(pl.multiple_of(j * tn, tn), tn)] = y
    ssq_ref[...] += jnp.sum(y * y, axis=-1, keepdims=True)

    @pl.when(j == pl.num_programs(1) - 1)
    def _():
        inv = lax.rsqrt(ssq_ref[...] * (1.0 / o_ref.shape[1]) + EPS)
        o_ref[...] = o_ref[...] * inv * g_ref[...]


def _out_proj(mix_a, mix_b, w, x, g):
    n, d = x.shape
    d_a, d_b = mix_a.shape[1], mix_b.shape[1]
    tm = _largest_divisor(n, (512, 256, 128))
    tn = _largest_divisor(d, (1024, 512, 256, 128))
    return pl.pallas_call(
        functools.partial(_out_kernel, d_a=d_a, tn=tn),
        grid=(n // tm, d // tn),
        in_specs=[pl.BlockSpec((tm, d_a), lambda i, j: (i, 0)),
                  pl.BlockSpec((tm, d_b), lambda i, j: (i, 0)),
                  pl.BlockSpec((d_a + d_b, tn), lambda i, j: (0, j)),
                  pl.BlockSpec((tm, tn), lambda i, j: (i, j)), pl.BlockSpec((1, d), lambda i, j: (0, 0))],
        out_specs=pl.BlockSpec((tm, d), lambda i, j: (i, 0)),
        out_shape=jax.ShapeDtypeStruct((n, d), jnp.float32),
        scratch_shapes=[pltpu.VMEM((tm, 1), jnp.float32)],
        compiler_params=_params(("arbitrary", "arbitrary")),
        name="out_proj",
    )(mix_a, mix_b, w, x, g.reshape(1, d))


def _cdiv(a, b):
    return -(-a // b)


def _attention_steps(nq, tq, kb):
    q_l, k_l, n_l = [], [], []
    for q in range(nq):
        nk = _cdiv((q + 1) * tq, kb)
        for k in range(nk):
            q_l.append(q), k_l.append(k), n_l.append(nk)
    return tuple(jnp.asarray(np.asarray(v, np.int32)) for v in (q_l, k_l, n_l))


def _layer(x_prompt, x_sample, cache_k, cache_v, cache_ki, g_norm, w_in, w_s, b_s, g_v, w_out, final_g):
    f32, bf = jnp.float32, jnp.bfloat16
    _, seq, d_model = x_prompt.shape
    dec_b, dec_t, _ = x_sample.shape
    past = cache_k.shape[1]
    d_a = d_model // 2
    d_b = d_model - d_a
    groups = d_a // HEAD_DIM
    kv_w = N_KV * HEAD_DIM
    idx_w = IDX_HEADS * IDX_DIM
    n_s = dec_b * dec_t
    n_all = seq + n_s
    assert seq % 512 == 0 and n_s % MLP_CHUNK == 0 and dec_t <= CHUNK and past % MLP_CHUNK == 0
    assert idx_w % d_a == 0 and d_a % HEAD_DIM == 0

    x_p, x_s = x_prompt[0], x_sample.reshape(n_s, d_model)
    widths = (d_a, d_a, d_a, d_b, kv_w, kv_w, d_b, idx_w, IDX_DIM, IDX_HEADS)
    o_u, o_va, o_za, o_q, o_k, o_v, o_zb, o_qi, o_ki, o_wi = np.concatenate([[0], np.cumsum(widths)])[:-1].tolist()

    def col_blocks(segments, bw):
        return [(off + c) // bw for off, wd in segments for c in range(0, wd, bw)]

    w_in_t = w_in.T
    bw = int(np.gcd.reduce([d_a, d_b, kv_w, idx_w, o_q, o_zb, o_qi]))
    w_main = _regroup_cast(w_in_t, col_blocks(((o_qi, idx_w), (o_q, d_b), (o_u, d_a), (o_va, d_a), (o_za, d_a),
                                               (o_zb, d_b)), bw), bw, bf)
    assert o_k % LANES == 0 and o_ki % LANES == 0 and o_wi % LANES == 0
    w_small = _regroup_cast(w_in_t, col_blocks(((o_k, 2 * kv_w), (o_ki, IDX_DIM), (o_wi, LANES)), LANES), LANES, bf,
                            tail_valid=IDX_HEADS)
    q_col = idx_w // d_a
    u_col, zb_col = q_col + 1, q_col + 4

    half = HEAD_DIM // 2
    inv = ROPE_THETA ** (-2.0 * jnp.arange(half, dtype=f32) / HEAD_DIM)

    def rope_tables(pos):
        ang = pos.astype(f32)[:, None] * inv[None, :]
        return (jnp.concatenate([jnp.cos(ang), jnp.cos(ang)], axis=1),
                jnp.concatenate([-jnp.sin(ang), jnp.sin(ang)], axis=1))

    pos_s = past + jnp.tile(jnp.arange(dec_t), dec_b)
    cos_p, sin_p = rope_tables(jnp.arange(seq))
    cos_s, sin_s = rope_tables(pos_s)

    hr_p, xn_p = _in_proj(x_p, g_norm, w_main, f32, row0=idx_w + d_b, emit_xn=True)
    (qirt,) = _qproj(xn_p, w_main, 0, idx_w, cos_p, sin_p, scale=1.0, with_norms=False)
    qrt, qn_p = _qproj(xn_p, w_main, idx_w, d_b, cos_p, sin_p, scale=HEAD_DIM ** -0.5 * LOG2_E, with_norms=True)
    hs_p = _in_proj(x_p, g_norm, w_small, f32)
    hm_s, hs_s = _in_proj(x_s, g_norm, w_main, f32), _in_proj(x_s, g_norm, w_small, f32)

    pidx = np.arange(MLP_CHUNK)
    mask_p = (pidx[None, :] // CHUNK) <= (pidx[:, None] // CHUNK)
    wm_p = jnp.where(mask_p[None], w_s, 0.0).astype(bf)
    b_p = b_s[:, :, None]
    tidx = pidx % dec_t
    same = (pidx[None, :] // dec_t) == (pidx[:, None] // dec_t)
    mask_s = same & ((tidx[None, :] // CHUNK) <= (tidx[:, None] // CHUNK))
    wm_s = jnp.where(mask_s[None], w_s[:, tidx][:, :, tidx], 0.0).astype(bf)
    b_sm = b_s[:, tidx][:, :, None]
    gv = g_v.reshape(1, d_a)
    (mix_a_p,) = _mlp_group(hr_p, 0, d_a, 0, seq, 512, wm_p, b_p, gv, False)
    mix_a_s, vn_s = _mlp_group(hm_s, u_col, d_a, 0, n_s, MLP_CHUNK, wm_s, b_sm, gv, True)

    kr_p, krb_p, vt_p, kir_p, kirb_p, wist, kn_p = _keyprep(hs_p, cos_p, sin_p)
    qir_s, qr_s, kr_s, krb_s, vb_s, kir_s, kirb_s, wis_s = _prep(
        hm_s, hs_s, cos_s, sin_s, d_b, q_col, row0=0, nrows=n_s, transposed=False)
    v_p, v_s_new = hs_p[:, kv_w:2 * kv_w], hs_s[:, kv_w:2 * kv_w]

    tq_p, kb_i, kb_a = PROMPT_TQ, PROMPT_KB_INDEX, PROMPT_KB_ATTN
    k_sel_p = min(TOPK_MAX, seq // 4)
    st_p, thr_p, taken_p, ties_p = _index_scores_t(qirt, wist, kirb_p, tq=tq_p, kb=kb_i, k_sel=k_sel_p)
    steps_p = _attention_steps(seq // tq_p, tq_p, kb_a)
    mix_b_p = _masked_attention_t(qrt, hr_p, 3, krb_p, vt_p, st_p, thr_p, taken_p, ties_p, qn_p, kn_p,
                                  steps_p, tq=tq_p, kb=kb_a)

    kb_s = SAMPLE_KB_INDEX
    n_keys = past + dec_t
    sk_s = _cdiv(n_keys, kb_s) * kb_s
    padk = lambda a: jnp.pad(a, ((0, 0), (0, sk_s - n_keys), (0, 0)))
    k_s = padk(jnp.concatenate([cache_k.reshape(dec_b, past, kv_w).astype(bf), krb_s.reshape(dec_b, dec_t, kv_w)], axis=1))
    v_s = padk(jnp.concatenate([cache_v.reshape(dec_b, past, kv_w).astype(bf), vb_s.reshape(dec_b, dec_t, kv_w)], axis=1))
    ki_s = padk(jnp.concatenate([cache_ki.astype(bf), kirb_s.reshape(dec_b, dec_t, IDX_DIM)], axis=1))
    kit_s = ki_s.transpose(0, 2, 1)
    kt_s = k_s.reshape(dec_b, sk_s, N_KV, HEAD_DIM).transpose(0, 2, 3, 1)
    kpos = jnp.arange(sk_s, dtype=jnp.int32)
    kc_s = jnp.broadcast_to(jnp.where(kpos < n_keys, kpos // CHUNK, FAR_CHUNK)[None, None, :], (dec_b, 1, sk_s))
    qc_s = (pos_s // CHUNK).astype(jnp.int32)[:, None]
    k_sel_s = min(TOPK_MAX, n_keys // 4)
    nkb_s = jnp.full((dec_b,), sk_s // kb_s, jnp.int32)
    s_s, thr_s, taken_s, ties_s = _index_scores(qir_s, wis_s, qc_s, kit_s, kc_s, nkb_s, row0=0, batches=dec_b, nq=1,
                                                tq=dec_t, kb=kb_s, k_sel=k_sel_s)
    mix_b_s = _masked_attention(qr_s, hm_s, zb_col, kt_s, v_s, s_s, thr_s, taken_s, ties_s,
                                zrow0=0, tq=dec_t, d_b=d_b)

    w_out_b = w_out.astype(bf)
    y_p = _out_proj(mix_a_p, mix_b_p, w_out_b, x_p, final_g)
    y_s = _out_proj(mix_a_s, mix_b_s, w_out_b, x_s, final_g)

    shp_p = (1, 1, seq, N_KV, HEAD_DIM)
    shp_s = (1, dec_b, dec_t, N_KV, HEAD_DIM)
    return (y_p[None], y_s.reshape(dec_b, dec_t, d_model),
            kr_p.reshape(shp_p), v_p.reshape(shp_p), kir_p.reshape(1, 1, seq, IDX_DIM),
            kr_s.reshape(shp_s), v_s_new.reshape(shp_s), kir_s.reshape(1, dec_b, dec_t, IDX_DIM),
            vn_s.reshape(1, dec_b, dec_t, groups, HEAD_DIM))


def kernel(x_prompt, x_sample, cache_k, cache_v, cache_idx_k, norm_g, w_in, w_s, b_s, v_norm_g, w_out, final_norm_g):
    assert x_prompt.shape[0] == 1 and norm_g.shape[0] == 1, "one prompt stream and one layer"
    return _layer(x_prompt, x_sample, cache_k[0], cache_v[0], cache_idx_k[0], norm_g[0], w_in[0], w_s[0], b_s[0],
                  v_norm_g[0], w_out[0], final_norm_g)
```

```python
import functools

import numpy as np
import jax
import jax.numpy as jnp
from jax import lax
from jax.experimental import pallas as pl
from jax.experimental.pallas import tpu as pltpu

CHUNK = 64
MLP_CHUNK = 128
HEAD_DIM = 128
N_KV = 4
IDX_HEADS = 32
IDX_DIM = 128
TOPK_MAX = 256
ROPE_THETA = 10000.0
EPS = 1e-6

LANES = 128
VMEM_LIMIT = 56 * 1024 * 1024
NEG_INF = float("-inf")
POS_INF = float("inf")
FAR_CHUNK = 1 << 20
LOG2_E = 1.4426950408889634
CHUNK_SHIFT = 6
PROMPT_TQ, PROMPT_KB_INDEX, PROMPT_KB_ATTN = 256, 256, 1024
ONES_ROWS = 16
VT_ROWS = HEAD_DIM + ONES_ROWS
ATTN_ROW_CHUNK = 128
TAKE_ALL = 1.0e9
RANK_CHUNK = 256
EXP_RANGE = 60.0
NORM_MARGIN = 1.001
SAMPLE_KB_INDEX = 256


def _largest_divisor(n, candidates):
    for c in candidates:
        if n % c == 0:
            return c
    raise ValueError(f"no tile in {candidates} divides {n}")


def _params(sem, flags=None):
    return pltpu.CompilerParams(dimension_semantics=sem, vmem_limit_bytes=VMEM_LIMIT, flags=flags)


def _cast_kernel(src_ref, w_ref, o_ref, *, bw, tail_valid):
    w = w_ref[...]
    if tail_valid < bw:
        row = lax.broadcasted_iota(jnp.int32, w.shape, 0)
        last = pl.program_id(0) == pl.num_programs(0) - 1
        w = jnp.where(jnp.logical_and(last, row >= tail_valid), 0.0, w)
    o_ref[...] = w.astype(o_ref.dtype)


def _regroup_cast(wt, src_blocks, bw, out_dtype, tail_valid=None):
    d = wt.shape[1]
    tail_valid = bw if tail_valid is None else tail_valid
    src = jnp.asarray(np.asarray(src_blocks, np.int32))
    grid_spec = pltpu.PrefetchScalarGridSpec(
        num_scalar_prefetch=1,
        grid=(len(src_blocks),),
        in_specs=[pl.BlockSpec((bw, d), lambda j, s: (s[j], 0))],
        out_specs=pl.BlockSpec((bw, d), lambda j, s: (j, 0)),
    )
    return pl.pallas_call(
        functools.partial(_cast_kernel, bw=bw, tail_valid=tail_valid),
        grid_spec=grid_spec,
        out_shape=jax.ShapeDtypeStruct((len(src_blocks) * bw, d), out_dtype),
        compiler_params=_params(("arbitrary",)),
        name="regroup_cast",
    )(src, wt)


def _in_proj_kernel(x_ref, g_ref, wt_ref, o_ref, xn_ref):
    @pl.when(pl.program_id(1) == 0)
    def _():
        x = x_ref[...]
        ms = jnp.mean(x * x, axis=-1, keepdims=True)
        xn_ref[...] = (x * lax.rsqrt(ms + EPS) * g_ref[...]).astype(xn_ref.dtype)

    o_ref[...] = lax.dot_general(xn_ref[...], wt_ref[...], (((1,), (1,)), ((), ())),
                                 preferred_element_type=jnp.float32).astype(o_ref.dtype)


def _in_proj(x, g, wt, out_dtype, row0=0, emit_xn=False):
    m, k = x.shape
    n = wt.shape[0] - row0
    tm = _largest_divisor(m, (512, 256, 128))
    tn = _largest_divisor(int(np.gcd(n, row0)), (1024, 1280, 512, 256, 128))
    out_specs = [pl.BlockSpec((tm, tn), lambda i, j: (i, j))]
    out_shape = [jax.ShapeDtypeStruct((m, n), out_dtype)]
    scratch = [pltpu.VMEM((tm, k), wt.dtype)]
    if emit_xn:
        out_specs.append(pl.BlockSpec((tm, k), lambda i, j: (i, 0)))
        out_shape.append(jax.ShapeDtypeStruct((m, k), wt.dtype))
        scratch = []
    res = pl.pallas_call(
        _in_proj_kernel,
        grid=(m // tm, n // tn),
        in_specs=[pl.BlockSpec((tm, k), lambda i, j: (i, 0)), pl.BlockSpec((1, k), lambda i, j: (0, 0)),
                  pl.BlockSpec((tn, k), lambda i, j: (row0 // tn + j, 0))],
        out_specs=out_specs,
        out_shape=out_shape,
        scratch_shapes=scratch,
        compiler_params=_params(("arbitrary", "arbitrary")),
        name="in_proj",
    )(x, g.reshape(1, k), wt)
    return res if emit_xn else res[0]


def _xn_proj_kernel(xn_ref, wt_ref, o_ref):
    o_ref[...] = lax.dot_general(xn_ref[...], wt_ref[...], (((1,), (1,)), ((), ())),
                                 preferred_element_type=jnp.float32).astype(o_ref.dtype)


def _xn_proj(xn, wt, out_dtype):
    m, k = xn.shape
    n = wt.shape[0]
    tm = _largest_divisor(m, (512, 256, 128))
    tn = _largest_divisor(n, (1024, 1280, 512, 256, 128))
    return pl.pallas_call(
        _xn_proj_kernel,
        grid=(m // tm, n // tn),
        in_specs=[pl.BlockSpec((tm, k), lambda i, j: (i, 0)), pl.BlockSpec((tn, k), lambda i, j: (j, 0))],
        out_specs=pl.BlockSpec((tm, tn), lambda i, j: (i, j)),
        out_shape=jax.ShapeDtypeStruct((m, n), out_dtype),
        compiler_params=_params(("arbitrary", "arbitrary")),
        name="kv_proj",
    )(xn, wt)


def _silu(z):
    return z * (1.0 / (1.0 + jnp.exp(-z)))


def _mlp_kernel(u_ref, va_ref, za_ref, wm_ref, b_ref, gv_ref, mix_ref, *vn_refs, rows, groups):
    for c in range(rows // MLP_CHUNK):
        rs = slice(c * MLP_CHUNK, (c + 1) * MLP_CHUNK)
        for g in range(groups):
            cs = slice(g * HEAD_DIM, (g + 1) * HEAD_DIM)
            va = va_ref[rs, cs]
            mu = jnp.mean(va, axis=-1, keepdims=True)
            cen = va - mu
            var = jnp.mean(cen * cen, axis=-1, keepdims=True)
            vn = cen * lax.rsqrt(var + EPS) * gv_ref[:, cs]
            if vn_refs:
                vn_refs[0][rs, cs] = vn
            mixed = jnp.dot(wm_ref[g], vn.astype(jnp.bfloat16), preferred_element_type=jnp.float32) + b_ref[g]
            a = u_ref[rs, cs] * mixed
            mix_ref[rs, cs] = (a * _silu(za_ref[rs, cs])).astype(mix_ref.dtype)


def _mlp_group(h_main, col0, d_a, row0, nrows, rows, wm, bias, gv, emit_vn):
    groups = d_a // HEAD_DIM
    rb0 = row0 // rows
    out_shape = [jax.ShapeDtypeStruct((nrows, d_a), jnp.bfloat16)]
    out_specs = [pl.BlockSpec((rows, d_a), lambda i: (i, 0))]
    if emit_vn:
        out_shape.append(jax.ShapeDtypeStruct((nrows, d_a), jnp.float32))
        out_specs.append(pl.BlockSpec((rows, d_a), lambda i: (i, 0)))
    col = lambda c: pl.BlockSpec((rows, d_a), lambda i: (rb0 + i, c))
    return pl.pallas_call(
        functools.partial(_mlp_kernel, rows=rows, groups=groups),
        grid=(nrows // rows,),
        in_specs=[col(col0), col(col0 + 1), col(col0 + 2),
                  pl.BlockSpec((groups, MLP_CHUNK, MLP_CHUNK), lambda i: (0, 0, 0)),
                  pl.BlockSpec((groups, MLP_CHUNK, 1), lambda i: (0, 0, 0)),
                  pl.BlockSpec((1, d_a), lambda i: (0, 0))],
        out_specs=out_specs,
        out_shape=out_shape,
        compiler_params=_params(("arbitrary",)),
        name="chunk_mlp",
    )(h_main, h_main, h_main, wm, bias, gv)


def _rope(x, cos, sin):
    return x * cos + pltpu.roll(x, HEAD_DIM // 2, 1) * sin


def _qproj_kernel(xn_ref, wt_ref, cos_ref, sin_ref, qt_ref, *qn_refs, scale):
    acc = lax.dot_general(xn_ref[...], wt_ref[...], (((1,), (1,)), ((), ())), preferred_element_type=jnp.float32)
    cos, sin = cos_ref[...], sin_ref[...]
    for h in range(acc.shape[1] // HEAD_DIM):
        cs = slice(h * HEAD_DIM, (h + 1) * HEAD_DIM)
        qb = (_rope(acc[:, cs], cos, sin) * scale).T.astype(qt_ref.dtype)
        qt_ref[cs, :] = qb
        if qn_refs:
            qf = qb.astype(jnp.float32)
            qn_refs[0][h:h + 1, :] = jnp.sqrt(jnp.sum(qf * qf, axis=0, keepdims=True)) * NORM_MARGIN


def _qproj(xn, wt, row0, nrows_w, cos_t, sin_t, *, scale, with_norms):
    m, k = xn.shape
    tm = _largest_divisor(m, (512, 256, 128))
    tn = _largest_divisor(int(np.gcd(nrows_w, row0)) if row0 else nrows_w, (1024, 512, 256, 128))
    hpt = tn // HEAD_DIM
    out_specs = [pl.BlockSpec((tn, tm), lambda i, j: (j, i))]
    out_shape = [jax.ShapeDtypeStruct((nrows_w, m), jnp.bfloat16)]
    if with_norms:
        out_specs.append(pl.BlockSpec((hpt, tm), lambda i, j: (j, i)))
        out_shape.append(jax.ShapeDtypeStruct((nrows_w // HEAD_DIM, m), jnp.float32))
    return pl.pallas_call(
        functools.partial(_qproj_kernel, scale=scale),
        grid=(m // tm, nrows_w // tn),
        in_specs=[pl.BlockSpec((tm, k), lambda i, j: (i, 0)),
                  pl.BlockSpec((tn, k), lambda i, j: (row0 // tn + j, 0)),
                  pl.BlockSpec((tm, HEAD_DIM), lambda i, j: (i, 0)), pl.BlockSpec((tm, HEAD_DIM), lambda i, j: (i, 0))],
        out_specs=out_specs,
        out_shape=out_shape,
        compiler_params=_params(("arbitrary", "arbitrary")),
        name="q_proj",
    )(xn, wt, cos_t, sin_t)


def _keyprep_kernel(k_ref, v_ref, ki_ref, wi_ref, cos_ref, sin_ref,
                    kr_ref, krb_ref, vt_ref, kir_ref, kirb_ref, wist_ref, kn_ref):
    cos, sin = cos_ref[...], sin_ref[...]
    lane = lax.broadcasted_iota(jnp.int32, (8, LANES), 1)
    kn = jnp.zeros((8, LANES), jnp.float32)
    for h in range(N_KV):
        cs = slice(h * HEAD_DIM, (h + 1) * HEAD_DIM)
        kr = _rope(k_ref[:, cs], cos, sin)
        kr_ref[:, cs] = kr
        krb_ref[:, cs] = kr.astype(krb_ref.dtype)
        vt_ref[h * VT_ROWS:h * VT_ROWS + HEAD_DIM, :] = v_ref[:, cs].T.astype(vt_ref.dtype)
        vt_ref[h * VT_ROWS + HEAD_DIM:(h + 1) * VT_ROWS, :] = jnp.ones((ONES_ROWS, v_ref.shape[0]), vt_ref.dtype)
        kb16 = kr.astype(krb_ref.dtype).astype(jnp.float32)
        kmax = jnp.sqrt(jnp.max(jnp.sum(kb16 * kb16, axis=1, keepdims=True), axis=0, keepdims=True))
        kn = jnp.where(lane == h, kmax, kn)
    kn_ref[0] = kn
    kir = _rope(ki_ref[...], cos, sin)
    kir_ref[...] = kir
    kirb_ref[...] = kir.astype(kirb_ref.dtype)
    wist_ref[...] = (wi_ref[...] * ((IDX_HEADS * IDX_DIM) ** -0.5)).T


def _keyprep(h_small, cos_t, sin_t):
    nrows = h_small.shape[0]
    rows = _largest_divisor(nrows, (256, 128))
    kv_w = N_KV * HEAD_DIM
    inspec = lambda w, c: pl.BlockSpec((rows, w), lambda i: (i, c))
    rowspec = lambda w: pl.BlockSpec((rows, w), lambda i: (i, 0))
    colspec = lambda w: pl.BlockSpec((w, rows), lambda i: (0, i))
    bf, f32, sds = jnp.bfloat16, jnp.float32, jax.ShapeDtypeStruct
    return pl.pallas_call(
        _keyprep_kernel,
        grid=(nrows // rows,),
        in_specs=[inspec(kv_w, 0), inspec(kv_w, 1), inspec(IDX_DIM, 2 * kv_w // IDX_DIM),
                  inspec(LANES, 2 * kv_w // LANES + 1), inspec(HEAD_DIM, 0), inspec(HEAD_DIM, 0)],
        out_specs=[rowspec(kv_w), rowspec(kv_w), colspec(N_KV * VT_ROWS), rowspec(IDX_DIM), rowspec(IDX_DIM),
                   colspec(LANES), pl.BlockSpec((1, 8, LANES), lambda i: (i, 0, 0))],
        out_shape=[sds((nrows, kv_w), f32), sds((nrows, kv_w), bf), sds((N_KV * VT_ROWS, nrows), bf),
                   sds((nrows, IDX_DIM), f32), sds((nrows, IDX_DIM), bf), sds((LANES, nrows), f32),
                   sds((nrows // rows, 8, LANES), f32)],
        compiler_params=_params(("arbitrary",)),
        name="key_prep",
    )(h_small, h_small, h_small, h_small, cos_t, sin_t)


def _prep_kernel(qi_ref, q_ref, k_ref, v_ref, ki_ref, wi_ref, cos_ref, sin_ref,
                 qir_ref, qr_ref, kr_ref, krb_ref, vb_ref, kir_ref, kirb_ref, wis_ref, *norm_refs,
                 n_heads, transposed):
    cos = cos_ref[...]
    sin = sin_ref[...]

    def put(ref, h, width, val):
        if transposed:
            ref[h * width:(h + 1) * width, :] = val.T.astype(ref.dtype)
        else:
            ref[:, h * width:(h + 1) * width] = val.astype(ref.dtype)

    for h in range(IDX_HEADS):
        put(qir_ref, h, IDX_DIM, _rope(qi_ref[:, h * IDX_DIM:(h + 1) * IDX_DIM], cos, sin))
    q_scale = HEAD_DIM ** -0.5 * LOG2_E
    for h in range(n_heads):
        qs = _rope(q_ref[:, h * HEAD_DIM:(h + 1) * HEAD_DIM], cos, sin) * q_scale
        put(qr_ref, h, HEAD_DIM, qs)
        if transposed:
            qb = qs.T.astype(qr_ref.dtype).astype(jnp.float32)
            norm_refs[0][h:h + 1, :] = jnp.sqrt(jnp.sum(qb * qb, axis=0, keepdims=True)) * NORM_MARGIN
    lane = lax.broadcasted_iota(jnp.int32, (8, LANES), 1)
    kn = jnp.zeros((8, LANES), jnp.float32)
    for h in range(N_KV):
        cs = slice(h * HEAD_DIM, (h + 1) * HEAD_DIM)
        kr = _rope(k_ref[:, cs], cos, sin)
        kr_ref[:, cs] = kr
        krb_ref[:, cs] = kr.astype(krb_ref.dtype)
        if transposed:
            vb_ref[h * VT_ROWS:h * VT_ROWS + HEAD_DIM, :] = v_ref[:, cs].T.astype(vb_ref.dtype)
            vb_ref[h * VT_ROWS + HEAD_DIM:(h + 1) * VT_ROWS, :] = jnp.ones((ONES_ROWS, v_ref.shape[0]), vb_ref.dtype)
            kb16 = kr.astype(krb_ref.dtype).astype(jnp.float32)
            kmax = jnp.sqrt(jnp.max(jnp.sum(kb16 * kb16, axis=1, keepdims=True), axis=0, keepdims=True))
            kn = jnp.where(lane == h, kmax, kn)
        else:
            vb_ref[:, cs] = v_ref[:, cs].astype(vb_ref.dtype)
    if transposed:
        norm_refs[1][0] = kn
    kir = _rope(ki_ref[...], cos, sin)
    kir_ref[...] = kir
    kirb_ref[...] = kir.astype(kirb_ref.dtype)
    put(wis_ref, 0, LANES, wi_ref[...] * ((IDX_HEADS * IDX_DIM) ** -0.5))


def _prep(h_main, h_small, cos_t, sin_t, d_b, q_col, *, row0, nrows, transposed):
    rows = _largest_divisor(nrows, (256, 128))
    rb0 = row0 // rows
    n_heads = d_b // HEAD_DIM
    idx_w = IDX_HEADS * IDX_DIM
    kv_w = N_KV * HEAD_DIM
    inspec = lambda w, c: pl.BlockSpec((rows, w), lambda i: (rb0 + i, c))
    bf, f32 = jnp.bfloat16, jnp.float32
    outs = [(idx_w, bf, True), (d_b, bf, True), (kv_w, f32, False), (kv_w, bf, False), (kv_w, bf, True),
            (IDX_DIM, f32, False), (IDX_DIM, bf, False), (LANES, f32, True)]
    out_specs, out_shape = [], []
    for idx, (w, dt, feature_major) in enumerate(outs):
        if transposed and feature_major:
            w = N_KV * VT_ROWS if idx == 4 else w
            out_specs.append(pl.BlockSpec((w, rows), lambda i: (0, i)))
            out_shape.append(jax.ShapeDtypeStruct((w, nrows), dt))
        else:
            out_specs.append(pl.BlockSpec((rows, w), lambda i: (i, 0)))
            out_shape.append(jax.ShapeDtypeStruct((nrows, w), dt))
    if transposed:
        out_specs += [pl.BlockSpec((n_heads, rows), lambda i: (0, i)), pl.BlockSpec((1, 8, LANES), lambda i: (i, 0, 0))]
        out_shape += [jax.ShapeDtypeStruct((n_heads, nrows), f32), jax.ShapeDtypeStruct((nrows // rows, 8, LANES), f32)]
    return pl.pallas_call(
        functools.partial(_prep_kernel, n_heads=n_heads, transposed=transposed),
        grid=(nrows // rows,),
        in_specs=[inspec(idx_w, 0), inspec(d_b, q_col),
                  inspec(kv_w, 0), inspec(kv_w, 1), inspec(IDX_DIM, 2 * kv_w // IDX_DIM),
                  inspec(LANES, 2 * kv_w // LANES + 1), inspec(HEAD_DIM, 0), inspec(HEAD_DIM, 0)],
        out_specs=out_specs,
        out_shape=out_shape,
        compiler_params=_params(("arbitrary",)),
        name="rope_prep",
    )(h_main, h_main, h_small, h_small, h_small, h_small, cos_t, sin_t)


def _bisect_start(mn, mx, nf, k_sel):
    return mn, mx, mn, jnp.where(nf <= float(k_sel), 1.0, 0.0), jnp.zeros_like(mn)


def _bisect_step(state, mid, cnt, kk):
    lo, hi, thr, done, tie = state
    active = done < 0.5
    hit = jnp.logical_and(active, cnt == kk)
    no_room = jnp.logical_or(mid <= lo, mid >= hi)
    stuck = jnp.logical_and(active, jnp.logical_and(no_room, cnt != kk))
    thr = jnp.where(hit, mid, thr)
    tie = jnp.where(stuck, 1.0, tie)
    lo = jnp.where(jnp.logical_and(active, cnt > kk), mid, lo)
    hi = jnp.where(jnp.logical_and(active, cnt < kk), mid, hi)
    done = jnp.where(jnp.logical_or(hit, stuck), 1.0, done)
    return lo, hi, thr, done, tie


def _kth_threshold(count, mn, mx, nf, k_sel, start=None):
    kk = jnp.minimum(nf, float(k_sel))

    def cond(st):
        return st[5] > 0.5

    def body(st):
        lo, hi = st[0], st[1]
        mid = 0.5 * lo + 0.5 * hi
        new = _bisect_step(st[:5], mid, count(mid, False), kk)
        return new + (jnp.sum(1.0 - new[3]),)

    start = _bisect_start(mn, mx, nf, k_sel) if start is None else start
    lo, hi, thr, _, tie, _ = lax.while_loop(cond, body, tuple(start) + (jnp.sum(1.0 - start[3]),))
    any_tie = jnp.sum(tie) > 0.5

    def resolve(_):
        is_tie = tie > 0.5
        kth = jnp.where(count(hi, False) >= kk, hi, lo)
        thr_t = jnp.where(is_tie, kth, thr)
        taken = jnp.where(is_tie, kk - count(thr_t, True), TAKE_ALL)
        return thr_t, taken

    thr, taken = lax.cond(any_tie, resolve, lambda _: (thr, jnp.full_like(thr, TAKE_ALL)), 0)
    return thr, taken, any_tie


def _index_kernel(nkb_ref, qi_ref, wi_ref, qc_ref, kit_ref, kc_ref, s_ref, thr_ref, taken_ref, tie_ref, wb_ref,
                  *, tq, kb, sk, k_sel):
    b = pl.program_id(0)
    i = pl.program_id(1)
    nkb = nkb_ref[b * pl.num_programs(1) + i]
    lane_tiles = kb // LANES
    rt = min(tq, 128)

    for h in range(IDX_HEADS):
        wb_ref[h] = jnp.broadcast_to(wi_ref[:, h:h + 1], (tq, LANES))

    def score_tile(j, carry):
        off = pl.multiple_of(j * kb, kb)
        kt = kit_ref[:, pl.ds(off, kb)]
        kc = kc_ref[:, pl.ds(off, kb)]
        for r0 in range(0, tq, rt):
            accs = [jnp.zeros((rt, LANES), jnp.float32) for _ in range(lane_tiles)]
            for h in range(IDX_HEADS):
                r = jnp.dot(qi_ref[r0:r0 + rt, h * IDX_DIM:(h + 1) * IDX_DIM], kt,
                            preferred_element_type=jnp.float32)
                w = wb_ref[h, r0:r0 + rt, :]
                for l in range(lane_tiles):
                    accs[l] = accs[l] + jnp.maximum(r[:, l * LANES:(l + 1) * LANES], 0.0) * w
            acc = jnp.concatenate(accs, axis=1)
            adm = kc <= qc_ref[r0:r0 + rt, :]
            s_ref[r0:r0 + rt, pl.ds(off, kb)] = jnp.where(adm, acc, NEG_INF)
        return carry

    lax.fori_loop(0, nkb, score_tile, 0)

    def fill_tile(j, carry):
        off = pl.multiple_of(j * kb, kb)
        s_ref[:, pl.ds(off, kb)] = jnp.full((tq, kb), NEG_INF, jnp.float32)
        return carry

    lax.fori_loop(nkb, sk // kb, fill_tile, 0)

    def stats_tile(j, carry):
        mx, mn, nf = carry
        off = pl.multiple_of(j * kb, kb)
        for l in range(lane_tiles):
            s = s_ref[:, pl.ds(off + l * LANES, LANES)]
            fin = s > NEG_INF
            mx = jnp.maximum(mx, s)
            mn = jnp.minimum(mn, jnp.where(fin, s, POS_INF))
            nf = nf + jnp.where(fin, 1.0, 0.0)
        return mx, mn, nf

    mx, mn, nf = lax.fori_loop(
        0, nkb, stats_tile,
        (jnp.full((tq, LANES), NEG_INF, jnp.float32), jnp.full((tq, LANES), POS_INF, jnp.float32),
         jnp.zeros((tq, LANES), jnp.float32)))
    mx = jnp.max(mx, axis=1, keepdims=True)
    mn = jnp.min(mn, axis=1, keepdims=True)
    nf = jnp.sum(nf, axis=1, keepdims=True)

    def count(v, strict):
        def tile(j, cnt):
            off = pl.multiple_of(j * kb, kb)
            for l in range(lane_tiles):
                s = s_ref[:, pl.ds(off + l * LANES, LANES)]
                cnt = cnt + jnp.where(s > v if strict else s >= v, 1.0, 0.0)
            return cnt
        cnt = lax.fori_loop(0, nkb, tile, jnp.zeros((tq, LANES), jnp.float32))
        return jnp.sum(cnt, axis=1, keepdims=True)

    thr, taken, any_tie = _kth_threshold(count, mn, mx, nf, k_sel)
    thr_ref[...] = jnp.broadcast_to(thr, (tq, LANES))
    taken_ref[...] = jnp.broadcast_to(taken, (tq, LANES))
    tie_ref[b * pl.num_programs(1) + i] = jnp.where(any_tie, 1, 0).astype(jnp.int32)


def _index_scores(qir, wis, qc, kit, kc, nkb, *, row0, batches, nq, tq, kb, k_sel):
    sk = kit.shape[-1]
    rb0 = row0 // tq
    grid_spec = pltpu.PrefetchScalarGridSpec(
        num_scalar_prefetch=1,
        grid=(batches, nq),
        in_specs=[pl.BlockSpec((tq, IDX_HEADS * IDX_DIM), lambda b, i, n: (rb0 + b * nq + i, 0)),
                  pl.BlockSpec((tq, LANES), lambda b, i, n: (rb0 + b * nq + i, 0)),
                  pl.BlockSpec((tq, 1), lambda b, i, n: (rb0 + b * nq + i, 0)),
                  pl.BlockSpec((None, IDX_DIM, sk), lambda b, i, n: (b, 0, 0)),
                  pl.BlockSpec((None, 1, sk), lambda b, i, n: (b, 0, 0))],
        out_specs=[pl.BlockSpec((None, tq, sk), lambda b, i, n: (b, i, 0)),
                   pl.BlockSpec((None, tq, LANES), lambda b, i, n: (b, i, 0)),
                   pl.BlockSpec((None, tq, LANES), lambda b, i, n: (b, i, 0)),
                   pl.BlockSpec(memory_space=pltpu.SMEM)],
        scratch_shapes=[pltpu.VMEM((IDX_HEADS, tq, LANES), jnp.float32)],
    )
    return pl.pallas_call(
        functools.partial(_index_kernel, tq=tq, kb=kb, sk=sk, k_sel=k_sel),
        grid_spec=grid_spec,
        out_shape=[jax.ShapeDtypeStruct((batches, nq * tq, sk), jnp.float32),
                   jax.ShapeDtypeStruct((batches, nq * tq, LANES), jnp.float32),
                   jax.ShapeDtypeStruct((batches, nq * tq, LANES), jnp.float32),
                   jax.ShapeDtypeStruct((batches * nq,), jnp.int32)],
        compiler_params=_params(("arbitrary", "arbitrary")),
        name="index_select",
    )(nkb, qir, wis, qc, kit, kc)


def _fold8(x, op):
    parts = [x[g * 8:(g + 1) * 8, :] for g in range(x.shape[0] // 8)]
    while len(parts) > 1:
        parts = [op(parts[a], parts[a + 1]) for a in range(0, len(parts) - 1, 2)] + (parts[-1:] if len(parts) % 2 else [])
    return parts[0]


def _index_kernel_t(qit_ref, wit_ref, ki_ref, st_ref, thr_ref, taken_ref, tie_ref, *, tq, kb, k_sel):
    i = pl.program_id(0)
    nkb = lax.div((i + 1) * tq + (kb - 1), kb)
    total = st_ref.shape[0] // kb
    qchunk = lax.shift_right_logical(i * tq + lax.broadcasted_iota(jnp.int32, (kb, tq), 1), CHUNK_SHIFT)
    krow = lax.broadcasted_iota(jnp.int32, (kb, tq), 0)

    def score_tile(j, carry):
        mx, mn, nf = carry
        off = pl.multiple_of(j * kb, kb)
        k_t = ki_ref[pl.ds(off, kb), :]
        acc = jnp.zeros((kb, tq), jnp.float32)
        for h in range(IDX_HEADS):
            r = jnp.dot(k_t, qit_ref[h * IDX_DIM:(h + 1) * IDX_DIM, :], preferred_element_type=jnp.float32)
            acc = acc + jnp.maximum(r, 0.0) * wit_ref[h:h + 1, :]
        adm = lax.shift_right_logical(off + krow, CHUNK_SHIFT) <= qchunk
        s = jnp.where(adm, acc, NEG_INF)
        st_ref[pl.ds(off, kb), :] = s
        fin = s > NEG_INF
        mx = jnp.maximum(mx, _fold8(s, jnp.maximum))
        mn = jnp.minimum(mn, _fold8(jnp.where(fin, s, POS_INF), jnp.minimum))
        nf = nf + _fold8(jnp.where(fin, 1.0, 0.0), jnp.add)
        return mx, mn, nf

    mx, mn, nf = lax.fori_loop(
        0, nkb, score_tile,
        (jnp.full((8, tq), NEG_INF, jnp.float32), jnp.full((8, tq), POS_INF, jnp.float32),
         jnp.zeros((8, tq), jnp.float32)))

    def fill_tile(j, carry):
        off = pl.multiple_of(j * kb, kb)
        st_ref[pl.ds(off, kb), :] = jnp.full((kb, tq), NEG_INF, jnp.float32)
        return carry

    lax.fori_loop(nkb, total, fill_tile, 0)
    mx = jnp.max(mx, axis=0, keepdims=True)
    mn = jnp.min(mn, axis=0, keepdims=True)
    nf = jnp.sum(nf, axis=0, keepdims=True)

    def count(v, strict):
        def tile(j, cnt):
            off = pl.multiple_of(j * kb, kb)
            s = st_ref[pl.ds(off, kb), :]
            return cnt + _fold8(jnp.where(s > v if strict else s >= v, 1.0, 0.0), jnp.add)
        cnt = lax.fori_loop(0, nkb, tile, jnp.zeros((8, tq), jnp.float32))
        return jnp.sum(cnt, axis=0, keepdims=True)

    thr, taken, any_tie = _kth_threshold(count, mn, mx, nf, k_sel)
    thr_ref[...] = jnp.broadcast_to(thr, (8, tq))
    taken_ref[...] = jnp.broadcast_to(taken, (8, tq))
    tie_ref[i] = jnp.where(any_tie, 1, 0).astype(jnp.int32)


def _index_scores_t(qirt, wist, kirb, *, tq, kb, k_sel):
    idx_w, seq = qirt.shape
    nq = seq // tq
    return pl.pallas_call(
        functools.partial(_index_kernel_t, tq=tq, kb=kb, k_sel=k_sel),
        grid=(nq,),
        in_specs=[pl.BlockSpec((idx_w, tq), lambda i: (0, i)),
                  pl.BlockSpec((LANES, tq), lambda i: (0, i)),
                  pl.BlockSpec((seq, IDX_DIM), lambda i: (0, 0))],
        out_specs=[pl.BlockSpec((seq, tq), lambda i: (0, i)),
                   pl.BlockSpec((8, tq), lambda i: (0, i)),
                   pl.BlockSpec((8, tq), lambda i: (0, i)),
                   pl.BlockSpec(memory_space=pltpu.SMEM)],
        out_shape=[jax.ShapeDtypeStruct((seq, seq), jnp.float32),
                   jax.ShapeDtypeStruct((8, seq), jnp.float32),
                   jax.ShapeDtypeStruct((8, seq), jnp.float32),
                   jax.ShapeDtypeStruct((nq,), jnp.int32)],
        compiler_params=_params(("arbitrary",)),
        name="index_select_t",
    )(qirt, wist, kirb)


def _tie_rank_matrix(n, lower):
    rows = lax.broadcasted_iota(jnp.int32, (n, n), 0)
    cols = lax.broadcasted_iota(jnp.int32, (n, n), 1)
    return jnp.where(cols < rows if lower else rows < cols, 1.0, 0.0).astype(jnp.bfloat16)


def _attn_kernel_t(qi_ref, kj_ref, nk_ref, tie_ref, qt_ref, zb_ref, k_ref, vt_ref, st_ref, thr_ref, taken_ref,
                   qn_ref, kn_ref, o_ref, m_ref, c_ref, acc_ref, s_ref, p_ref, bias_ref, seen_ref, *, n_heads):
    p = pl.program_id(0)
    kj = kj_ref[p]
    hpg = n_heads // N_KV

    @pl.when(kj == 0)
    def _():
        m_ref[...] = jnp.full(m_ref.shape, NEG_INF, jnp.float32)
        c_ref[...] = jnp.zeros(c_ref.shape, jnp.float32)
        acc_ref[...] = jnp.zeros(acc_ref.shape, jnp.float32)
        seen_ref[...] = jnp.zeros(seen_ref.shape, jnp.float32)

    kb = st_ref.shape[0]
    rc = ATTN_ROW_CHUNK
    thr = thr_ref[0:1, :]
    for r in range(0, kb, rc):
        bias_ref[r:r + rc, :] = jnp.where(st_ref[r:r + rc, :] >= thr, 0.0, NEG_INF)

    @pl.when(tie_ref[qi_ref[p]] != 0)
    def _():
        taken = taken_ref[0:1, :]
        ltri = _tie_rank_matrix(RANK_CHUNK, True)
        seen = seen_ref[...]
        for r in range(0, kb, RANK_CHUNK):
            s = st_ref[r:r + RANK_CHUNK, :]
            eq = s == thr
            rank = jnp.dot(ltri, jnp.where(eq, 1.0, 0.0).astype(jnp.bfloat16),
                           preferred_element_type=jnp.float32) + seen
            take = jnp.logical_or(s > thr, jnp.logical_and(eq, rank < taken))
            bias_ref[r:r + RANK_CHUNK, :] = jnp.where(take, 0.0, NEG_INF)
            seen = seen + jnp.sum(jnp.where(eq, 1.0, 0.0), axis=0, keepdims=True)
        seen_ref[...] = seen

    def logits(h, r):
        g = h // hpg
        return jnp.dot(k_ref[r:r + rc, g * HEAD_DIM:(g + 1) * HEAD_DIM], qt_ref[h * HEAD_DIM:(h + 1) * HEAD_DIM, :],
                       preferred_element_type=jnp.float32) + bias_ref[r:r + rc, :]

    def weighted_values(h):
        g = h // hpg
        return jnp.dot(vt_ref[g * VT_ROWS:(g + 1) * VT_ROWS, :], p_ref[h], preferred_element_type=jnp.float32)

    first = kj == 0
    shift0 = [jnp.where(m_ref[h] == NEG_INF, 0.0, m_ref[h]) for h in range(n_heads)]
    kn = kn_ref[0]
    for blk in range(1, kn_ref.shape[0]):
        kn = jnp.maximum(kn, kn_ref[blk])
    excess = None
    for h in range(n_heads):
        bound = qn_ref[h:h + 1, :] * kn[0:1, h // hpg:h // hpg + 1]
        e = bound - shift0[h]
        excess = e if excess is None else jnp.maximum(excess, e)
    in_range = jnp.max(excess) <= EXP_RANGE

    @pl.when(in_range)
    def _():
        for h in range(n_heads):
            cm = None
            for r in range(0, kb, rc):
                t = logits(h, r) - shift0[h]
                p_ref[h, r:r + rc, :] = jnp.exp2(t).astype(p_ref.dtype)
                c8 = _fold8(t, jnp.maximum)
                cm = c8 if cm is None else jnp.maximum(cm, c8)
            m_ref[h] = jnp.maximum(m_ref[h], shift0[h] + jnp.max(cm, axis=0, keepdims=True))
        for h in range(n_heads):
            alpha = jnp.where(first, 1.0, jnp.exp2(c_ref[h] - shift0[h]))
            acc_ref[h] = alpha * acc_ref[h] + weighted_values(h)
            c_ref[h] = shift0[h]

    @pl.when(jnp.logical_not(in_range))
    def _():
        cmax = []
        for h in range(n_heads):
            cm = None
            for r in range(0, kb, rc):
                s = logits(h, r)
                s_ref[h, r:r + rc, :] = s
                c8 = _fold8(s, jnp.maximum)
                cm = c8 if cm is None else jnp.maximum(cm, c8)
            cmax.append(jnp.max(cm, axis=0, keepdims=True))
        for h in range(n_heads):
            m_cur = jnp.maximum(m_ref[h], cmax[h])
            shift = jnp.where(m_cur == NEG_INF, 0.0, m_cur)
            for r in range(0, kb, rc):
                p_ref[h, r:r + rc, :] = jnp.exp2(s_ref[h, r:r + rc, :] - shift).astype(p_ref.dtype)
            alpha = jnp.where(first, 1.0, jnp.exp2(c_ref[h] - shift))
            acc_ref[h] = alpha * acc_ref[h] + weighted_values(h)
            c_ref[h] = shift
            m_ref[h] = m_cur

    @pl.when(kj == nk_ref[p] - 1)
    def _():
        for h in range(n_heads):
            cs = slice(h * HEAD_DIM, (h + 1) * HEAD_DIM)
            o = (acc_ref[h, 0:HEAD_DIM, :] / acc_ref[h, HEAD_DIM:HEAD_DIM + 1, :]).T
            o_ref[:, cs] = (o * _silu(zb_ref[:, cs])).astype(o_ref.dtype)


def _masked_attention_t(qrt, h_main, zb_col, krb, vt, st, thr, taken, ties, qn, kn, steps, *, tq, kb):
    qidx, kidx, nk = steps
    d_b, seq = qrt.shape
    n_heads = d_b // HEAD_DIM
    kv_w = N_KV * HEAD_DIM
    kn_blocks = kb * kn.shape[0] // seq
    assert kb % RANK_CHUNK == 0 and kb % ATTN_ROW_CHUNK == 0 and kn_blocks >= 1
    grid_spec = pltpu.PrefetchScalarGridSpec(
        num_scalar_prefetch=4,
        grid=(qidx.shape[0],),
        in_specs=[pl.BlockSpec((d_b, tq), lambda p, q, k, n, t: (0, q[p])),
                  pl.BlockSpec((tq, d_b), lambda p, q, k, n, t: (q[p], zb_col)),
                  pl.BlockSpec((kb, kv_w), lambda p, q, k, n, t: (k[p], 0)),
                  pl.BlockSpec((N_KV * VT_ROWS, kb), lambda p, q, k, n, t: (0, k[p])),
                  pl.BlockSpec((kb, tq), lambda p, q, k, n, t: (k[p], q[p])),
                  pl.BlockSpec((8, tq), lambda p, q, k, n, t: (0, q[p])),
                  pl.BlockSpec((8, tq), lambda p, q, k, n, t: (0, q[p])),
                  pl.BlockSpec((n_heads, tq), lambda p, q, k, n, t: (0, q[p])),
                  pl.BlockSpec((kn_blocks, 8, LANES), lambda p, q, k, n, t: (k[p], 0, 0))],
        out_specs=pl.BlockSpec((tq, d_b), lambda p, q, k, n, t: (q[p], 0)),
        scratch_shapes=[pltpu.VMEM((n_heads, 1, tq), jnp.float32),
                        pltpu.VMEM((n_heads, 1, tq), jnp.float32),
                        pltpu.VMEM((n_heads, VT_ROWS, tq), jnp.float32),
                        pltpu.VMEM((n_heads, kb, tq), jnp.float32),
                        pltpu.VMEM((n_heads, kb, tq), jnp.bfloat16),
                        pltpu.VMEM((kb, tq), jnp.float32),
                        pltpu.VMEM((1, tq), jnp.float32)],
    )
    return pl.pallas_call(
        functools.partial(_attn_kernel_t, n_heads=n_heads),
        grid_spec=grid_spec,
        out_shape=jax.ShapeDtypeStruct((seq, d_b), jnp.bfloat16),
        compiler_params=_params(("arbitrary",)),
        name="masked_attention_t",
    )(qidx, kidx, nk, ties, qrt, h_main, krb, vt, st, thr, taken, qn, kn)


def _attn_kernel(tie_ref, q_ref, zb_ref, kt_ref, v_ref, s_ref, thr_ref, taken_ref, o_ref, bias_ref, *, n_heads):
    b = pl.program_id(0)
    hpg = n_heads // N_KV
    sk = s_ref.shape[1]
    thr = thr_ref[:, 0:1]
    bias_ref[...] = jnp.where(s_ref[...] >= thr, 0.0, NEG_INF)

    @pl.when(tie_ref[b] != 0)
    def _():
        taken = taken_ref[:, 0:1]
        utri = _tie_rank_matrix(RANK_CHUNK, False)
        seen = jnp.zeros_like(thr)
        for c in range(0, sk, RANK_CHUNK):
            s = s_ref[:, c:c + RANK_CHUNK]
            eq = s == thr
            rank = jnp.dot(jnp.where(eq, 1.0, 0.0).astype(jnp.bfloat16), utri,
                           preferred_element_type=jnp.float32) + seen
            take = jnp.logical_or(s > thr, jnp.logical_and(eq, rank < taken))
            bias_ref[:, c:c + RANK_CHUNK] = jnp.where(take, 0.0, NEG_INF)
            seen = seen + jnp.sum(jnp.where(eq, 1.0, 0.0), axis=1, keepdims=True)

    for h in range(n_heads):
        g = h // hpg
        cs = slice(h * HEAD_DIM, (h + 1) * HEAD_DIM)
        s = jnp.dot(q_ref[:, cs], kt_ref[g], preferred_element_type=jnp.float32) + bias_ref[...]
        pr = jnp.exp2(s - jnp.max(s, axis=1, keepdims=True))
        o = jnp.dot(pr.astype(v_ref.dtype), v_ref[:, g * HEAD_DIM:(g + 1) * HEAD_DIM],
                    preferred_element_type=jnp.float32) / jnp.sum(pr, axis=1, keepdims=True)
        o_ref[:, cs] = (o * _silu(zb_ref[:, cs])).astype(o_ref.dtype)


def _masked_attention(qr, h_main, zb_col, kt, vb, scores, thr, taken, ties, *, zrow0, tq, d_b):
    batches, _, _, sk = kt.shape
    zrb0 = zrow0 // tq
    n_heads = d_b // HEAD_DIM
    kv_w = N_KV * HEAD_DIM
    assert sk % RANK_CHUNK == 0
    grid_spec = pltpu.PrefetchScalarGridSpec(
        num_scalar_prefetch=1,
        grid=(batches,),
        in_specs=[pl.BlockSpec((tq, d_b), lambda b, t: (b, 0)),
                  pl.BlockSpec((tq, d_b), lambda b, t: (zrb0 + b, zb_col)),
                  pl.BlockSpec((None, N_KV, HEAD_DIM, sk), lambda b, t: (b, 0, 0, 0)),
                  pl.BlockSpec((None, sk, kv_w), lambda b, t: (b, 0, 0)),
                  pl.BlockSpec((None, tq, sk), lambda b, t: (b, 0, 0)),
                  pl.BlockSpec((None, tq, LANES), lambda b, t: (b, 0, 0)),
                  pl.BlockSpec((None, tq, LANES), lambda b, t: (b, 0, 0))],
        out_specs=pl.BlockSpec((tq, d_b), lambda b, t: (b, 0)),
        scratch_shapes=[pltpu.VMEM((tq, sk), jnp.float32)],
    )
    return pl.pallas_call(
        functools.partial(_attn_kernel, n_heads=n_heads),
        grid_spec=grid_spec,
        out_shape=jax.ShapeDtypeStruct((batches * tq, d_b), jnp.bfloat16),
        compiler_params=_params(("arbitrary",)),
        name="masked_attention",
    )(ties, qr, h_main, kt, vb, scores, thr, taken)


def _out_kernel(mixa_ref, mixb_ref, w_ref, x_ref, g_ref, o_ref, ssq_ref, *, d_a, tn):
    j = pl.program_id(1)

    @pl.when(j == 0)
    def _():
        ssq_ref[...] = jnp.zeros(ssq_ref.shape, jnp.float32)

    y = (x_ref[...] + jnp.dot(mixa_ref[...], w_ref[0:d_a, :], preferred_element_type=jnp.float32)
         + jnp.dot(mixb_ref[...], w_ref[d_a:, :], preferred_element_type=jnp.float32))
    o_ref[:, pl.ds(pl.multiple_of(j * tn, tn), tn)] = y
    ssq_ref[...] += jnp.sum(y * y, axis=-1, keepdims=True)

    @pl.when(j == pl.num_programs(1) - 1)
    def _():
        inv = lax.rsqrt(ssq_ref[...] * (1.0 / o_ref.shape[1]) + EPS)
        o_ref[...] = o_ref[...] * inv * g_ref[...]


def _out_proj(mix_a, mix_b, w, x, g):
    n, d = x.shape
    d_a, d_b = mix_a.shape[1], mix_b.shape[1]
    tm = _largest_divisor(n, (512, 256, 128))
    tn = _largest_divisor(d, (1024, 512, 256, 128))
    return pl.pallas_call(
        functools.partial(_out_kernel, d_a=d_a, tn=tn),
        grid=(n // tm, d // tn),
        in_specs=[pl.BlockSpec((tm, d_a), lambda i, j: (i, 0)),
                  pl.BlockSpec((tm, d_b), lambda i, j: (i, 0)),
                  pl.BlockSpec((d_a + d_b, tn), lambda i, j: (0, j)),
                  pl.BlockSpec((tm, tn), lambda i, j: (i, j)), pl.BlockSpec((1, d), lambda i, j: (0, 0))],
        out_specs=pl.BlockSpec((tm, d), lambda i, j: (i, 0)),
        out_shape=jax.ShapeDtypeStruct((n, d), jnp.float32),
        scratch_shapes=[pltpu.VMEM((tm, 1), jnp.float32)],
        compiler_params=_params(("arbitrary", "arbitrary")),
        name="out_proj",
    )(mix_a, mix_b, w, x, g.reshape(1, d))


def _cdiv(a, b):
    return -(-a // b)


def _attention_steps(nq, tq, kb):
    q_l, k_l, n_l = [], [], []
    for q in range(nq):
        nk = _cdiv((q + 1) * tq, kb)
        for k in range(nk):
            q_l.append(q), k_l.append(k), n_l.append(nk)
    return tuple(jnp.asarray(np.asarray(v, np.int32)) for v in (q_l, k_l, n_l))


def _layer(x_prompt, x_sample, cache_k, cache_v, cache_ki, g_norm, w_in, w_s, b_s, g_v, w_out, final_g):
    f32, bf = jnp.float32, jnp.bfloat16
    _, seq, d_model = x_prompt.shape
    dec_b, dec_t, _ = x_sample.shape
    past = cache_k.shape[1]
    d_a = d_model // 2
    d_b = d_model - d_a
    groups = d_a // HEAD_DIM
    kv_w = N_KV * HEAD_DIM
    idx_w = IDX_HEADS * IDX_DIM
    n_s = dec_b * dec_t
    n_all = seq + n_s
    assert seq % 512 == 0 and n_s % MLP_CHUNK == 0 and dec_t <= CHUNK and past % MLP_CHUNK == 0
    assert idx_w % d_a == 0 and d_a % HEAD_DIM == 0

    x_p, x_s = x_prompt[0], x_sample.reshape(n_s, d_model)
    widths = (d_a, d_a, d_a, d_b, kv_w, kv_w, d_b, idx_w, IDX_DIM, IDX_HEADS)
    o_u, o_va, o_za, o_q, o_k, o_v, o_zb, o_qi, o_ki, o_wi = np.concatenate([[0], np.cumsum(widths)])[:-1].tolist()

    def col_blocks(segments, bw):
        return [(off + c) // bw for off, wd in segments for c in range(0, wd, bw)]

    w_in_t = w_in.T
    bw = int(np.gcd.reduce([d_a, d_b, kv_w, idx_w, o_q, o_zb, o_qi]))
    w_main = _regroup_cast(w_in_t, col_blocks(((o_qi, idx_w), (o_q, d_b), (o_u, d_a), (o_va, d_a), (o_za, d_a),
                                               (o_zb, d_b)), bw), bw, bf)
    assert o_k % LANES == 0 and o_ki % LANES == 0 and o_wi % LANES == 0
    w_small = _regroup_cast(w_in_t, col_blocks(((o_k, 2 * kv_w), (o_ki, IDX_DIM), (o_wi, LANES)), LANES), LANES, bf,
                            tail_valid=IDX_HEADS)
    q_col = idx_w // d_a
    u_col, zb_col = q_col + 1, q_col + 4

    half = HEAD_DIM // 2
    inv = ROPE_THETA ** (-2.0 * jnp.arange(half, dtype=f32) / HEAD_DIM)

    def rope_tables(pos):
        ang = pos.astype(f32)[:, None] * inv[None, :]
        return (jnp.concatenate([jnp.cos(ang), jnp.cos(ang)], axis=1),
                jnp.concatenate([-jnp.sin(ang), jnp.sin(ang)], axis=1))

    pos_s = past + jnp.tile(jnp.arange(dec_t), dec_b)
    cos_p, sin_p = rope_tables(jnp.arange(seq))
    cos_s, sin_s = rope_tables(pos_s)

    hr_p, xn_p = _in_proj(x_p, g_norm, w_main, f32, row0=idx_w + d_b, emit_xn=True)
    (qirt,) = _qproj(xn_p, w_main, 0, idx_w, cos_p, sin_p, scale=1.0, with_norms=False)
    qrt, qn_p = _qproj(xn_p, w_main, idx_w, d_b, cos_p, sin_p, scale=HEAD_DIM ** -0.5 * LOG2_E, with_norms=True)
    hs_p = _xn_proj(xn_p, w_small, f32)
    hm_s, hs_s = _in_proj(x_s, g_norm, w_main, f32), _in_proj(x_s, g_norm, w_small, f32)

    pidx = np.arange(MLP_CHUNK)
    mask_p = (pidx[None, :] // CHUNK) <= (pidx[:, None] // CHUNK)
    wm_p = jnp.where(mask_p[None], w_s, 0.0).astype(bf)
    b_p = b_s[:, :, None]
    tidx = pidx % dec_t
    same = (pidx[None, :] // dec_t) == (pidx[:, None] // dec_t)
    mask_s = same & ((tidx[None, :] // CHUNK) <= (tidx[:, None] // CHUNK))
    wm_s = jnp.where(mask_s[None], w_s[:, tidx][:, :, tidx], 0.0).astype(bf)
    b_sm = b_s[:, tidx][:, :, None]
    gv = g_v.reshape(1, d_a)
    (mix_a_p,) = _mlp_group(hr_p, 0, d_a, 0, seq, 512, wm_p, b_p, gv, False)
    mix_a_s, vn_s = _mlp_group(hm_s, u_col, d_a, 0, n_s, MLP_CHUNK, wm_s, b_sm, gv, True)

    kr_p, krb_p, vt_p, kir_p, kirb_p, wist, kn_p = _keyprep(hs_p, cos_p, sin_p)
    qir_s, qr_s, kr_s, krb_s, vb_s, kir_s, kirb_s, wis_s = _prep(
        hm_s, hs_s, cos_s, sin_s, d_b, q_col, row0=0, nrows=n_s, transposed=False)
    v_p, v_s_new = hs_p[:, kv_w:2 * kv_w], hs_s[:, kv_w:2 * kv_w]

    tq_p, kb_i, kb_a = PROMPT_TQ, PROMPT_KB_INDEX, PROMPT_KB_ATTN
    k_sel_p = min(TOPK_MAX, seq // 4)
    st_p, thr_p, taken_p, ties_p = _index_scores_t(qirt, wist, kirb_p, tq=tq_p, kb=kb_i, k_sel=k_sel_p)
    steps_p = _attention_steps(seq // tq_p, tq_p, kb_a)
    mix_b_p = _masked_attention_t(qrt, hr_p, 3, krb_p, vt_p, st_p, thr_p, taken_p, ties_p, qn_p, kn_p,
                                  steps_p, tq=tq_p, kb=kb_a)

    kb_s = SAMPLE_KB_INDEX
    n_keys = past + dec_t
    sk_s = _cdiv(n_keys, kb_s) * kb_s
    padk = lambda a: jnp.pad(a, ((0, 0), (0, sk_s - n_keys), (0, 0)))
    k_s = padk(jnp.concatenate([cache_k.reshape(dec_b, past, kv_w).astype(bf), krb_s.reshape(dec_b, dec_t, kv_w)], axis=1))
    v_s = padk(jnp.concatenate([cache_v.reshape(dec_b, past, kv_w).astype(bf), vb_s.reshape(dec_b, dec_t, kv_w)], axis=1))
    ki_s = padk(jnp.concatenate([cache_ki.astype(bf), kirb_s.reshape(dec_b, dec_t, IDX_DIM)], axis=1))
    kit_s = ki_s.transpose(0, 2, 1)
    kt_s = k_s.reshape(dec_b, sk_s, N_KV, HEAD_DIM).transpose(0, 2, 3, 1)
    kpos = jnp.arange(sk_s, dtype=jnp.int32)
    kc_s = jnp.broadcast_to(jnp.where(kpos < n_keys, kpos // CHUNK, FAR_CHUNK)[None, None, :], (dec_b, 1, sk_s))
    qc_s = (pos_s // CHUNK).astype(jnp.int32)[:, None]
    k_sel_s = min(TOPK_MAX, n_keys // 4)
    nkb_s = jnp.full((dec_b,), sk_s // kb_s, jnp.int32)
    s_s, thr_s, taken_s, ties_s = _index_scores(qir_s, wis_s, qc_s, kit_s, kc_s, nkb_s, row0=0, batches=dec_b, nq=1,
                                                tq=dec_t, kb=kb_s, k_sel=k_sel_s)
    mix_b_s = _masked_attention(qr_s, hm_s, zb_col, kt_s, v_s, s_s, thr_s, taken_s, ties_s,
                                zrow0=0, tq=dec_t, d_b=d_b)

    w_out_b = w_out.astype(bf)
    y_p = _out_proj(mix_a_p, mix_b_p, w_out_b, x_p, final_g)
    y_s = _out_proj(mix_a_s, mix_b_s, w_out_b, x_s, final_g)

    shp_p = (1, 1, seq, N_KV, HEAD_DIM)
    shp_s = (1, dec_b, dec_t, N_KV, HEAD_DIM)
    return (y_p[None], y_s.reshape(dec_b, dec_t, d_model),
            kr_p.reshape(shp_p), v_p.reshape(shp_p), kir_p.reshape(1, 1, seq, IDX_DIM),
            kr_s.reshape(shp_s), v_s_new.reshape(shp_s), kir_s.reshape(1, dec_b, dec_t, IDX_DIM),
            vn_s.reshape(1, dec_b, dec_t, groups, HEAD_DIM))


def kernel(x_prompt, x_sample, cache_k, cache_v, cache_idx_k, norm_g, w_in, w_s, b_s, v_norm_g, w_out, final_norm_g):
    assert x_prompt.shape[0] == 1 and norm_g.shape[0] == 1, "one prompt stream and one layer"
    return _layer(x_prompt, x_sample, cache_k[0], cache_v[0], cache_idx_k[0], norm_g[0], w_in[0], w_s[0], b_s[0],
                  v_norm_g[0], w_out[0], final_norm_g)
```

```python
import functools

import numpy as np
import jax
import jax.numpy as jnp
from jax import lax
from jax.experimental import pallas as pl
from jax.experimental.pallas import tpu as pltpu

CHUNK = 64
MLP_CHUNK = 128
HEAD_DIM = 128
N_KV = 4
IDX_HEADS = 32
IDX_DIM = 128
TOPK_MAX = 256
ROPE_THETA = 10000.0
EPS = 1e-6

LANES = 128
VMEM_LIMIT = 56 * 1024 * 1024
NEG_INF = float("-inf")
POS_INF = float("inf")
FAR_CHUNK = 1 << 20
LOG2_E = 1.4426950408889634
CHUNK_SHIFT = 6
PROMPT_TQ, PROMPT_KB_INDEX, PROMPT_KB_ATTN = 256, 256, 1024
ONES_ROWS = 16
VT_ROWS = HEAD_DIM + ONES_ROWS
ATTN_ROW_CHUNK = 128
TAKE_ALL = 1.0e9
RANK_CHUNK = 256
EXP_RANGE = 60.0
NORM_MARGIN = 1.001
SAMPLE_KB_INDEX = 256


def _largest_divisor(n, candidates):
    for c in candidates:
        if n % c == 0:
            return c
    raise ValueError(f"no tile in {candidates} divides {n}")


def _params(sem, flags=None):
    return pltpu.CompilerParams(dimension_semantics=sem, vmem_limit_bytes=VMEM_LIMIT, flags=flags)


def _cast_kernel(src_ref, w_ref, o_ref, *, bw, tail_valid):
    w = w_ref[...]
    if tail_valid < bw:
        row = lax.broadcasted_iota(jnp.int32, w.shape, 0)
        last = pl.program_id(0) == pl.num_programs(0) - 1
        w = jnp.where(jnp.logical_and(last, row >= tail_valid), 0.0, w)
    o_ref[...] = w.astype(o_ref.dtype)


def _regroup_cast(wt, src_blocks, bw, out_dtype, tail_valid=None):
    d = wt.shape[1]
    tail_valid = bw if tail_valid is None else tail_valid
    src = jnp.asarray(np.asarray(src_blocks, np.int32))
    grid_spec = pltpu.PrefetchScalarGridSpec(
        num_scalar_prefetch=1,
        grid=(len(src_blocks),),
        in_specs=[pl.BlockSpec((bw, d), lambda j, s: (s[j], 0))],
        out_specs=pl.BlockSpec((bw, d), lambda j, s: (j, 0)),
    )
    return pl.pallas_call(
        functools.partial(_cast_kernel, bw=bw, tail_valid=tail_valid),
        grid_spec=grid_spec,
        out_shape=jax.ShapeDtypeStruct((len(src_blocks) * bw, d), out_dtype),
        compiler_params=_params(("arbitrary",)),
        name="regroup_cast",
    )(src, wt)


def _in_proj_kernel(x_ref, g_ref, wt_ref, o_ref, xn_ref):
    @pl.when(pl.program_id(1) == 0)
    def _():
        x = x_ref[...]
        ms = jnp.mean(x * x, axis=-1, keepdims=True)
        xn_ref[...] = (x * lax.rsqrt(ms + EPS) * g_ref[...]).astype(xn_ref.dtype)

    o_ref[...] = lax.dot_general(xn_ref[...], wt_ref[...], (((1,), (1,)), ((), ())),
                                 preferred_element_type=jnp.float32).astype(o_ref.dtype)


def _in_proj(x, g, wt, out_dtype, row0=0, emit_xn=False):
    m, k = x.shape
    n = wt.shape[0] - row0
    tm = _largest_divisor(m, (512, 256, 128))
    tn = _largest_divisor(int(np.gcd(n, row0)), (1024, 1280, 512, 256, 128))
    out_specs = [pl.BlockSpec((tm, tn), lambda i, j: (i, j))]
    out_shape = [jax.ShapeDtypeStruct((m, n), out_dtype)]
    scratch = [pltpu.VMEM((tm, k), wt.dtype)]
    if emit_xn:
        out_specs.append(pl.BlockSpec((tm, k), lambda i, j: (i, 0)))
        out_shape.append(jax.ShapeDtypeStruct((m, k), wt.dtype))
        scratch = []
    res = pl.pallas_call(
        _in_proj_kernel,
        grid=(m // tm, n // tn),
        in_specs=[pl.BlockSpec((tm, k), lambda i, j: (i, 0)), pl.BlockSpec((1, k), lambda i, j: (0, 0)),
                  pl.BlockSpec((tn, k), lambda i, j: (row0 // tn + j, 0))],
        out_specs=out_specs,
        out_shape=out_shape,
        scratch_shapes=scratch,
        compiler_params=_params(("arbitrary", "arbitrary")),
        name="in_proj",
    )(x, g.reshape(1, k), wt)
    return res if emit_xn else res[0]


def _xn_proj_kernel(xn_ref, wt_ref, o_ref):
    o_ref[...] = lax.dot_general(xn_ref[...], wt_ref[...], (((1,), (1,)), ((), ())),
                                 preferred_element_type=jnp.float32).astype(o_ref.dtype)


def _xn_proj(xn, wt, out_dtype):
    m, k = xn.shape
    n = wt.shape[0]
    tm = _largest_divisor(m, (512, 256, 128))
    tn = _largest_divisor(n, (1024, 1280, 512, 256, 128))
    return pl.pallas_call(
        _xn_proj_kernel,
        grid=(m // tm, n // tn),
        in_specs=[pl.BlockSpec((tm, k), lambda i, j: (i, 0)), pl.BlockSpec((tn, k), lambda i, j: (j, 0))],
        out_specs=pl.BlockSpec((tm, tn), lambda i, j: (i, j)),
        out_shape=jax.ShapeDtypeStruct((m, n), out_dtype),
        compiler_params=_params(("arbitrary", "arbitrary")),
        name="kv_proj",
    )(xn, wt)


def _silu(z):
    return z * (1.0 / (1.0 + jnp.exp(-z)))


def _mlp_kernel(u_ref, va_ref, za_ref, wm_ref, b_ref, gv_ref, mix_ref, *vn_refs, rows, groups):
    for c in range(rows // MLP_CHUNK):
        rs = slice(c * MLP_CHUNK, (c + 1) * MLP_CHUNK)
        for g in range(groups):
            cs = slice(g * HEAD_DIM, (g + 1) * HEAD_DIM)
            va = va_ref[rs, cs]
            mu = jnp.mean(va, axis=-1, keepdims=True)
            cen = va - mu
            var = jnp.mean(cen * cen, axis=-1, keepdims=True)
            vn = cen * lax.rsqrt(var + EPS) * gv_ref[:, cs]
            if vn_refs:
                vn_refs[0][rs, cs] = vn
            mixed = jnp.dot(wm_ref[g], vn.astype(jnp.bfloat16), preferred_element_type=jnp.float32) + b_ref[g]
            a = u_ref[rs, cs] * mixed
            mix_ref[rs, cs] = (a * _silu(za_ref[rs, cs])).astype(mix_ref.dtype)


def _mlp_group(h_main, col0, d_a, row0, nrows, rows, wm, bias, gv, emit_vn):
    groups = d_a // HEAD_DIM
    rb0 = row0 // rows
    out_shape = [jax.ShapeDtypeStruct((nrows, d_a), jnp.bfloat16)]
    out_specs = [pl.BlockSpec((rows, d_a), lambda i: (i, 0))]
    if emit_vn:
        out_shape.append(jax.ShapeDtypeStruct((nrows, d_a), jnp.float32))
        out_specs.append(pl.BlockSpec((rows, d_a), lambda i: (i, 0)))
    col = lambda c: pl.BlockSpec((rows, d_a), lambda i: (rb0 + i, c))
    return pl.pallas_call(
        functools.partial(_mlp_kernel, rows=rows, groups=groups),
        grid=(nrows // rows,),
        in_specs=[col(col0), col(col0 + 1), col(col0 + 2),
                  pl.BlockSpec((groups, MLP_CHUNK, MLP_CHUNK), lambda i: (0, 0, 0)),
                  pl.BlockSpec((groups, MLP_CHUNK, 1), lambda i: (0, 0, 0)),
                  pl.BlockSpec((1, d_a), lambda i: (0, 0))],
        out_specs=out_specs,
        out_shape=out_shape,
        compiler_params=_params(("arbitrary",)),
        name="chunk_mlp",
    )(h_main, h_main, h_main, wm, bias, gv)


def _rope(x, cos, sin):
    return x * cos + pltpu.roll(x, HEAD_DIM // 2, 1) * sin


def _qproj_kernel(xn_ref, wt_ref, cos_ref, sin_ref, qt_ref, *qn_refs, scale):
    acc = lax.dot_general(xn_ref[...], wt_ref[...], (((1,), (1,)), ((), ())), preferred_element_type=jnp.float32)
    cos, sin = cos_ref[...], sin_ref[...]
    for h in range(acc.shape[1] // HEAD_DIM):
        cs = slice(h * HEAD_DIM, (h + 1) * HEAD_DIM)
        qb = (_rope(acc[:, cs], cos, sin) * scale).T.astype(qt_ref.dtype)
        qt_ref[cs, :] = qb
        if qn_refs:
            qf = qb.astype(jnp.float32)
            qn_refs[0][h:h + 1, :] = jnp.sqrt(jnp.sum(qf * qf, axis=0, keepdims=True)) * NORM_MARGIN


def _qproj(xn, wt, row0, nrows_w, cos_t, sin_t, *, scale, with_norms):
    m, k = xn.shape
    tm = _largest_divisor(m, (512, 256, 128))
    tn = _largest_divisor(int(np.gcd(nrows_w, row0)) if row0 else nrows_w, (1024, 512, 256, 128))
    hpt = tn // HEAD_DIM
    out_specs = [pl.BlockSpec((tn, tm), lambda i, j: (j, i))]
    out_shape = [jax.ShapeDtypeStruct((nrows_w, m), jnp.bfloat16)]
    if with_norms:
        out_specs.append(pl.BlockSpec((hpt, tm), lambda i, j: (j, i)))
        out_shape.append(jax.ShapeDtypeStruct((nrows_w // HEAD_DIM, m), jnp.float32))
    return pl.pallas_call(
        functools.partial(_qproj_kernel, scale=scale),
        grid=(m // tm, nrows_w // tn),
        in_specs=[pl.BlockSpec((tm, k), lambda i, j: (i, 0)),
                  pl.BlockSpec((tn, k), lambda i, j: (row0 // tn + j, 0)),
                  pl.BlockSpec((tm, HEAD_DIM), lambda i, j: (i, 0)), pl.BlockSpec((tm, HEAD_DIM), lambda i, j: (i, 0))],
        out_specs=out_specs,
        out_shape=out_shape,
        compiler_params=_params(("arbitrary", "arbitrary")),
        name="q_proj",
    )(xn, wt, cos_t, sin_t)


def _kvproj_kernel(xn_ref, wt_ref, cos_ref, sin_ref,
                   kr_ref, krb_ref, v_ref, vt_ref, kir_ref, kirb_ref, wist_ref, kn_ref):
    acc = lax.dot_general(xn_ref[...], wt_ref[...], (((1,), (1,)), ((), ())), preferred_element_type=jnp.float32)
    cos, sin = cos_ref[...], sin_ref[...]
    kv_w = N_KV * HEAD_DIM
    lane = lax.broadcasted_iota(jnp.int32, (8, LANES), 1)
    kn = jnp.zeros((8, LANES), jnp.float32)
    for h in range(N_KV):
        cs = slice(h * HEAD_DIM, (h + 1) * HEAD_DIM)
        kr = _rope(acc[:, cs], cos, sin)
        kr_ref[:, cs] = kr
        krb_ref[:, cs] = kr.astype(krb_ref.dtype)
        v = acc[:, kv_w + h * HEAD_DIM:kv_w + (h + 1) * HEAD_DIM]
        v_ref[:, cs] = v
        vt_ref[h * VT_ROWS:h * VT_ROWS + HEAD_DIM, :] = v.T.astype(vt_ref.dtype)
        vt_ref[h * VT_ROWS + HEAD_DIM:(h + 1) * VT_ROWS, :] = jnp.ones((ONES_ROWS, acc.shape[0]), vt_ref.dtype)
        kb16 = kr.astype(krb_ref.dtype).astype(jnp.float32)
        kmax = jnp.sqrt(jnp.max(jnp.sum(kb16 * kb16, axis=1, keepdims=True), axis=0, keepdims=True))
        kn = jnp.where(lane == h, kmax, kn)
    kn_ref[0] = kn
    kir = _rope(acc[:, 2 * kv_w:2 * kv_w + IDX_DIM], cos, sin)
    kir_ref[...] = kir
    kirb_ref[...] = kir.astype(kirb_ref.dtype)
    wi = acc[:, 2 * kv_w + IDX_DIM:2 * kv_w + IDX_DIM + LANES]
    wist_ref[...] = (wi * ((IDX_HEADS * IDX_DIM) ** -0.5)).T


def _kvproj(xn, wt, cos_t, sin_t):
    m, k = xn.shape
    n = wt.shape[0]
    kv_w = N_KV * HEAD_DIM
    assert n == 2 * kv_w + IDX_DIM + LANES
    tm = _largest_divisor(m, (512, 256, 128))
    rowspec = lambda w: pl.BlockSpec((tm, w), lambda i: (i, 0))
    colspec = lambda w: pl.BlockSpec((w, tm), lambda i: (0, i))
    bf, f32, sds = jnp.bfloat16, jnp.float32, jax.ShapeDtypeStruct
    return pl.pallas_call(
        _kvproj_kernel,
        grid=(m // tm,),
        in_specs=[rowspec(k), pl.BlockSpec((n, k), lambda i: (0, 0)), rowspec(HEAD_DIM), rowspec(HEAD_DIM)],
        out_specs=[rowspec(kv_w), rowspec(kv_w), rowspec(kv_w), colspec(N_KV * VT_ROWS), rowspec(IDX_DIM),
                   rowspec(IDX_DIM), colspec(LANES), pl.BlockSpec((1, 8, LANES), lambda i: (i, 0, 0))],
        out_shape=[sds((m, kv_w), f32), sds((m, kv_w), bf), sds((m, kv_w), f32), sds((N_KV * VT_ROWS, m), bf),
                   sds((m, IDX_DIM), f32), sds((m, IDX_DIM), bf), sds((LANES, m), f32),
                   sds((m // tm, 8, LANES), f32)],
        compiler_params=_params(("arbitrary",)),
        name="kv_proj_prep",
    )(xn, wt, cos_t, sin_t)


def _keyprep_kernel(k_ref, v_ref, ki_ref, wi_ref, cos_ref, sin_ref,
                    kr_ref, krb_ref, vt_ref, kir_ref, kirb_ref, wist_ref, kn_ref):
    cos, sin = cos_ref[...], sin_ref[...]
    lane = lax.broadcasted_iota(jnp.int32, (8, LANES), 1)
    kn = jnp.zeros((8, LANES), jnp.float32)
    for h in range(N_KV):
        cs = slice(h * HEAD_DIM, (h + 1) * HEAD_DIM)
        kr = _rope(k_ref[:, cs], cos, sin)
        kr_ref[:, cs] = kr
        krb_ref[:, cs] = kr.astype(krb_ref.dtype)
        vt_ref[h * VT_ROWS:h * VT_ROWS + HEAD_DIM, :] = v_ref[:, cs].T.astype(vt_ref.dtype)
        vt_ref[h * VT_ROWS + HEAD_DIM:(h + 1) * VT_ROWS, :] = jnp.ones((ONES_ROWS, v_ref.shape[0]), vt_ref.dtype)
        kb16 = kr.astype(krb_ref.dtype).astype(jnp.float32)
        kmax = jnp.sqrt(jnp.max(jnp.sum(kb16 * kb16, axis=1, keepdims=True), axis=0, keepdims=True))
        kn = jnp.where(lane == h, kmax, kn)
    kn_ref[0] = kn
    kir = _rope(ki_ref[...], cos, sin)
    kir_ref[...] = kir
    kirb_ref[...] = kir.astype(kirb_ref.dtype)
    wist_ref[...] = (wi_ref[...] * ((IDX_HEADS * IDX_DIM) ** -0.5)).T


def _keyprep(h_small, cos_t, sin_t):
    nrows = h_small.shape[0]
    rows = _largest_divisor(nrows, (256, 128))
    kv_w = N_KV * HEAD_DIM
    inspec = lambda w, c: pl.BlockSpec((rows, w), lambda i: (i, c))
    rowspec = lambda w: pl.BlockSpec((rows, w), lambda i: (i, 0))
    colspec = lambda w: pl.BlockSpec((w, rows), lambda i: (0, i))
    bf, f32, sds = jnp.bfloat16, jnp.float32, jax.ShapeDtypeStruct
    return pl.pallas_call(
        _keyprep_kernel,
        grid=(nrows // rows,),
        in_specs=[inspec(kv_w, 0), inspec(kv_w, 1), inspec(IDX_DIM, 2 * kv_w // IDX_DIM),
                  inspec(LANES, 2 * kv_w // LANES + 1), inspec(HEAD_DIM, 0), inspec(HEAD_DIM, 0)],
        out_specs=[rowspec(kv_w), rowspec(kv_w), colspec(N_KV * VT_ROWS), rowspec(IDX_DIM), rowspec(IDX_DIM),
                   colspec(LANES), pl.BlockSpec((1, 8, LANES), lambda i: (i, 0, 0))],
        out_shape=[sds((nrows, kv_w), f32), sds((nrows, kv_w), bf), sds((N_KV * VT_ROWS, nrows), bf),
                   sds((nrows, IDX_DIM), f32), sds((nrows, IDX_DIM), bf), sds((LANES, nrows), f32),
                   sds((nrows // rows, 8, LANES), f32)],
        compiler_params=_params(("arbitrary",)),
        name="key_prep",
    )(h_small, h_small, h_small, h_small, cos_t, sin_t)


def _prep_kernel(qi_ref, q_ref, k_ref, v_ref, ki_ref, wi_ref, cos_ref, sin_ref,
                 qir_ref, qr_ref, kr_ref, krb_ref, vb_ref, kir_ref, kirb_ref, wis_ref, *norm_refs,
                 n_heads, transposed):
    cos = cos_ref[...]
    sin = sin_ref[...]

    def put(ref, h, width, val):
        if transposed:
            ref[h * width:(h + 1) * width, :] = val.T.astype(ref.dtype)
        else:
            ref[:, h * width:(h + 1) * width] = val.astype(ref.dtype)

    for h in range(IDX_HEADS):
        put(qir_ref, h, IDX_DIM, _rope(qi_ref[:, h * IDX_DIM:(h + 1) * IDX_DIM], cos, sin))
    q_scale = HEAD_DIM ** -0.5 * LOG2_E
    for h in range(n_heads):
        qs = _rope(q_ref[:, h * HEAD_DIM:(h + 1) * HEAD_DIM], cos, sin) * q_scale
        put(qr_ref, h, HEAD_DIM, qs)
        if transposed:
            qb = qs.T.astype(qr_ref.dtype).astype(jnp.float32)
            norm_refs[0][h:h + 1, :] = jnp.sqrt(jnp.sum(qb * qb, axis=0, keepdims=True)) * NORM_MARGIN
    lane = lax.broadcasted_iota(jnp.int32, (8, LANES), 1)
    kn = jnp.zeros((8, LANES), jnp.float32)
    for h in range(N_KV):
        cs = slice(h * HEAD_DIM, (h + 1) * HEAD_DIM)
        kr = _rope(k_ref[:, cs], cos, sin)
        kr_ref[:, cs] = kr
        krb_ref[:, cs] = kr.astype(krb_ref.dtype)
        if transposed:
            vb_ref[h * VT_ROWS:h * VT_ROWS + HEAD_DIM, :] = v_ref[:, cs].T.astype(vb_ref.dtype)
            vb_ref[h * VT_ROWS + HEAD_DIM:(h + 1) * VT_ROWS, :] = jnp.ones((ONES_ROWS, v_ref.shape[0]), vb_ref.dtype)
            kb16 = kr.astype(krb_ref.dtype).astype(jnp.float32)
            kmax = jnp.sqrt(jnp.max(jnp.sum(kb16 * kb16, axis=1, keepdims=True), axis=0, keepdims=True))
            kn = jnp.where(lane == h, kmax, kn)
        else:
            vb_ref[:, cs] = v_ref[:, cs].astype(vb_ref.dtype)
    if transposed:
        norm_refs[1][0] = kn
    kir = _rope(ki_ref[...], cos, sin)
    kir_ref[...] = kir
    kirb_ref[...] = kir.astype(kirb_ref.dtype)
    put(wis_ref, 0, LANES, wi_ref[...] * ((IDX_HEADS * IDX_DIM) ** -0.5))


def _prep(h_main, h_small, cos_t, sin_t, d_b, q_col, *, row0, nrows, transposed):
    rows = _largest_divisor(nrows, (256, 128))
    rb0 = row0 // rows
    n_heads = d_b // HEAD_DIM
    idx_w = IDX_HEADS * IDX_DIM
    kv_w = N_KV * HEAD_DIM
    inspec = lambda w, c: pl.BlockSpec((rows, w), lambda i: (rb0 + i, c))
    bf, f32 = jnp.bfloat16, jnp.float32
    outs = [(idx_w, bf, True), (d_b, bf, True), (kv_w, f32, False), (kv_w, bf, False), (kv_w, bf, True),
            (IDX_DIM, f32, False), (IDX_DIM, bf, False), (LANES, f32, True)]
    out_specs, out_shape = [], []
    for idx, (w, dt, feature_major) in enumerate(outs):
        if transposed and feature_major:
            w = N_KV * VT_ROWS if idx == 4 else w
            out_specs.append(pl.BlockSpec((w, rows), lambda i: (0, i)))
            out_shape.append(jax.ShapeDtypeStruct((w, nrows), dt))
        else:
            out_specs.append(pl.BlockSpec((rows, w), lambda i: (i, 0)))
            out_shape.append(jax.ShapeDtypeStruct((nrows, w), dt))
    if transposed:
        out_specs += [pl.BlockSpec((n_heads, rows), lambda i: (0, i)), pl.BlockSpec((1, 8, LANES), lambda i: (i, 0, 0))]
        out_shape += [jax.ShapeDtypeStruct((n_heads, nrows), f32), jax.ShapeDtypeStruct((nrows // rows, 8, LANES), f32)]
    return pl.pallas_call(
        functools.partial(_prep_kernel, n_heads=n_heads, transposed=transposed),
        grid=(nrows // rows,),
        in_specs=[inspec(idx_w, 0), inspec(d_b, q_col),
                  inspec(kv_w, 0), inspec(kv_w, 1), inspec(IDX_DIM, 2 * kv_w // IDX_DIM),
                  inspec(LANES, 2 * kv_w // LANES + 1), inspec(HEAD_DIM, 0), inspec(HEAD_DIM, 0)],
        out_specs=out_specs,
        out_shape=out_shape,
        compiler_params=_params(("arbitrary",)),
        name="rope_prep",
    )(h_main, h_main, h_small, h_small, h_small, h_small, cos_t, sin_t)


def _bisect_start(mn, mx, nf, k_sel):
    return mn, mx, mn, jnp.where(nf <= float(k_sel), 1.0, 0.0), jnp.zeros_like(mn)


def _bisect_step(state, mid, cnt, kk):
    lo, hi, thr, done, tie = state
    active = done < 0.5
    hit = jnp.logical_and(active, cnt == kk)
    no_room = jnp.logical_or(mid <= lo, mid >= hi)
    stuck = jnp.logical_and(active, jnp.logical_and(no_room, cnt != kk))
    thr = jnp.where(hit, mid, thr)
    tie = jnp.where(stuck, 1.0, tie)
    lo = jnp.where(jnp.logical_and(active, cnt > kk), mid, lo)
    hi = jnp.where(jnp.logical_and(active, cnt < kk), mid, hi)
    done = jnp.where(jnp.logical_or(hit, stuck), 1.0, done)
    return lo, hi, thr, done, tie


def _kth_threshold(count, mn, mx, nf, k_sel, start=None):
    kk = jnp.minimum(nf, float(k_sel))

    def cond(st):
        return st[5] > 0.5

    def body(st):
        lo, hi = st[0], st[1]
        mid = 0.5 * lo + 0.5 * hi
        new = _bisect_step(st[:5], mid, count(mid, False), kk)
        return new + (jnp.sum(1.0 - new[3]),)

    start = _bisect_start(mn, mx, nf, k_sel) if start is None else start
    lo, hi, thr, _, tie, _ = lax.while_loop(cond, body, tuple(start) + (jnp.sum(1.0 - start[3]),))
    any_tie = jnp.sum(tie) > 0.5

    def resolve(_):
        is_tie = tie > 0.5
        kth = jnp.where(count(hi, False) >= kk, hi, lo)
        thr_t = jnp.where(is_tie, kth, thr)
        taken = jnp.where(is_tie, kk - count(thr_t, True), TAKE_ALL)
        return thr_t, taken

    thr, taken = lax.cond(any_tie, resolve, lambda _: (thr, jnp.full_like(thr, TAKE_ALL)), 0)
    return thr, taken, any_tie


def _index_kernel(nkb_ref, qi_ref, wi_ref, qc_ref, kit_ref, kc_ref, s_ref, thr_ref, taken_ref, tie_ref, wb_ref,
                  *, tq, kb, sk, k_sel):
    b = pl.program_id(0)
    i = pl.program_id(1)
    nkb = nkb_ref[b * pl.num_programs(1) + i]
    lane_tiles = kb // LANES
    rt = min(tq, 128)

    for h in range(IDX_HEADS):
        wb_ref[h] = jnp.broadcast_to(wi_ref[:, h:h + 1], (tq, LANES))

    def score_tile(j, carry):
        off = pl.multiple_of(j * kb, kb)
        kt = kit_ref[:, pl.ds(off, kb)]
        kc = kc_ref[:, pl.ds(off, kb)]
        for r0 in range(0, tq, rt):
            accs = [jnp.zeros((rt, LANES), jnp.float32) for _ in range(lane_tiles)]
            for h in range(IDX_HEADS):
                r = jnp.dot(qi_ref[r0:r0 + rt, h * IDX_DIM:(h + 1) * IDX_DIM], kt,
                            preferred_element_type=jnp.float32)
                w = wb_ref[h, r0:r0 + rt, :]
                for l in range(lane_tiles):
                    accs[l] = accs[l] + jnp.maximum(r[:, l * LANES:(l + 1) * LANES], 0.0) * w
            acc = jnp.concatenate(accs, axis=1)
            adm = kc <= qc_ref[r0:r0 + rt, :]
            s_ref[r0:r0 + rt, pl.ds(off, kb)] = jnp.where(adm, acc, NEG_INF)
        return carry

    lax.fori_loop(0, nkb, score_tile, 0)

    def fill_tile(j, carry):
        off = pl.multiple_of(j * kb, kb)
        s_ref[:, pl.ds(off, kb)] = jnp.full((tq, kb), NEG_INF, jnp.float32)
        return carry

    lax.fori_loop(nkb, sk // kb, fill_tile, 0)

    def stats_tile(j, carry):
        mx, mn, nf = carry
        off = pl.multiple_of(j * kb, kb)
        for l in range(lane_tiles):
            s = s_ref[:, pl.ds(off + l * LANES, LANES)]
            fin = s > NEG_INF
            mx = jnp.maximum(mx, s)
            mn = jnp.minimum(mn, jnp.where(fin, s, POS_INF))
            nf = nf + jnp.where(fin, 1.0, 0.0)
        return mx, mn, nf

    mx, mn, nf = lax.fori_loop(
        0, nkb, stats_tile,
        (jnp.full((tq, LANES), NEG_INF, jnp.float32), jnp.full((tq, LANES), POS_INF, jnp.float32),
         jnp.zeros((tq, LANES), jnp.float32)))
    mx = jnp.max(mx, axis=1, keepdims=True)
    mn = jnp.min(mn, axis=1, keepdims=True)
    nf = jnp.sum(nf, axis=1, keepdims=True)

    def count(v, strict):
        def tile(j, cnt):
            off = pl.multiple_of(j * kb, kb)
            for l in range(lane_tiles):
                s = s_ref[:, pl.ds(off + l * LANES, LANES)]
                cnt = cnt + jnp.where(s > v if strict else s >= v, 1.0, 0.0)
            return cnt
        cnt = lax.fori_loop(0, nkb, tile, jnp.zeros((tq, LANES), jnp.float32))
        return jnp.sum(cnt, axis=1, keepdims=True)

    thr, taken, any_tie = _kth_threshold(count, mn, mx, nf, k_sel)
    thr_ref[...] = jnp.broadcast_to(thr, (tq, LANES))
    taken_ref[...] = jnp.broadcast_to(taken, (tq, LANES))
    tie_ref[b * pl.num_programs(1) + i] = jnp.where(any_tie, 1, 0).astype(jnp.int32)


def _index_scores(qir, wis, qc, kit, kc, nkb, *, row0, batches, nq, tq, kb, k_sel):
    sk = kit.shape[-1]
    rb0 = row0 // tq
    grid_spec = pltpu.PrefetchScalarGridSpec(
        num_scalar_prefetch=1,
        grid=(batches, nq),
        in_specs=[pl.BlockSpec((tq, IDX_HEADS * IDX_DIM), lambda b, i, n: (rb0 + b * nq + i, 0)),
                  pl.BlockSpec((tq, LANES), lambda b, i, n: (rb0 + b * nq + i, 0)),
                  pl.BlockSpec((tq, 1), lambda b, i, n: (rb0 + b * nq + i, 0)),
                  pl.BlockSpec((None, IDX_DIM, sk), lambda b, i, n: (b, 0, 0)),
                  pl.BlockSpec((None, 1, sk), lambda b, i, n: (b, 0, 0))],
        out_specs=[pl.BlockSpec((None, tq, sk), lambda b, i, n: (b, i, 0)),
                   pl.BlockSpec((None, tq, LANES), lambda b, i, n: (b, i, 0)),
                   pl.BlockSpec((None, tq, LANES), lambda b, i, n: (b, i, 0)),
                   pl.BlockSpec(memory_space=pltpu.SMEM)],
        scratch_shapes=[pltpu.VMEM((IDX_HEADS, tq, LANES), jnp.float32)],
    )
    return pl.pallas_call(
        functools.partial(_index_kernel, tq=tq, kb=kb, sk=sk, k_sel=k_sel),
        grid_spec=grid_spec,
        out_shape=[jax.ShapeDtypeStruct((batches, nq * tq, sk), jnp.float32),
                   jax.ShapeDtypeStruct((batches, nq * tq, LANES), jnp.float32),
                   jax.ShapeDtypeStruct((batches, nq * tq, LANES), jnp.float32),
                   jax.ShapeDtypeStruct((batches * nq,), jnp.int32)],
        compiler_params=_params(("arbitrary", "arbitrary")),
        name="index_select",
    )(nkb, qir, wis, qc, kit, kc)


def _fold8(x, op):
    parts = [x[g * 8:(g + 1) * 8, :] for g in range(x.shape[0] // 8)]
    while len(parts) > 1:
        parts = [op(parts[a], parts[a + 1]) for a in range(0, len(parts) - 1, 2)] + (parts[-1:] if len(parts) % 2 else [])
    return parts[0]


def _index_kernel_t(qit_ref, wit_ref, ki_ref, st_ref, thr_ref, taken_ref, tie_ref, *, tq, kb, k_sel):
    i = pl.program_id(0)
    nkb = lax.div((i + 1) * tq + (kb - 1), kb)
    total = st_ref.shape[0] // kb
    qchunk = lax.shift_right_logical(i * tq + lax.broadcasted_iota(jnp.int32, (kb, tq), 1), CHUNK_SHIFT)
    krow = lax.broadcasted_iota(jnp.int32, (kb, tq), 0)

    def score_tile(j, carry):
        mx, mn, nf = carry
        off = pl.multiple_of(j * kb, kb)
        k_t = ki_ref[pl.ds(off, kb), :]
        acc = jnp.zeros((kb, tq), jnp.float32)
        for h in range(IDX_HEADS):
            r = jnp.dot(k_t, qit_ref[h * IDX_DIM:(h + 1) * IDX_DIM, :], preferred_element_type=jnp.float32)
            acc = acc + jnp.maximum(r, 0.0) * wit_ref[h:h + 1, :]
        adm = lax.shift_right_logical(off + krow, CHUNK_SHIFT) <= qchunk
        s = jnp.where(adm, acc, NEG_INF)
        st_ref[pl.ds(off, kb), :] = s
        fin = s > NEG_INF
        mx = jnp.maximum(mx, _fold8(s, jnp.maximum))
        mn = jnp.minimum(mn, _fold8(jnp.where(fin, s, POS_INF), jnp.minimum))
        nf = nf + _fold8(jnp.where(fin, 1.0, 0.0), jnp.add)
        return mx, mn, nf

    mx, mn, nf = lax.fori_loop(
        0, nkb, score_tile,
        (jnp.full((8, tq), NEG_INF, jnp.float32), jnp.full((8, tq), POS_INF, jnp.float32),
         jnp.zeros((8, tq), jnp.float32)))

    def fill_tile(j, carry):
        off = pl.multiple_of(j * kb, kb)
        st_ref[pl.ds(off, kb), :] = jnp.full((kb, tq), NEG_INF, jnp.float32)
        return carry

    lax.fori_loop(nkb, total, fill_tile, 0)
    mx = jnp.max(mx, axis=0, keepdims=True)
    mn = jnp.min(mn, axis=0, keepdims=True)
    nf = jnp.sum(nf, axis=0, keepdims=True)

    def count(v, strict):
        def tile(j, cnt):
            off = pl.multiple_of(j * kb, kb)
            s = st_ref[pl.ds(off, kb), :]
            return cnt + _fold8(jnp.where(s > v if strict else s >= v, 1.0, 0.0), jnp.add)
        cnt = lax.fori_loop(0, nkb, tile, jnp.zeros((8, tq), jnp.float32))
        return jnp.sum(cnt, axis=0, keepdims=True)

    thr, taken, any_tie = _kth_threshold(count, mn, mx, nf, k_sel)
    thr_ref[...] = jnp.broadcast_to(thr, (8, tq))
    taken_ref[...] = jnp.broadcast_to(taken, (8, tq))
    tie_ref[i] = jnp.where(any_tie, 1, 0).astype(jnp.int32)


def _index_scores_t(qirt, wist, kirb, *, tq, kb, k_sel):
    idx_w, seq = qirt.shape
    nq = seq // tq
    return pl.pallas_call(
        functools.partial(_index_kernel_t, tq=tq, kb=kb, k_sel=k_sel),
        grid=(nq,),
        in_specs=[pl.BlockSpec((idx_w, tq), lambda i: (0, i)),
                  pl.BlockSpec((LANES, tq), lambda i: (0, i)),
                  pl.BlockSpec((seq, IDX_DIM), lambda i: (0, 0))],
        out_specs=[pl.BlockSpec((seq, tq), lambda i: (0, i)),
                   pl.BlockSpec((8, tq), lambda i: (0, i)),
                   pl.BlockSpec((8, tq), lambda i: (0, i)),
                   pl.BlockSpec(memory_space=pltpu.SMEM)],
        out_shape=[jax.ShapeDtypeStruct((seq, seq), jnp.float32),
                   jax.ShapeDtypeStruct((8, seq), jnp.float32),
                   jax.ShapeDtypeStruct((8, seq), jnp.float32),
                   jax.ShapeDtypeStruct((nq,), jnp.int32)],
        compiler_params=_params(("arbitrary",)),
        name="index_select_t",
    )(qirt, wist, kirb)


def _tie_rank_matrix(n, lower):
    rows = lax.broadcasted_iota(jnp.int32, (n, n), 0)
    cols = lax.broadcasted_iota(jnp.int32, (n, n), 1)
    return jnp.where(cols < rows if lower else rows < cols, 1.0, 0.0).astype(jnp.bfloat16)


def _attn_kernel_t(qi_ref, kj_ref, nk_ref, tie_ref, qt_ref, zb_ref, k_ref, vt_ref, st_ref, thr_ref, taken_ref,
                   qn_ref, kn_ref, o_ref, m_ref, c_ref, acc_ref, s_ref, p_ref, bias_ref, seen_ref, *, n_heads):
    p = pl.program_id(0)
    kj = kj_ref[p]
    hpg = n_heads // N_KV

    @pl.when(kj == 0)
    def _():
        m_ref[...] = jnp.full(m_ref.shape, NEG_INF, jnp.float32)
        c_ref[...] = jnp.zeros(c_ref.shape, jnp.float32)
        acc_ref[...] = jnp.zeros(acc_ref.shape, jnp.float32)
        seen_ref[...] = jnp.zeros(seen_ref.shape, jnp.float32)

    kb = st_ref.shape[0]
    rc = ATTN_ROW_CHUNK
    thr = thr_ref[0:1, :]
    for r in range(0, kb, rc):
        bias_ref[r:r + rc, :] = jnp.where(st_ref[r:r + rc, :] >= thr, 0.0, NEG_INF)

    @pl.when(tie_ref[qi_ref[p]] != 0)
    def _():
        taken = taken_ref[0:1, :]
        ltri = _tie_rank_matrix(RANK_CHUNK, True)
        seen = seen_ref[...]
        for r in range(0, kb, RANK_CHUNK):
            s = st_ref[r:r + RANK_CHUNK, :]
            eq = s == thr
            rank = jnp.dot(ltri, jnp.where(eq, 1.0, 0.0).astype(jnp.bfloat16),
                           preferred_element_type=jnp.float32) + seen
            take = jnp.logical_or(s > thr, jnp.logical_and(eq, rank < taken))
            bias_ref[r:r + RANK_CHUNK, :] = jnp.where(take, 0.0, NEG_INF)
            seen = seen + jnp.sum(jnp.where(eq, 1.0, 0.0), axis=0, keepdims=True)
        seen_ref[...] = seen

    def logits(h, r):
        g = h // hpg
        return jnp.dot(k_ref[r:r + rc, g * HEAD_DIM:(g + 1) * HEAD_DIM], qt_ref[h * HEAD_DIM:(h + 1) * HEAD_DIM, :],
                       preferred_element_type=jnp.float32) + bias_ref[r:r + rc, :]

    def weighted_values(h):
        g = h // hpg
        return jnp.dot(vt_ref[g * VT_ROWS:(g + 1) * VT_ROWS, :], p_ref[h], preferred_element_type=jnp.float32)

    first = kj == 0
    shift0 = [jnp.where(m_ref[h] == NEG_INF, 0.0, m_ref[h]) for h in range(n_heads)]
    kn = kn_ref[0]
    for blk in range(1, kn_ref.shape[0]):
        kn = jnp.maximum(kn, kn_ref[blk])
    excess = None
    for h in range(n_heads):
        bound = qn_ref[h:h + 1, :] * kn[0:1, h // hpg:h // hpg + 1]
        e = bound - shift0[h]
        excess = e if excess is None else jnp.maximum(excess, e)
    in_range = jnp.max(excess) <= EXP_RANGE

    @pl.when(in_range)
    def _():
        for h in range(n_heads):
            cm = None
            for r in range(0, kb, rc):
                t = logits(h, r) - shift0[h]
                p_ref[h, r:r + rc, :] = jnp.exp2(t).astype(p_ref.dtype)
                c8 = _fold8(t, jnp.maximum)
                cm = c8 if cm is None else jnp.maximum(cm, c8)
            m_ref[h] = jnp.maximum(m_ref[h], shift0[h] + jnp.max(cm, axis=0, keepdims=True))
        for h in range(n_heads):
            alpha = jnp.where(first, 1.0, jnp.exp2(c_ref[h] - shift0[h]))
            acc_ref[h] = alpha * acc_ref[h] + weighted_values(h)
            c_ref[h] = shift0[h]

    @pl.when(jnp.logical_not(in_range))
    def _():
        cmax = []
        for h in range(n_heads):
            cm = None
            for r in range(0, kb, rc):
                s = logits(h, r)
                s_ref[h, r:r + rc, :] = s
                c8 = _fold8(s, jnp.maximum)
                cm = c8 if cm is None else jnp.maximum(cm, c8)
            cmax.append(jnp.max(cm, axis=0, keepdims=True))
        for h in range(n_heads):
            m_cur = jnp.maximum(m_ref[h], cmax[h])
            shift = jnp.where(m_cur == NEG_INF, 0.0, m_cur)
            for r in range(0, kb, rc):
                p_ref[h, r:r + rc, :] = jnp.exp2(s_ref[h, r:r + rc, :] - shift).astype(p_ref.dtype)
            alpha = jnp.where(first, 1.0, jnp.exp2(c_ref[h] - shift))
            acc_ref[h] = alpha * acc_ref[h] + weighted_values(h)
            c_ref[h] = shift
            m_ref[h] = m_cur

    @pl.when(kj == nk_ref[p] - 1)
    def _():
        for h in range(n_heads):
            cs = slice(h * HEAD_DIM, (h + 1) * HEAD_DIM)
            o = (acc_ref[h, 0:HEAD_DIM, :] / acc_ref[h, HEAD_DIM:HEAD_DIM + 1, :]).T
            o_ref[:, cs] = (o * _silu(zb_ref[:, cs])).astype(o_ref.dtype)


def _masked_attention_t(qrt, h_main, zb_col, krb, vt, st, thr, taken, ties, qn, kn, steps, *, tq, kb):
    qidx, kidx, nk = steps
    d_b, seq = qrt.shape
    n_heads = d_b // HEAD_DIM
    kv_w = N_KV * HEAD_DIM
    kn_blocks = kb * kn.shape[0] // seq
    assert kb % RANK_CHUNK == 0 and kb % ATTN_ROW_CHUNK == 0 and kn_blocks >= 1
    grid_spec = pltpu.PrefetchScalarGridSpec(
        num_scalar_prefetch=4,
        grid=(qidx.shape[0],),
        in_specs=[pl.BlockSpec((d_b, tq), lambda p, q, k, n, t: (0, q[p])),
                  pl.BlockSpec((tq, d_b), lambda p, q, k, n, t: (q[p], zb_col)),
                  pl.BlockSpec((kb, kv_w), lambda p, q, k, n, t: (k[p], 0)),
                  pl.BlockSpec((N_KV * VT_ROWS, kb), lambda p, q, k, n, t: (0, k[p])),
                  pl.BlockSpec((kb, tq), lambda p, q, k, n, t: (k[p], q[p])),
                  pl.BlockSpec((8, tq), lambda p, q, k, n, t: (0, q[p])),
                  pl.BlockSpec((8, tq), lambda p, q, k, n, t: (0, q[p])),
                  pl.BlockSpec((n_heads, tq), lambda p, q, k, n, t: (0, q[p])),
                  pl.BlockSpec((kn_blocks, 8, LANES), lambda p, q, k, n, t: (k[p], 0, 0))],
        out_specs=pl.BlockSpec((tq, d_b), lambda p, q, k, n, t: (q[p], 0)),
        scratch_shapes=[pltpu.VMEM((n_heads, 1, tq), jnp.float32),
                        pltpu.VMEM((n_heads, 1, tq), jnp.float32),
                        pltpu.VMEM((n_heads, VT_ROWS, tq), jnp.float32),
                        pltpu.VMEM((n_heads, kb, tq), jnp.float32),
                        pltpu.VMEM((n_heads, kb, tq), jnp.bfloat16),
                        pltpu.VMEM((kb, tq), jnp.float32),
                        pltpu.VMEM((1, tq), jnp.float32)],
    )
    return pl.pallas_call(
        functools.partial(_attn_kernel_t, n_heads=n_heads),
        grid_spec=grid_spec,
        out_shape=jax.ShapeDtypeStruct((seq, d_b), jnp.bfloat16),
        compiler_params=_params(("arbitrary",)),
        name="masked_attention_t",
    )(qidx, kidx, nk, ties, qrt, h_main, krb, vt, st, thr, taken, qn, kn)


def _attn_kernel(tie_ref, q_ref, zb_ref, kt_ref, v_ref, s_ref, thr_ref, taken_ref, o_ref, bias_ref, *, n_heads):
    b = pl.program_id(0)
    hpg = n_heads // N_KV
    sk = s_ref.shape[1]
    thr = thr_ref[:, 0:1]
    bias_ref[...] = jnp.where(s_ref[...] >= thr, 0.0, NEG_INF)

    @pl.when(tie_ref[b] != 0)
    def _():
        taken = taken_ref[:, 0:1]
        utri = _tie_rank_matrix(RANK_CHUNK, False)
        seen = jnp.zeros_like(thr)
        for c in range(0, sk, RANK_CHUNK):
            s = s_ref[:, c:c + RANK_CHUNK]
            eq = s == thr
            rank = jnp.dot(jnp.where(eq, 1.0, 0.0).astype(jnp.bfloat16), utri,
                           preferred_element_type=jnp.float32) + seen
            take = jnp.logical_or(s > thr, jnp.logical_and(eq, rank < taken))
            bias_ref[:, c:c + RANK_CHUNK] = jnp.where(take, 0.0, NEG_INF)
            seen = seen + jnp.sum(jnp.where(eq, 1.0, 0.0), axis=1, keepdims=True)

    for h in range(n_heads):
        g = h // hpg
        cs = slice(h * HEAD_DIM, (h + 1) * HEAD_DIM)
        s = jnp.dot(q_ref[:, cs], kt_ref[g], preferred_element_type=jnp.float32) + bias_ref[...]
        pr = jnp.exp2(s - jnp.max(s, axis=1, keepdims=True))
        o = jnp.dot(pr.astype(v_ref.dtype), v_ref[:, g * HEAD_DIM:(g + 1) * HEAD_DIM],
                    preferred_element_type=jnp.float32) / jnp.sum(pr, axis=1, keepdims=True)
        o_ref[:, cs] = (o * _silu(zb_ref[:, cs])).astype(o_ref.dtype)


def _masked_attention(qr, h_main, zb_col, kt, vb, scores, thr, taken, ties, *, zrow0, tq, d_b):
    batches, _, _, sk = kt.shape
    zrb0 = zrow0 // tq
    n_heads = d_b // HEAD_DIM
    kv_w = N_KV * HEAD_DIM
    assert sk % RANK_CHUNK == 0
    grid_spec = pltpu.PrefetchScalarGridSpec(
        num_scalar_prefetch=1,
        grid=(batches,),
        in_specs=[pl.BlockSpec((tq, d_b), lambda b, t: (b, 0)),
                  pl.BlockSpec((tq, d_b), lambda b, t: (zrb0 + b, zb_col)),
                  pl.BlockSpec((None, N_KV, HEAD_DIM, sk), lambda b, t: (b, 0, 0, 0)),
                  pl.BlockSpec((None, sk, kv_w), lambda b, t: (b, 0, 0)),
                  pl.BlockSpec((None, tq, sk), lambda b, t: (b, 0, 0)),
                  pl.BlockSpec((None, tq, LANES), lambda b, t: (b, 0, 0)),
                  pl.BlockSpec((None, tq, LANES), lambda b, t: (b, 0, 0))],
        out_specs=pl.BlockSpec((tq, d_b), lambda b, t: (b, 0)),
        scratch_shapes=[pltpu.VMEM((tq, sk), jnp.float32)],
    )
    return pl.pallas_call(
        functools.partial(_attn_kernel, n_heads=n_heads),
        grid_spec=grid_spec,
        out_shape=jax.ShapeDtypeStruct((batches * tq, d_b), jnp.bfloat16),
        compiler_params=_params(("arbitrary",)),
        name="masked_attention",
    )(ties, qr, h_main, kt, vb, scores, thr, taken)


def _out_kernel(mixa_ref, mixb_ref, w_ref, x_ref, g_ref, o_ref, ssq_ref, *, d_a, tn):
    j = pl.program_id(1)

    @pl.when(j == 0)
    def _():
        ssq_ref[...] = jnp.zeros(ssq_ref.shape, jnp.float32)

    y = (x_ref[...] + jnp.dot(mixa_ref[...], w_ref[0:d_a, :], preferred_element_type=jnp.float32)
         + jnp.dot(mixb_ref[...], w_ref[d_a:, :], preferred_element_type=jnp.float32))
    o_ref[:, pl.ds(pl.multiple_of(j * tn, tn), tn)] = y
    ssq_ref[...] += jnp.sum(y * y, axis=-1, keepdims=True)

    @pl.when(j == pl.num_programs(1) - 1)
    def _():
        inv = lax.rsqrt(ssq_ref[...] * (1.0 / o_ref.shape[1]) + EPS)
        o_ref[...] = o_ref[...] * inv * g_ref[...]


def _out_proj(mix_a, mix_b, w, x, g):
    n, d = x.shape
    d_a, d_b = mix_a.shape[1], mix_b.shape[1]
    tm = _largest_divisor(n, (512, 256, 128))
    tn = _largest_divisor(d, (1024, 512, 256, 128))
    return pl.pallas_call(
        functools.partial(_out_kernel, d_a=d_a, tn=tn),
        grid=(n // tm, d // tn),
        in_specs=[pl.BlockSpec((tm, d_a), lambda i, j: (i, 0)),
                  pl.BlockSpec((tm, d_b), lambda i, j: (i, 0)),
                  pl.BlockSpec((d_a + d_b, tn), lambda i, j: (0, j)),
                  pl.BlockSpec((tm, tn), lambda i, j: (i, j)), pl.BlockSpec((1, d), lambda i, j: (0, 0))],
        out_specs=pl.BlockSpec((tm, d), lambda i, j: (i, 0)),
        out_shape=jax.ShapeDtypeStruct((n, d), jnp.float32),
        scratch_shapes=[pltpu.VMEM((tm, 1), jnp.float32)],
        compiler_params=_params(("arbitrary", "arbitrary")),
        name="out_proj",
    )(mix_a, mix_b, w, x, g.reshape(1, d))


def _cdiv(a, b):
    return -(-a // b)


def _attention_steps(nq, tq, kb):
    q_l, k_l, n_l = [], [], []
    for q in range(nq):
        nk = _cdiv((q + 1) * tq, kb)
        for k in range(nk):
            q_l.append(q), k_l.append(k), n_l.append(nk)
    return tuple(jnp.asarray(np.asarray(v, np.int32)) for v in (q_l, k_l, n_l))


def _layer(x_prompt, x_sample, cache_k, cache_v, cache_ki, g_norm, w_in, w_s, b_s, g_v, w_out, final_g):
    f32, bf = jnp.float32, jnp.bfloat16
    _, seq, d_model = x_prompt.shape
    dec_b, dec_t, _ = x_sample.shape
    past = cache_k.shape[1]
    d_a = d_model // 2
    d_b = d_model - d_a
    groups = d_a // HEAD_DIM
    kv_w = N_KV * HEAD_DIM
    idx_w = IDX_HEADS * IDX_DIM
    n_s = dec_b * dec_t
    n_all = seq + n_s
    assert seq % 512 == 0 and n_s % MLP_CHUNK == 0 and dec_t <= CHUNK and past % MLP_CHUNK == 0
    assert idx_w % d_a == 0 and d_a % HEAD_DIM == 0

    x_p, x_s = x_prompt[0], x_sample.reshape(n_s, d_model)
    widths = (d_a, d_a, d_a, d_b, kv_w, kv_w, d_b, idx_w, IDX_DIM, IDX_HEADS)
    o_u, o_va, o_za, o_q, o_k, o_v, o_zb, o_qi, o_ki, o_wi = np.concatenate([[0], np.cumsum(widths)])[:-1].tolist()

    def col_blocks(segments, bw):
        return [(off + c) // bw for off, wd in segments for c in range(0, wd, bw)]

    w_in_t = w_in.T
    bw = int(np.gcd.reduce([d_a, d_b, kv_w, idx_w, o_q, o_zb, o_qi]))
    w_main = _regroup_cast(w_in_t, col_blocks(((o_qi, idx_w), (o_q, d_b), (o_u, d_a), (o_va, d_a), (o_za, d_a),
                                               (o_zb, d_b)), bw), bw, bf)
    assert o_k % LANES == 0 and o_ki % LANES == 0 and o_wi % LANES == 0
    w_small = _regroup_cast(w_in_t, col_blocks(((o_k, 2 * kv_w), (o_ki, IDX_DIM), (o_wi, LANES)), LANES), LANES, bf,
                            tail_valid=IDX_HEADS)
    q_col = idx_w // d_a
    u_col, zb_col = q_col + 1, q_col + 4

    half = HEAD_DIM // 2
    inv = ROPE_THETA ** (-2.0 * jnp.arange(half, dtype=f32) / HEAD_DIM)

    def rope_tables(pos):
        ang = pos.astype(f32)[:, None] * inv[None, :]
        return (jnp.concatenate([jnp.cos(ang), jnp.cos(ang)], axis=1),
                jnp.concatenate([-jnp.sin(ang), jnp.sin(ang)], axis=1))

    pos_s = past + jnp.tile(jnp.arange(dec_t), dec_b)
    cos_p, sin_p = rope_tables(jnp.arange(seq))
    cos_s, sin_s = rope_tables(pos_s)

    hr_p, xn_p = _in_proj(x_p, g_norm, w_main, f32, row0=idx_w + d_b, emit_xn=True)
    (qirt,) = _qproj(xn_p, w_main, 0, idx_w, cos_p, sin_p, scale=1.0, with_norms=False)
    qrt, qn_p = _qproj(xn_p, w_main, idx_w, d_b, cos_p, sin_p, scale=HEAD_DIM ** -0.5 * LOG2_E, with_norms=True)
    kr_p, krb_p, v_p, vt_p, kir_p, kirb_p, wist, kn_p = _kvproj(xn_p, w_small, cos_p, sin_p)
    hm_s, hs_s = _in_proj(x_s, g_norm, w_main, f32), _in_proj(x_s, g_norm, w_small, f32)

    pidx = np.arange(MLP_CHUNK)
    mask_p = (pidx[None, :] // CHUNK) <= (pidx[:, None] // CHUNK)
    wm_p = jnp.where(mask_p[None], w_s, 0.0).astype(bf)
    b_p = b_s[:, :, None]
    tidx = pidx % dec_t
    same = (pidx[None, :] // dec_t) == (pidx[:, None] // dec_t)
    mask_s = same & ((tidx[None, :] // CHUNK) <= (tidx[:, None] // CHUNK))
    wm_s = jnp.where(mask_s[None], w_s[:, tidx][:, :, tidx], 0.0).astype(bf)
    b_sm = b_s[:, tidx][:, :, None]
    gv = g_v.reshape(1, d_a)
    (mix_a_p,) = _mlp_group(hr_p, 0, d_a, 0, seq, 512, wm_p, b_p, gv, False)
    mix_a_s, vn_s = _mlp_group(hm_s, u_col, d_a, 0, n_s, MLP_CHUNK, wm_s, b_sm, gv, True)

    qir_s, qr_s, kr_s, krb_s, vb_s, kir_s, kirb_s, wis_s = _prep(
        hm_s, hs_s, cos_s, sin_s, d_b, q_col, row0=0, nrows=n_s, transposed=False)
    v_s_new = hs_s[:, kv_w:2 * kv_w]

    tq_p, kb_i, kb_a = PROMPT_TQ, PROMPT_KB_INDEX, PROMPT_KB_ATTN
    k_sel_p = min(TOPK_MAX, seq // 4)
    st_p, thr_p, taken_p, ties_p = _index_scores_t(qirt, wist, kirb_p, tq=tq_p, kb=kb_i, k_sel=k_sel_p)
    steps_p = _attention_steps(seq // tq_p, tq_p, kb_a)
    mix_b_p = _masked_attention_t(qrt, hr_p, 3, krb_p, vt_p, st_p, thr_p, taken_p, ties_p, qn_p, kn_p,
                                  steps_p, tq=tq_p, kb=kb_a)

    kb_s = SAMPLE_KB_INDEX
    n_keys = past + dec_t
    sk_s = _cdiv(n_keys, kb_s) * kb_s
    padk = lambda a: jnp.pad(a, ((0, 0), (0, sk_s - n_keys), (0, 0)))
    k_s = padk(jnp.concatenate([cache_k.reshape(dec_b, past, kv_w).astype(bf), krb_s.reshape(dec_b, dec_t, kv_w)], axis=1))
    v_s = padk(jnp.concatenate([cache_v.reshape(dec_b, past, kv_w).astype(bf), vb_s.reshape(dec_b, dec_t, kv_w)], axis=1))
    ki_s = padk(jnp.concatenate([cache_ki.astype(bf), kirb_s.reshape(dec_b, dec_t, IDX_DIM)], axis=1))
    kit_s = ki_s.transpose(0, 2, 1)
    kt_s = k_s.reshape(dec_b, sk_s, N_KV, HEAD_DIM).transpose(0, 2, 3, 1)
    kpos = jnp.arange(sk_s, dtype=jnp.int32)
    kc_s = jnp.broadcast_to(jnp.where(kpos < n_keys, kpos // CHUNK, FAR_CHUNK)[None, None, :], (dec_b, 1, sk_s))
    qc_s = (pos_s // CHUNK).astype(jnp.int32)[:, None]
    k_sel_s = min(TOPK_MAX, n_keys // 4)
    nkb_s = jnp.full((dec_b,), sk_s // kb_s, jnp.int32)
    s_s, thr_s, taken_s, ties_s = _index_scores(qir_s, wis_s, qc_s, kit_s, kc_s, nkb_s, row0=0, batches=dec_b, nq=1,
                                                tq=dec_t, kb=kb_s, k_sel=k_sel_s)
    mix_b_s = _masked_attention(qr_s, hm_s, zb_col, kt_s, v_s, s_s, thr_s, taken_s, ties_s,
                                zrow0=0, tq=dec_t, d_b=d_b)

    w_out_b = w_out.astype(bf)
    y_p = _out_proj(mix_a_p, mix_b_p, w_out_b, x_p, final_g)
    y_s = _out_proj(mix_a_s, mix_b_s, w_out_b, x_s, final_g)

    shp_p = (1, 1, seq, N_KV, HEAD_DIM)
    shp_s = (1, dec_b, dec_t, N_KV, HEAD_DIM)
    return (y_p[None], y_s.reshape(dec_b, dec_t, d_model),
            kr_p.reshape(shp_p), v_p.reshape(shp_p), kir_p.reshape(1, 1, seq, IDX_DIM),
            kr_s.reshape(shp_s), v_s_new.reshape(shp_s), kir_s.reshape(1, dec_b, dec_t, IDX_DIM),
            vn_s.reshape(1, dec_b, dec_t, groups, HEAD_DIM))


def kernel(x_prompt, x_sample, cache_k, cache_v, cache_idx_k, norm_g, w_in, w_s, b_s, v_norm_g, w_out, final_norm_g):
    assert x_prompt.shape[0] == 1 and norm_g.shape[0] == 1, "one prompt stream and one layer"
    return _layer(x_prompt, x_sample, cache_k[0], cache_v[0], cache_idx_k[0], norm_g[0], w_in[0], w_s[0], b_s[0],
                  v_norm_g[0], w_out[0], final_norm_g)
```
